```python
import jax, jax.numpy as jnp
from jax import lax
import numpy as np

D_MODEL = 1024
BATCH = 2
SEQ = 8192
DEPTH = 2

N_A_LAYERS = DEPTH // 2
N_B_LAYERS = DEPTH - N_A_LAYERS

EPS = 1e-6
PLE_DIM = 256
D_FF = ((8 * D_MODEL // 3 + 255) // 256) * 256

A_KEY_DIM = 128
A_HEADS = D_MODEL // A_KEY_DIM
A_VAL_DIM = D_MODEL // A_HEADS
A_WIDTH = A_HEADS * A_KEY_DIM
A_CHUNK = 64

B_HEADS = 16
B_NOPE = 128
B_ROPE = 64
B_VDIM = 128
B_Q_LORA = 512
B_KV_LORA = 256
Q_BLOCK = 128
ROPE_THETA = 10000.0

kernel_name = 'yoco_hgrn2_mla_macaron_ple'


def _rmsnorm(x, g):
    x32 = x.astype(jnp.float32)
    y = x32 * lax.rsqrt(jnp.mean(x32 * x32, axis=-1, keepdims=True) + EPS)
    return (y * g.astype(jnp.float32)).astype(x.dtype)


def _swiglu(h, w_in, w_out):
    gate, up = jnp.split(h @ w_in, 2, axis=-1)
    return (jax.nn.silu(gate) * up) @ w_out


def _rope(x, positions):
    d = x.shape[-1]
    inv_freq = 1.0 / (ROPE_THETA ** (jnp.arange(0, d, 2, dtype=jnp.float32) / d))
    ang = positions.astype(jnp.float32)[..., None] * inv_freq
    ang = ang.reshape(ang.shape[:2] + (1,) * (x.ndim - 3) + (d // 2,))
    cos, sin = jnp.cos(ang), jnp.sin(ang)
    x32 = x.astype(jnp.float32)
    x1, x2 = x32[..., : d // 2], x32[..., d // 2:]
    return jnp.concatenate([x1 * cos - x2 * sin, x2 * cos + x1 * sin], axis=-1).astype(x.dtype)


def _hgrn2_chunk(state, inp):
    q, k, v, g = inp
    b = jnp.cumsum(g, axis=2)
    o_inter = jnp.einsum('bhtk,bhkv->bhtv', q * jnp.exp(b), state)
    causal = jnp.tril(jnp.ones((A_CHUNK, A_CHUNK), dtype=bool))[:, :, None]
    diff = b[:, :, :, None, :] - b[:, :, None, :, :]
    decay = jnp.where(causal, jnp.exp(jnp.where(causal, diff, 0.0)), 0.0)
    scores = jnp.einsum('bhtk,bhsk,bhtsk->bhts', q, k, decay)
    o_intra = jnp.einsum('bhts,bhsv->bhtv', scores, v)
    b_last = b[:, :, -1:, :]
    new_state = jnp.exp(b_last[:, :, 0, :])[..., None] * state + jnp.einsum(
        'bhsk,bhsv->bhkv', k * jnp.exp(b_last - b), v)
    return new_state, o_inter + o_intra


def _hgrn2_mixer(h, w_in, lb, out_gain, w_out):
    bsz, seq, _ = h.shape
    q, f_logit, i, g = jnp.split(h @ w_in, 4, axis=-1)
    f_logit = f_logit.astype(jnp.float32)
    lb = lb.astype(jnp.float32)
    log_f = jnp.log(lb + (1.0 - lb) * jax.nn.sigmoid(f_logit))
    k = (1.0 - lb) * jax.nn.sigmoid(-f_logit)
    n_chunks = seq // A_CHUNK

    def to_chunks(t, d):
        t = t.astype(jnp.float32).reshape(bsz, n_chunks, A_CHUNK, A_HEADS, d)
        return t.transpose(1, 0, 3, 2, 4)

    qc = to_chunks(q, A_KEY_DIM) * (A_KEY_DIM ** -0.5)
    kc = to_chunks(k, A_KEY_DIM)
    vc = to_chunks(i, A_VAL_DIM)
    gc = to_chunks(log_f, A_KEY_DIM)
    s0 = jnp.zeros((bsz, A_HEADS, A_KEY_DIM, A_VAL_DIM), jnp.float32)
    _, o = lax.scan(_hgrn2_chunk, s0, (qc, kc, vc, gc))
    o = o.transpose(1, 0, 3, 2, 4).reshape(bsz, seq, A_HEADS, A_VAL_DIM)
    o = _rmsnorm(o, out_gain).reshape(bsz, seq, A_HEADS * A_VAL_DIM)
    o = o * jax.nn.sigmoid(g.astype(jnp.float32))
    return o.astype(h.dtype) @ w_out


def _mla_shared_kv(x, positions, norm_in, w_down, latent_norm, w_up):
    bsz, seq, _ = x.shape
    h = _rmsnorm(x, norm_in)
    c_kv, k_r = jnp.split(h @ w_down, [B_KV_LORA], axis=-1)
    c_kv = _rmsnorm(c_kv, latent_norm)
    kv = (c_kv @ w_up).reshape(bsz, seq, B_HEADS, B_NOPE + B_VDIM)
    k_nope, v = jnp.split(kv, [B_NOPE], axis=-1)
    k_rope = _rope(k_r, positions)
    return k_nope, k_rope, v


def _mla_mixer(h, positions, k_nope, k_rope, v, w_dq, q_norm, w_uq, w_out):
    bsz, seq, _ = h.shape
    c_q = _rmsnorm(h @ w_dq, q_norm)
    q = (c_q @ w_uq).reshape(bsz, seq, B_HEADS, B_NOPE + B_ROPE)
    q_nope, q_rope = jnp.split(q, [B_NOPE], axis=-1)
    q_rope = _rope(q_rope, positions)
    n_blocks = seq // Q_BLOCK
    scale = (B_NOPE + B_ROPE) ** -0.5

    def blocks(t):
        return t.reshape((bsz, n_blocks, Q_BLOCK) + t.shape[2:]).swapaxes(0, 1)

    key_idx = jnp.arange(seq)

    def attend(args):
        qn, qr, blk = args
        s = (jnp.einsum('bqhd,bkhd->bhqk', qn, k_nope)
             + jnp.einsum('bqhd,bkd->bhqk', qr, k_rope)).astype(jnp.float32) * scale
        q_idx = blk * Q_BLOCK + jnp.arange(Q_BLOCK)
        mask = q_idx[:, None] >= key_idx[None, :]
        s = jnp.where(mask[None, None], s, -jnp.inf)
        pr = jax.nn.softmax(s, axis=-1).astype(v.dtype)
        return jnp.einsum('bhqk,bkhd->bqhd', pr, v)

    o = lax.map(attend, (blocks(q_nope), blocks(q_rope), jnp.arange(n_blocks, dtype=jnp.int32)))
    o = o.swapaxes(0, 1).reshape(bsz, seq, B_HEADS * B_VDIM)
    return o @ w_out


def setup_inputs(seed: int = 0) -> dict:
    key = jax.random.key(seed)
    ks = jax.random.split(key, 24)
    f32 = jnp.float32

    def w(k, shape, fan_in):
        return jax.random.normal(k, shape, f32) * (fan_in ** -0.5)

    def gain(k, shape):
        return 1.0 + 0.02 * jax.random.normal(k, shape, f32)

    offset = jax.random.randint(ks[2], (BATCH, 1), 0, 1024, dtype=jnp.int32)
    positions = offset + jnp.arange(SEQ, dtype=jnp.int32)[None, :]
    return {
        'x': jax.random.normal(ks[0], (BATCH, SEQ, D_MODEL), f32),
        'p': jax.random.normal(ks[1], (DEPTH, BATCH, SEQ, PLE_DIM), f32),
        'positions': positions,
        'norm_gains': gain(ks[3], (DEPTH, 4, D_MODEL)),
        'ffn_w_in': w(ks[4], (DEPTH, 2, D_MODEL, 2 * D_FF), D_MODEL),
        'ffn_w_out': w(ks[5], (DEPTH, 2, D_FF, D_MODEL), D_FF),
        'ple_w_gate': w(ks[6], (DEPTH, D_MODEL, D_MODEL), D_MODEL),
        'ple_w_in': w(ks[7], (DEPTH, PLE_DIM, D_MODEL), PLE_DIM),
        'a_w_in': w(ks[8], (N_A_LAYERS, D_MODEL, 4 * A_WIDTH), D_MODEL),
        'a_lb_logits': 0.1 * jax.random.normal(ks[9], (N_A_LAYERS + 1, A_WIDTH), f32),
        'a_out_gain': gain(ks[10], (N_A_LAYERS, A_VAL_DIM)),
        'a_w_out': w(ks[11], (N_A_LAYERS, A_WIDTH, D_MODEL), A_WIDTH),
        'kv_norm_in': gain(ks[12], (D_MODEL,)),
        'kv_w_down': w(ks[13], (D_MODEL, B_KV_LORA + B_ROPE), D_MODEL),
        'kv_latent_norm': gain(ks[14], (B_KV_LORA,)),
        'kv_w_up': w(ks[15], (B_KV_LORA, B_HEADS * (B_NOPE + B_VDIM)), B_KV_LORA),
        'b_w_dq': w(ks[16], (N_B_LAYERS, D_MODEL, B_Q_LORA), D_MODEL),
        'b_q_norm': gain(ks[17], (N_B_LAYERS, B_Q_LORA)),
        'b_w_uq': w(ks[18], (N_B_LAYERS, B_Q_LORA, B_HEADS * (B_NOPE + B_ROPE)), B_Q_LORA),
        'b_w_out': w(ks[19], (N_B_LAYERS, B_HEADS * B_VDIM, D_MODEL), B_HEADS * B_VDIM),
        'final_norm': gain(ks[20], (D_MODEL,)),
    }


def reference(x, p, positions, norm_gains, ffn_w_in, ffn_w_out, ple_w_gate, ple_w_in,
              a_w_in, a_lb_logits, a_out_gain, a_w_out,
              kv_norm_in, kv_w_down, kv_latent_norm, kv_w_up,
              b_w_dq, b_q_norm, b_w_uq, b_w_out, final_norm):
    lower_bounds = jnp.cumsum(jax.nn.softmax(a_lb_logits.astype(jnp.float32), axis=0), axis=0)
    shared = None
    if N_A_LAYERS == 0:
        shared = _mla_shared_kv(x, positions, kv_norm_in, kv_w_down, kv_latent_norm, kv_w_up)
    for li in range(DEPTH):
        g = norm_gains[li]
        x = x + 0.5 * _swiglu(_rmsnorm(x, g[0]), ffn_w_in[li, 0], ffn_w_out[li, 0])
        h = _rmsnorm(x, g[1])
        if li < N_A_LAYERS:
            x = x + _hgrn2_mixer(h, a_w_in[li], lower_bounds[li], a_out_gain[li], a_w_out[li])
        else:
            bi = li - N_A_LAYERS
            k_nope, k_rope, v = shared
            x = x + _mla_mixer(h, positions, k_nope, k_rope, v,
                               b_w_dq[bi], b_q_norm[bi], b_w_uq[bi], b_w_out[bi])
        x = x + 0.5 * _swiglu(_rmsnorm(x, g[2]), ffn_w_in[li, 1], ffn_w_out[li, 1])
        gate = jax.nn.sigmoid((_rmsnorm(x, g[3]) @ ple_w_gate[li]).astype(jnp.float32))
        x = x + (gate * (p[li] @ ple_w_in[li]).astype(jnp.float32)).astype(x.dtype)
        if li == N_A_LAYERS - 1:
            shared = _mla_shared_kv(x, positions, kv_norm_in, kv_w_down, kv_latent_norm, kv_w_up)
    return _rmsnorm(x, final_norm)
```

```python
import functools
import math

import jax
import jax.numpy as jnp
from jax import lax
from jax.experimental import pallas as pl
from jax.experimental.pallas import tpu as pltpu

F32 = jnp.float32
BF16 = jnp.bfloat16

EPS = 1e-6
ROPE_THETA = 10000.0
A_CHUNK = 64
A_SUB = 16
NEG_BIG = -1e30

V7X_VMEM_BYTES = 64 * 1024 * 1024
VMEM_LIMIT = 56 * 1024 * 1024


def _rms(x, g):
    ms = jnp.mean(x * x, axis=-1, keepdims=True)
    return x * lax.rsqrt(ms + EPS) * g


def _sigmoid(x):
    return 1.0 / (1.0 + jnp.exp(-x))


def _dot(a, b):
    return jnp.dot(a, b, preferred_element_type=F32)


def _dot_nt(a, b):
    return lax.dot_general(a, b, (((1,), (1,)), ((), ())), preferred_element_type=F32)


def _dot_tn(a, b):
    return lax.dot_general(a, b, (((0,), (0,)), ((), ())), preferred_element_type=F32)


def _const_spec(shape):
    nd = len(shape)
    return pl.BlockSpec(shape, lambda *_: (0,) * nd, pipeline_mode=pl.Buffered(1))


def _params(n_axes):
    return pltpu.CompilerParams(
        dimension_semantics=("arbitrary",) * n_axes, vmem_limit_bytes=VMEM_LIMIT)


def _block_kernel(*refs, d_ff, has_pre, has_ple, has_final):
    it = iter(refs)
    x_ref = next(it)
    if has_pre:
        y_ref, wpre_ref = next(it), next(it)
    g_ref, win_ref, wout_ref = next(it), next(it), next(it)
    if has_ple:
        gp_ref, wg_ref, p_ref, wp_ref = next(it), next(it), next(it), next(it)
    if has_final:
        gf_ref = next(it)
    o_ref = next(it)

    x = x_ref[...]
    if has_pre:
        x = x + _dot(y_ref[...], wpre_ref[...])
    h = _rms(x, g_ref[...]).astype(BF16)
    gate = _dot(h, win_ref[:, :d_ff])
    up = _dot(h, win_ref[:, d_ff:])
    act = (gate * _sigmoid(gate) * up).astype(BF16)
    x = x + 0.5 * _dot(act, wout_ref[...])
    if has_ple:
        hg = _rms(x, gp_ref[...]).astype(BF16)
        emb_gate = _sigmoid(_dot(hg, wg_ref[...]))
        emb = _dot(p_ref[...].astype(BF16), wp_ref[...])
        x = x + emb_gate * emb
    if has_final:
        x = _rms(x, gf_ref[...])
    o_ref[...] = x


def _block_call(x, g, w_in, w_out, *, pre=None, ple=None, final=None, tm=256):
    t, d = x.shape
    d_ff = w_out.shape[0]
    row = lambda i: (i, 0)
    args, specs = [x], [pl.BlockSpec((tm, d), row)]
    if pre is not None:
        y, w_pre = pre
        args += [y, w_pre]
        specs += [pl.BlockSpec((tm, y.shape[1]), row), _const_spec(w_pre.shape)]
    args += [g, w_in, w_out]
    specs += [_const_spec(g.shape), _const_spec(w_in.shape), _const_spec(w_out.shape)]
    if ple is not None:
        gp, wg, p, wp = ple
        args += [gp, wg, p, wp]
        specs += [_const_spec(gp.shape), _const_spec(wg.shape),
                  pl.BlockSpec((tm, p.shape[1]), row), _const_spec(wp.shape)]
    if final is not None:
        args += [final]
        specs += [_const_spec(final.shape)]
    kern = functools.partial(_block_kernel, d_ff=d_ff, has_pre=pre is not None,
                             has_ple=ple is not None, has_final=final is not None)
    return pl.pallas_call(
        kern, grid=(t // tm,), in_specs=specs, out_specs=pl.BlockSpec((tm, d), row),
        out_shape=jax.ShapeDtypeStruct((t, d), F32), compiler_params=_params(1),
        name="token_block")(*args)


def _hgrn_proj_kernel(x_ref, g_ref, w_ref, lbl_ref, q_ref, k_ref, v_ref, lf_ref, sg_ref,
                      *, width, layer, q_scale):
    h = _rms(x_ref[...], g_ref[...]).astype(BF16)
    q = _dot(h, w_ref[:, 0 * width:1 * width])
    f = _dot(h, w_ref[:, 1 * width:2 * width])
    v = _dot(h, w_ref[:, 2 * width:3 * width])
    og = _dot(h, w_ref[:, 3 * width:4 * width])
    lbl = lbl_ref[...]
    e = jnp.exp(lbl - jnp.max(lbl, axis=0, keepdims=True))
    sm = e / jnp.sum(e, axis=0, keepdims=True)
    lb = jnp.sum(sm[:layer + 1], axis=0, keepdims=True)
    sf = _sigmoid(f)
    q_ref[...] = (q * q_scale).astype(BF16)
    k_ref[...] = ((1.0 - lb) * _sigmoid(-f)).astype(BF16)
    v_ref[...] = v.astype(BF16)
    lf_ref[...] = jnp.log(lb + (1.0 - lb) * sf)
    sg_ref[...] = _sigmoid(og).astype(BF16)


def _hgrn_proj_call(x, g, w_in, lb_logits, *, layer, key_dim, tm=256):
    t, d = x.shape
    width = w_in.shape[1] // 4
    row = lambda i: (i, 0)
    kern = functools.partial(_hgrn_proj_kernel, width=width, layer=layer,
                             q_scale=float(key_dim) ** -0.5)
    out_spec = pl.BlockSpec((tm, width), row)
    bf = jax.ShapeDtypeStruct((t, width), BF16)
    return pl.pallas_call(
        kern, grid=(t // tm,),
        in_specs=[pl.BlockSpec((tm, d), row), _const_spec(g.shape), _const_spec(w_in.shape),
                  _const_spec(lb_logits.shape)],
        out_specs=[out_spec] * 5,
        out_shape=[bf, bf, bf, jax.ShapeDtypeStruct((t, width), F32), bf],
        compiler_params=_params(1), name="hgrn_proj")(x, g, w_in, lb_logits)


def _hgrn_rec_kernel(q_ref, k_ref, v_ref, lf_ref, sg_ref, gain_ref, o_ref, st_ref, *, ts):
    C, c = A_CHUNK, A_SUB
    nc, nsub, dk = ts // C, C // c, q_ref.shape[-1]

    @pl.when(pl.program_id(2) == 0)
    def _():
        st_ref[...] = jnp.zeros_like(st_ref)

    q = q_ref[0].astype(F32)
    k = k_ref[0].astype(F32)
    v = v_ref[0]
    g = lf_ref[0]

    r = lax.broadcasted_iota(jnp.int32, (128, 128), 0)
    s = lax.broadcasted_iota(jnp.int32, (128, 128), 1)
    lc, ls = C.bit_length() - 1, c.bit_length() - 1
    same_c = (r >> lc) == (s >> lc)
    same_s = (r >> ls) == (s >> ls)
    incl_c = jnp.where(same_c, jnp.where(s <= r, 1.0, 0.0), 0.0)
    after_c = jnp.where(same_c, jnp.where(s > r, 1.0, 0.0), 0.0)
    incl_s = jnp.where(same_s, jnp.where(s <= r, 1.0, 0.0), 0.0)
    m3 = jnp.concatenate([incl_c, after_c, incl_s], axis=0).astype(BF16)
    m9 = jnp.concatenate([m3, m3, m3], axis=1)
    g1 = g.astype(BF16)
    r1 = g - g1.astype(F32)
    g2 = r1.astype(BF16)
    g3 = (r1 - g2.astype(F32)).astype(BF16)
    b_parts, c_parts, bl_parts = [], [], []
    for j in range(ts // 128):
        rows = slice(j * 128, (j + 1) * 128)
        gs = jnp.concatenate([g1[rows], g2[rows], g3[rows]], axis=0)
        res = _dot(m9, gs)
        b_parts.append(res[0:128])
        c_parts.append(res[128:256])
        bl_parts.append(res[256:384])
    b = jnp.concatenate(b_parts, axis=0)
    cc = jnp.concatenate(c_parts, axis=0)
    bl = jnp.concatenate(bl_parts, axis=0)

    q_hat = (q * jnp.exp(b)).astype(BF16)
    k_hat = (k * jnp.exp(cc)).astype(BF16)
    q_loc = q * jnp.exp(bl)

    row_c = lax.broadcasted_iota(jnp.int32, (nc, C, dk), 1)
    k3 = k.reshape(nc, C, dk)
    cc3 = cc.reshape(nc, C, dk)
    ql3 = q_loc.reshape(nc, C, dk)
    qs_parts, ks_parts = [], []
    for i in range(1, nsub):
        ref_i = cc3[:, i * c - 1:i * c, :]
        arg = jnp.where(row_c < i * c, cc3 - ref_i, NEG_BIG)
        ks_parts.append((k3 * jnp.exp(arg)).astype(BF16))
        in_i = (row_c >= i * c) & (row_c < (i + 1) * c)
        qs_parts.append(jnp.where(in_i, ql3, 0.0).astype(BF16))
    q_stack = jnp.concatenate(qs_parts, axis=2)
    k_stack = jnp.concatenate(ks_parts, axis=2)

    ng = ts // c
    row_s = lax.broadcasted_iota(jnp.int32, (ng, c, dk), 1)
    q16 = q.reshape(ng, c, dk)
    k16 = k.reshape(ng, c, dk)
    bl16 = bl.reshape(ng, c, dk)
    ones = jnp.ones((dk, C), BF16)
    rr = lax.broadcasted_iota(jnp.int32, (ts, C), 0)
    ll = lax.broadcasted_iota(jnp.int32, (ts, C), 1)
    col_in_sub = ll - (((rr & (C - 1)) >> ls) << ls)
    a_diag = jnp.zeros((ts, C), F32)
    for j in range(c):
        arg = jnp.where(row_s >= j, bl16 - bl16[:, j:j + 1, :], NEG_BIG)
        e = (q16 * k16[:, j:j + 1, :] * jnp.exp(arg)).reshape(ts, dk).astype(BF16)
        a_diag = jnp.where(col_in_sub == j, _dot(e, ones), a_diag)

    gain = gain_ref[...]
    for n in range(nc):
        rows = slice(n * C, (n + 1) * C)
        a = _dot_nt(q_stack[n], k_stack[n]) + a_diag[rows]
        o_intra = _dot(a.astype(BF16), v[rows])
        st = st_ref[...]
        o = o_intra + _dot_nt(q_hat[rows], st.astype(BF16))
        upd = _dot_tn(v[rows], k_hat[rows])
        st_ref[...] = st * jnp.exp(b[(n + 1) * C - 1:(n + 1) * C, :]) + upd
        o = _rms(o, gain) * sg_ref[0, rows, :].astype(F32)
        o_ref[0, rows, :] = o.astype(BF16)


def _hgrn_rec_call(q, k, v, lf, sg, gain, *, heads, ts=256):
    bsz, seq, width = q.shape
    dk = width // heads
    blk = pl.BlockSpec((1, ts, dk), lambda b, h, s: (b, s, h))
    kern = functools.partial(_hgrn_rec_kernel, ts=ts)
    return pl.pallas_call(
        kern, grid=(bsz, heads, seq // ts),
        in_specs=[blk, blk, blk, blk, blk, _const_spec(gain.shape)],
        out_specs=blk, out_shape=jax.ShapeDtypeStruct((bsz, seq, width), BF16),
        scratch_shapes=[pltpu.VMEM((dk, dk), F32)],
        compiler_params=_params(3), name="hgrn_rec")(q, k, v, lf, sg, gain)


def _rope_tables_lanes(pos_col, inv_freq_row, half):
    ang = pos_col.astype(F32) * inv_freq_row
    cos, sin = jnp.cos(ang), jnp.sin(ang)
    lane = lax.broadcasted_iota(jnp.int32, ang.shape, 1)
    c_tab = jnp.where(lane < 2 * half, cos, 0.0)
    s_tab = jnp.where(lane < half, -sin, jnp.where(lane < 2 * half, sin, 0.0))
    return c_tab, s_tab


def _kv_kernel(x_ref, pos_ref, g_ref, wdc_ref, wdr_ref, gl_ref, wk_ref, wvt_ref, freq_ref,
               k_out, vt_out, *, heads, nope, vdim, half):
    h = _rms(x_ref[0], g_ref[...]).astype(BF16)
    c_kv = _rms(_dot(h, wdc_ref[...]), gl_ref[...]).astype(BF16)
    kr = _dot(h, wdr_ref[...])
    c_tab, s_tab = _rope_tables_lanes(pos_ref[0], freq_ref[...], half)
    k_rope = (kr * c_tab + pltpu.roll(kr, half, 1) * s_tab).astype(BF16)
    k_nope = _dot(c_kv, wk_ref[...])
    v_t = _dot_nt(wvt_ref[...], c_kv)
    for hd in range(heads):
        k_out[0, hd, :, 0:nope] = k_nope[:, hd * nope:(hd + 1) * nope].astype(BF16)
        k_out[0, hd, :, nope:] = k_rope
        vt_out[0, hd] = v_t[hd * vdim:(hd + 1) * vdim].astype(BF16)


def _kv_call(x, pos_col, g, wd_c, wd_r, gl, wk, wvt, freq_row, *, heads, nope, vdim, half, tm=256):
    bsz, seq, d = x.shape
    kern = functools.partial(_kv_kernel, heads=heads, nope=nope, vdim=vdim, half=half)
    kd = nope + 128
    return pl.pallas_call(
        kern, grid=(bsz, seq // tm),
        in_specs=[pl.BlockSpec((1, tm, d), lambda b, s: (b, s, 0)),
                  pl.BlockSpec((1, tm, 1), lambda b, s: (b, s, 0)),
                  _const_spec(g.shape), _const_spec(wd_c.shape), _const_spec(wd_r.shape),
                  _const_spec(gl.shape), _const_spec(wk.shape), _const_spec(wvt.shape),
                  _const_spec(freq_row.shape)],
        out_specs=[pl.BlockSpec((1, heads, tm, kd), lambda b, s: (b, 0, s, 0)),
                   pl.BlockSpec((1, heads, vdim, tm), lambda b, s: (b, 0, 0, s))],
        out_shape=[jax.ShapeDtypeStruct((bsz, heads, seq, kd), BF16),
                   jax.ShapeDtypeStruct((bsz, heads, vdim, seq), BF16)],
        compiler_params=_params(2), name="mla_shared_kv")(
            x, pos_col, g, wd_c, wd_r, gl, wk, wvt, freq_row)


def _q_kernel(x_ref, pos_ref, g_ref, wdq_ref, gq_ref, wuqt_ref, freq_ref, qt_out,
              *, heads, nope, half, qd, scale):
    h = _rms(x_ref[0], g_ref[...]).astype(BF16)
    c_q = (_rms(_dot(h, wdq_ref[...]), gq_ref[...]) * scale).astype(BF16)
    q_t = _dot_nt(wuqt_ref[...], c_q)
    ang = freq_ref[...] * pos_ref[0].astype(F32)
    cos, sin = jnp.cos(ang), jnp.sin(ang)
    zeros = jnp.zeros((qd - nope - 2 * half, q_t.shape[1]), F32)
    for hd in range(heads):
        base = hd * qd
        x1 = q_t[base + nope:base + nope + half]
        x2 = q_t[base + nope + half:base + nope + 2 * half]
        full = jnp.concatenate(
            [q_t[base:base + nope], x1 * cos - x2 * sin, x2 * cos + x1 * sin, zeros], axis=0)
        qt_out[0, hd] = full.astype(BF16)


def _q_call(x, pos_row, g, w_dq, gq, w_uqt, freq_col, *, heads, nope, half, qd, scale, tm=256):
    bsz, seq, d = x.shape
    kern = functools.partial(_q_kernel, heads=heads, nope=nope, half=half, qd=qd, scale=scale)
    return pl.pallas_call(
        kern, grid=(bsz, seq // tm),
        in_specs=[pl.BlockSpec((1, tm, d), lambda b, s: (b, s, 0)),
                  pl.BlockSpec((1, 1, tm), lambda b, s: (b, 0, s)),
                  _const_spec(g.shape), _const_spec(w_dq.shape), _const_spec(gq.shape),
                  _const_spec(w_uqt.shape), _const_spec(freq_col.shape)],
        out_specs=pl.BlockSpec((1, heads, qd, tm), lambda b, s: (b, 0, 0, s)),
        out_shape=jax.ShapeDtypeStruct((bsz, heads, qd, seq), BF16),
        compiler_params=_params(2), name="mla_q")(x, pos_row, g, w_dq, gq, w_uqt, freq_col)


def _attn_kernel(qt_ref, k_ref, vt_ref, o_ref, *, tq, tk):
    qi = pl.program_id(2)
    qt = qt_ref[0, 0]
    dv = vt_ref.shape[2]

    def step(j, carry, masked):
        m, l, acc = carry
        start = pl.multiple_of(j * tk, tk)
        s = _dot(k_ref[0, 0, pl.ds(start, tk), :], qt)
        if masked:
            key = start + lax.broadcasted_iota(jnp.int32, (tk, tq), 0)
            qry = qi * tq + lax.broadcasted_iota(jnp.int32, (tk, tq), 1)
            s = jnp.where(key <= qry, s, NEG_BIG)
        m_new = jnp.maximum(m, jnp.max(s, axis=0, keepdims=True))
        alpha = jnp.exp2(m - m_new)
        p = jnp.exp2(s - m_new)
        l = alpha * l + jnp.sum(p, axis=0, keepdims=True)
        acc = alpha * acc + _dot(vt_ref[0, 0, :, pl.ds(start, tk)], p.astype(BF16))
        return m_new, l, acc

    init = (jnp.full((1, tq), NEG_BIG, F32), jnp.zeros((1, tq), F32), jnp.zeros((dv, tq), F32))
    n_full = qi * (tq // tk)
    carry = lax.fori_loop(0, n_full, lambda j, cr: step(j, cr, False), init)
    for d in range(tq // tk):
        carry = step(n_full + d, carry, True)
    _, l, acc = carry
    o_ref[0] = jnp.transpose(acc / l).astype(BF16)


def _attn_call(qt, k, vt, *, tq=512, tk=512):
    bsz, heads, qd, seq = qt.shape
    dv = vt.shape[2]
    kern = functools.partial(_attn_kernel, tq=tq, tk=tk)
    return pl.pallas_call(
        kern, grid=(bsz, heads, seq // tq),
        in_specs=[pl.BlockSpec((1, 1, qd, tq), lambda b, h, i: (b, h, 0, i)),
                  pl.BlockSpec((1, 1, seq, k.shape[3]), lambda b, h, i: (b, h, 0, 0)),
                  pl.BlockSpec((1, 1, dv, seq), lambda b, h, i: (b, h, 0, 0))],
        out_specs=pl.BlockSpec((1, tq, dv), lambda b, h, i: (b, i, h)),
        out_shape=jax.ShapeDtypeStruct((bsz, seq, heads * dv), BF16),
        compiler_params=_params(3), name="mla_attention")(qt, k, vt)


def kernel(x, p, positions, norm_gains, ffn_w_in, ffn_w_out, ple_w_gate, ple_w_in, a_w_in,
           a_lb_logits, a_out_gain, a_w_out, kv_norm_in, kv_w_down, kv_latent_norm, kv_w_up,
           b_w_dq, b_q_norm, b_w_uq, b_w_out, final_norm):
    bsz, seq, d = x.shape
    depth = norm_gains.shape[0]
    n_a = a_w_in.shape[0]
    t = bsz * seq
    bf = lambda w: w.astype(BF16)
    row = lambda g: g.reshape(1, -1).astype(F32)

    a_key = a_out_gain.shape[1]
    a_heads = a_w_out.shape[1] // a_key
    kv_lora = kv_latent_norm.shape[0]
    rope = kv_w_down.shape[1] - kv_lora
    half = rope // 2
    vdim = 128
    heads = b_w_out.shape[1] // vdim
    nope = kv_w_up.shape[1] // heads - vdim
    qd = 256
    assert nope == 128 and rope == 64 and b_w_uq.shape[2] == heads * (nope + rope)

    inv_freq = 1.0 / (ROPE_THETA ** (jnp.arange(0, rope, 2, dtype=F32) / rope))
    freq_row = jnp.tile(inv_freq, 128 // half).reshape(1, 128)
    freq_col = inv_freq.reshape(half, 1)
    pos_col = positions.reshape(bsz, seq, 1)
    pos_row = positions.reshape(bsz, 1, seq)

    wd_c = bf(kv_w_down[:, :kv_lora])
    wd_r = bf(jnp.concatenate([kv_w_down[:, kv_lora:]] * (128 // rope), axis=1))
    w_up = kv_w_up.reshape(kv_lora, heads, nope + vdim)
    wk = bf(w_up[:, :, :nope].reshape(kv_lora, heads * nope))
    wvt = bf(w_up[:, :, nope:].reshape(kv_lora, heads * vdim).T)

    def shared_kv(xs):
        return _kv_call(xs.reshape(bsz, seq, d), pos_col, row(kv_norm_in), wd_c, wd_r,
                        row(kv_latent_norm), wk, wvt, freq_row,
                        heads=heads, nope=nope, vdim=vdim, half=half)

    xf = x.reshape(t, d)
    shared = shared_kv(xf) if n_a == 0 else None
    attn_scale = float(nope + rope) ** -0.5 * math.log2(math.e)
    for li in range(depth):
        g = norm_gains[li]
        pre = None
        if li < n_a:
            x1 = _block_call(xf, row(g[0]), bf(ffn_w_in[li, 0]), bf(ffn_w_out[li, 0]))
            q, k, v, lf, sg = _hgrn_proj_call(x1, row(g[1]), bf(a_w_in[li]),
                                              a_lb_logits.astype(F32), layer=li, key_dim=a_key)
            r3 = lambda a: a.reshape(bsz, seq, -1)
            o = _hgrn_rec_call(r3(q), r3(k), r3(v), r3(lf), r3(sg), row(a_out_gain[li]),
                               heads=a_heads)
            pre = (o.reshape(t, -1), bf(a_w_out[li]))
        else:
            bi = li - n_a
            x1 = _block_call(xf, row(g[0]), bf(ffn_w_in[li, 0]), bf(ffn_w_out[li, 0]))
            w_uq = b_w_uq[bi].reshape(-1, heads, nope + rope)
            w_uq = jnp.pad(w_uq, ((0, 0), (0, 0), (0, qd - nope - rope)))
            w_uqt = bf(w_uq.reshape(-1, heads * qd).T)
            qt = _q_call(x1.reshape(bsz, seq, d), pos_row, row(g[1]), bf(b_w_dq[bi]),
                         row(b_q_norm[bi]), w_uqt, freq_col, heads=heads, nope=nope, half=half,
                         qd=qd, scale=attn_scale)
            k_all, vt_all = shared
            o = _attn_call(qt, k_all, vt_all)
            pre = (o.reshape(t, -1), bf(b_w_out[bi]))
        ple = (row(g[3]), bf(ple_w_gate[li]), p[li].reshape(t, -1), bf(ple_w_in[li]))
        final = row(final_norm) if li == depth - 1 else None
        xf = _block_call(x1, row(g[2]), bf(ffn_w_in[li, 1]), bf(ffn_w_out[li, 1]),
                         pre=pre, ple=ple, final=final)
        if li == n_a - 1:
            shared = shared_kv(xf)
    return xf.reshape(bsz, seq, d)
```

```python
import functools
import math

import jax
import jax.numpy as jnp
from jax import lax
from jax.experimental import pallas as pl
from jax.experimental.pallas import tpu as pltpu

F32 = jnp.float32
BF16 = jnp.bfloat16

EPS = 1e-6
ROPE_THETA = 10000.0
A_CHUNK = 64
A_SUB = 16
NEG_BIG = -1e30

V7X_VMEM_BYTES = 64 * 1024 * 1024
VMEM_LIMIT = 56 * 1024 * 1024


def _rms(x, g):
    ms = jnp.mean(x * x, axis=-1, keepdims=True)
    return x * lax.rsqrt(ms + EPS) * g


def _sigmoid(x):
    return 1.0 / (1.0 + jnp.exp(-x))


def _dot(a, b):
    return jnp.dot(a, b, preferred_element_type=F32)


def _dot_nt(a, b):
    return lax.dot_general(a, b, (((1,), (1,)), ((), ())), preferred_element_type=F32)


def _dot_tn(a, b):
    return lax.dot_general(a, b, (((0,), (0,)), ((), ())), preferred_element_type=F32)


def _const_spec(shape):
    nd = len(shape)
    return pl.BlockSpec(shape, lambda *_: (0,) * nd, pipeline_mode=pl.Buffered(1))


def _params(n_axes):
    return pltpu.CompilerParams(
        dimension_semantics=("arbitrary",) * n_axes, vmem_limit_bytes=VMEM_LIMIT)


def _block_kernel(*refs, d_ff, has_pre, has_ple, has_final):
    it = iter(refs)
    x_ref = next(it)
    if has_pre:
        y_ref, wpre_ref = next(it), next(it)
    g_ref, win_ref, wout_ref = next(it), next(it), next(it)
    if has_ple:
        gp_ref, wg_ref, p_ref, wp_ref = next(it), next(it), next(it), next(it)
    if has_final:
        gf_ref = next(it)
    o_ref = next(it)

    x = x_ref[...]
    if has_pre:
        x = x + _dot(y_ref[...], wpre_ref[...])
    h = _rms(x, g_ref[...]).astype(BF16)
    gate = _dot(h, win_ref[:, :d_ff])
    up = _dot(h, win_ref[:, d_ff:])
    act = (gate * _sigmoid(gate) * up).astype(BF16)
    x = x + 0.5 * _dot(act, wout_ref[...])
    if has_ple:
        hg = _rms(x, gp_ref[...]).astype(BF16)
        emb_gate = _sigmoid(_dot(hg, wg_ref[...]))
        emb = _dot(p_ref[...].astype(BF16), wp_ref[...])
        x = x + emb_gate * emb
    if has_final:
        x = _rms(x, gf_ref[...])
    o_ref[...] = x


def _block_call(x, g, w_in, w_out, *, pre=None, ple=None, final=None, tm=256):
    t, d = x.shape
    d_ff = w_out.shape[0]
    row = lambda i: (i, 0)
    args, specs = [x], [pl.BlockSpec((tm, d), row)]
    if pre is not None:
        y, w_pre = pre
        args += [y, w_pre]
        specs += [pl.BlockSpec((tm, y.shape[1]), row), _const_spec(w_pre.shape)]
    args += [g, w_in, w_out]
    specs += [_const_spec(g.shape), _const_spec(w_in.shape), _const_spec(w_out.shape)]
    if ple is not None:
        gp, wg, p, wp = ple
        args += [gp, wg, p, wp]
        specs += [_const_spec(gp.shape), _const_spec(wg.shape),
                  pl.BlockSpec((tm, p.shape[1]), row), _const_spec(wp.shape)]
    if final is not None:
        args += [final]
        specs += [_const_spec(final.shape)]
    kern = functools.partial(_block_kernel, d_ff=d_ff, has_pre=pre is not None,
                             has_ple=ple is not None, has_final=final is not None)
    return pl.pallas_call(
        kern, grid=(t // tm,), in_specs=specs, out_specs=pl.BlockSpec((tm, d), row),
        out_shape=jax.ShapeDtypeStruct((t, d), F32), compiler_params=_params(1),
        name="token_block")(*args)


def _hgrn_proj_kernel(x_ref, g_ref, w_ref, lbl_ref, q_ref, k_ref, v_ref, lf_ref, sg_ref,
                      *, width, layer, q_scale):
    h = _rms(x_ref[...], g_ref[...]).astype(BF16)
    q = _dot(h, w_ref[:, 0 * width:1 * width])
    f = _dot(h, w_ref[:, 1 * width:2 * width])
    v = _dot(h, w_ref[:, 2 * width:3 * width])
    og = _dot(h, w_ref[:, 3 * width:4 * width])
    lbl = lbl_ref[...]
    e = jnp.exp(lbl - jnp.max(lbl, axis=0, keepdims=True))
    sm = e / jnp.sum(e, axis=0, keepdims=True)
    lb = jnp.sum(sm[:layer + 1], axis=0, keepdims=True)
    sf = _sigmoid(f)
    q_ref[...] = (q * q_scale).astype(BF16)
    k_ref[...] = ((1.0 - lb) * _sigmoid(-f)).astype(BF16)
    v_ref[...] = v.astype(BF16)
    lf_ref[...] = jnp.log(lb + (1.0 - lb) * sf)
    sg_ref[...] = _sigmoid(og).astype(BF16)


def _hgrn_proj_call(x, g, w_in, lb_logits, *, layer, key_dim, tm=256):
    t, d = x.shape
    width = w_in.shape[1] // 4
    row = lambda i: (i, 0)
    kern = functools.partial(_hgrn_proj_kernel, width=width, layer=layer,
                             q_scale=float(key_dim) ** -0.5)
    out_spec = pl.BlockSpec((tm, width), row)
    bf = jax.ShapeDtypeStruct((t, width), BF16)
    return pl.pallas_call(
        kern, grid=(t // tm,),
        in_specs=[pl.BlockSpec((tm, d), row), _const_spec(g.shape), _const_spec(w_in.shape),
                  _const_spec(lb_logits.shape)],
        out_specs=[out_spec] * 5,
        out_shape=[bf, bf, bf, jax.ShapeDtypeStruct((t, width), F32), bf],
        compiler_params=_params(1), name="hgrn_proj")(x, g, w_in, lb_logits)


def _hgrn_rec_kernel(q_ref, k_ref, v_ref, lf_ref, sg_ref, gain_ref, o_ref, st_ref, *, ts):
    C, c = A_CHUNK, A_SUB
    nc, nsub, dk = ts // C, C // c, q_ref.shape[-1]

    @pl.when(pl.program_id(2) == 0)
    def _():
        st_ref[...] = jnp.zeros_like(st_ref)

    q = q_ref[0].astype(F32)
    k = k_ref[0].astype(F32)
    v = v_ref[0]
    g = lf_ref[0]

    r = lax.broadcasted_iota(jnp.int32, (128, 128), 0)
    s = lax.broadcasted_iota(jnp.int32, (128, 128), 1)
    lc, ls = C.bit_length() - 1, c.bit_length() - 1
    same_c = (r >> lc) == (s >> lc)
    same_s = (r >> ls) == (s >> ls)
    incl_c = jnp.where(same_c, jnp.where(s <= r, 1.0, 0.0), 0.0)
    after_c = jnp.where(same_c, jnp.where(s > r, 1.0, 0.0), 0.0)
    incl_s = jnp.where(same_s, jnp.where(s <= r, 1.0, 0.0), 0.0)
    m3 = jnp.concatenate([incl_c, after_c, incl_s], axis=0).astype(BF16)
    m9 = jnp.concatenate([m3, m3, m3], axis=1)
    g1 = g.astype(BF16)
    r1 = g - g1.astype(F32)
    g2 = r1.astype(BF16)
    g3 = (r1 - g2.astype(F32)).astype(BF16)
    b_parts, c_parts, bl_parts = [], [], []
    for j in range(ts // 128):
        rows = slice(j * 128, (j + 1) * 128)
        gs = jnp.concatenate([g1[rows], g2[rows], g3[rows]], axis=0)
        res = _dot(m9, gs)
        b_parts.append(res[0:128])
        c_parts.append(res[128:256])
        bl_parts.append(res[256:384])
    b = jnp.concatenate(b_parts, axis=0)
    cc = jnp.concatenate(c_parts, axis=0)
    bl = jnp.concatenate(bl_parts, axis=0)

    q_hat = (q * jnp.exp(b)).astype(BF16)
    k_hat = (k * jnp.exp(cc)).astype(BF16)
    q_loc = q * jnp.exp(bl)

    row_c = lax.broadcasted_iota(jnp.int32, (nc, C, dk), 1)
    k3 = k.reshape(nc, C, dk)
    cc3 = cc.reshape(nc, C, dk)
    ql3 = q_loc.reshape(nc, C, dk)
    qs_parts, ks_parts = [], []
    for i in range(1, nsub):
        ref_i = cc3[:, i * c - 1:i * c, :]
        arg = jnp.where(row_c < i * c, cc3 - ref_i, NEG_BIG)
        ks_parts.append((k3 * jnp.exp(arg)).astype(BF16))
        in_i = (row_c >= i * c) & (row_c < (i + 1) * c)
        qs_parts.append(jnp.where(in_i, ql3, 0.0).astype(BF16))
    q_stack = jnp.concatenate(qs_parts, axis=2)
    k_stack = jnp.concatenate(ks_parts, axis=2)

    ng = ts // c
    row_s = lax.broadcasted_iota(jnp.int32, (ng, c, dk), 1)
    q16 = q.reshape(ng, c, dk)
    k16 = k.reshape(ng, c, dk)
    bl16 = bl.reshape(ng, c, dk)
    ones = jnp.ones((dk, C), BF16)
    rr = lax.broadcasted_iota(jnp.int32, (ts, C), 0)
    ll = lax.broadcasted_iota(jnp.int32, (ts, C), 1)
    col_in_sub = ll - (((rr & (C - 1)) >> ls) << ls)
    a_diag = jnp.zeros((ts, C), F32)
    for j in range(c):
        arg = jnp.where(row_s >= j, bl16 - bl16[:, j:j + 1, :], NEG_BIG)
        e = (q16 * k16[:, j:j + 1, :] * jnp.exp(arg)).reshape(ts, dk).astype(BF16)
        a_diag = jnp.where(col_in_sub == j, _dot(e, ones), a_diag)

    gain = gain_ref[...]
    for n in range(nc):
        rows = slice(n * C, (n + 1) * C)
        a = _dot_nt(q_stack[n], k_stack[n]) + a_diag[rows]
        o_intra = _dot(a.astype(BF16), v[rows])
        st = st_ref[...]
        o = o_intra + _dot_nt(q_hat[rows], st.astype(BF16))
        upd = _dot_tn(v[rows], k_hat[rows])
        st_ref[...] = st * jnp.exp(b[(n + 1) * C - 1:(n + 1) * C, :]) + upd
        o = _rms(o, gain) * sg_ref[0, rows, :].astype(F32)
        o_ref[0, rows, :] = o.astype(BF16)


def _hgrn_rec_call(q, k, v, lf, sg, gain, *, heads, ts=256):
    bsz, seq, width = q.shape
    dk = width // heads
    blk = pl.BlockSpec((1, ts, dk), lambda b, h, s: (b, s, h))
    kern = functools.partial(_hgrn_rec_kernel, ts=ts)
    return pl.pallas_call(
        kern, grid=(bsz, heads, seq // ts),
        in_specs=[blk, blk, blk, blk, blk, _const_spec(gain.shape)],
        out_specs=blk, out_shape=jax.ShapeDtypeStruct((bsz, seq, width), BF16),
        scratch_shapes=[pltpu.VMEM((dk, dk), F32)],
        compiler_params=_params(3), name="hgrn_rec")(q, k, v, lf, sg, gain)


def _rope_tables_lanes(pos_col, inv_freq_row, half):
    ang = pos_col.astype(F32) * inv_freq_row
    cos, sin = jnp.cos(ang), jnp.sin(ang)
    lane = lax.broadcasted_iota(jnp.int32, ang.shape, 1)
    c_tab = jnp.where(lane < 2 * half, cos, 0.0)
    s_tab = jnp.where(lane < half, -sin, jnp.where(lane < 2 * half, sin, 0.0))
    return c_tab, s_tab


def _kv_kernel(x_ref, pos_ref, g_ref, wdc_ref, wdr_ref, gl_ref, wk_ref, wvt_ref, freq_ref,
               k_out, vt_out, *, heads, nope, vdim, half):
    h = _rms(x_ref[0], g_ref[...]).astype(BF16)
    c_kv = _rms(_dot(h, wdc_ref[...]), gl_ref[...]).astype(BF16)
    kr = _dot(h, wdr_ref[...])
    c_tab, s_tab = _rope_tables_lanes(pos_ref[0], freq_ref[...], half)
    k_rope = (kr * c_tab + pltpu.roll(kr, half, 1) * s_tab).astype(BF16)
    k_nope = _dot(c_kv, wk_ref[...])
    v_t = _dot_nt(wvt_ref[...], c_kv)
    for hd in range(heads):
        k_out[0, hd, :, 0:nope] = k_nope[:, hd * nope:(hd + 1) * nope].astype(BF16)
        k_out[0, hd, :, nope:] = k_rope
        vt_out[0, hd] = v_t[hd * vdim:(hd + 1) * vdim].astype(BF16)


def _kv_call(x, pos_col, g, wd_c, wd_r, gl, wk, wvt, freq_row, *, heads, nope, vdim, half, tm=256):
    bsz, seq, d = x.shape
    kern = functools.partial(_kv_kernel, heads=heads, nope=nope, vdim=vdim, half=half)
    kd = nope + 128
    return pl.pallas_call(
        kern, grid=(bsz, seq // tm),
        in_specs=[pl.BlockSpec((1, tm, d), lambda b, s: (b, s, 0)),
                  pl.BlockSpec((1, tm, 1), lambda b, s: (b, s, 0)),
                  _const_spec(g.shape), _const_spec(wd_c.shape), _const_spec(wd_r.shape),
                  _const_spec(gl.shape), _const_spec(wk.shape), _const_spec(wvt.shape),
                  _const_spec(freq_row.shape)],
        out_specs=[pl.BlockSpec((1, heads, tm, kd), lambda b, s: (b, 0, s, 0)),
                   pl.BlockSpec((1, heads, vdim, tm), lambda b, s: (b, 0, 0, s))],
        out_shape=[jax.ShapeDtypeStruct((bsz, heads, seq, kd), BF16),
                   jax.ShapeDtypeStruct((bsz, heads, vdim, seq), BF16)],
        compiler_params=_params(2), name="mla_shared_kv")(
            x, pos_col, g, wd_c, wd_r, gl, wk, wvt, freq_row)


def _q_kernel(x_ref, pos_ref, g_ref, wdq_ref, gq_ref, wuqt_ref, freq_ref, qt_out,
              *, heads, nope, half, qd, scale):
    h = _rms(x_ref[0], g_ref[...]).astype(BF16)
    c_q = (_rms(_dot(h, wdq_ref[...]), gq_ref[...]) * scale).astype(BF16)
    q_t = _dot_nt(wuqt_ref[...], c_q)
    ang = freq_ref[...] * pos_ref[0].astype(F32)
    cos, sin = jnp.cos(ang), jnp.sin(ang)
    zeros = jnp.zeros((qd - nope - 2 * half, q_t.shape[1]), F32)
    for hd in range(heads):
        base = hd * qd
        x1 = q_t[base + nope:base + nope + half]
        x2 = q_t[base + nope + half:base + nope + 2 * half]
        full = jnp.concatenate(
            [q_t[base:base + nope], x1 * cos - x2 * sin, x2 * cos + x1 * sin, zeros], axis=0)
        qt_out[0, hd] = full.astype(BF16)


def _q_call(x, pos_row, g, w_dq, gq, w_uqt, freq_col, *, heads, nope, half, qd, scale, tm=256):
    bsz, seq, d = x.shape
    kern = functools.partial(_q_kernel, heads=heads, nope=nope, half=half, qd=qd, scale=scale)
    return pl.pallas_call(
        kern, grid=(bsz, seq // tm),
        in_specs=[pl.BlockSpec((1, tm, d), lambda b, s: (b, s, 0)),
                  pl.BlockSpec((1, 1, tm), lambda b, s: (b, 0, s)),
                  _const_spec(g.shape), _const_spec(w_dq.shape), _const_spec(gq.shape),
                  _const_spec(w_uqt.shape), _const_spec(freq_col.shape)],
        out_specs=pl.BlockSpec((1, heads, qd, tm), lambda b, s: (b, 0, 0, s)),
        out_shape=jax.ShapeDtypeStruct((bsz, heads, qd, seq), BF16),
        compiler_params=_params(2), name="mla_q")(x, pos_row, g, w_dq, gq, w_uqt, freq_col)


def _attn_kernel(qt_ref, k_ref, vt_ref, o_ref, s0, s1, p0, p1, acc_ref, *, tq, tk):
    qi = pl.program_id(2)
    qt = qt_ref[0, 0]

    def scores(j, s_out):
        start = pl.multiple_of(j * tk, tk)
        s_out[...] = _dot(k_ref[0, 0, pl.ds(start, tk), :], qt)

    def softmax_step(s_in, p_out, m, l, key_offset=None):
        s = s_in[...]
        if key_offset is not None:
            key = key_offset + lax.broadcasted_iota(jnp.int32, (tk, tq), 0)
            qry = lax.broadcasted_iota(jnp.int32, (tk, tq), 1)
            s = jnp.where(key <= qry, s, NEG_BIG)
        m_new = jnp.maximum(m, jnp.max(s, axis=0, keepdims=True))
        alpha = jnp.exp2(m - m_new)
        p = jnp.exp2(s - m_new)
        p_out[...] = p.astype(BF16)
        return m_new, alpha * l + jnp.sum(p, axis=0, keepdims=True), alpha

    def accumulate(j, p_in, alpha):
        start = pl.multiple_of(j * tk, tk)
        acc_ref[...] = alpha * acc_ref[...] + _dot(vt_ref[0, 0, :, pl.ds(start, tk)], p_in[...])

    def body(j, carry):
        a_prev, m, l = carry
        scores(2 * j + 1, s1)
        m, l, a0 = softmax_step(s0, p0, m, l)
        accumulate(jnp.maximum(2 * j - 1, 0), p1, a_prev)
        scores(2 * j + 2, s0)
        m, l, a1 = softmax_step(s1, p1, m, l)
        accumulate(2 * j, p0, a0)
        return a1, m, l

    acc_ref[...] = jnp.zeros_like(acc_ref)
    p1[...] = jnp.zeros_like(p1)
    scores(0, s0)
    init = (jnp.ones((1, tq), F32), jnp.full((1, tq), NEG_BIG, F32), jnp.zeros((1, tq), F32))
    a_prev, m, l = lax.fori_loop(0, qi, body, init)
    scores(2 * qi + 1, s1)
    m, l, a0 = softmax_step(s0, p0, m, l, key_offset=0)
    accumulate(jnp.maximum(2 * qi - 1, 0), p1, a_prev)
    m, l, a1 = softmax_step(s1, p1, m, l, key_offset=tk)
    accumulate(2 * qi, p0, a0)
    accumulate(2 * qi + 1, p1, a1)
    o_ref[0] = jnp.transpose(acc_ref[...] / l).astype(BF16)


def _attn_call(qt, k, vt, *, tq=1024):
    tk = tq // 2
    bsz, heads, qd, seq = qt.shape
    dv = vt.shape[2]
    kern = functools.partial(_attn_kernel, tq=tq, tk=tk)
    return pl.pallas_call(
        kern, grid=(bsz, heads, seq // tq),
        in_specs=[pl.BlockSpec((1, 1, qd, tq), lambda b, h, i: (b, h, 0, i)),
                  pl.BlockSpec((1, 1, seq, k.shape[3]), lambda b, h, i: (b, h, 0, 0)),
                  pl.BlockSpec((1, 1, dv, seq), lambda b, h, i: (b, h, 0, 0))],
        out_specs=pl.BlockSpec((1, tq, dv), lambda b, h, i: (b, i, h)),
        out_shape=jax.ShapeDtypeStruct((bsz, seq, heads * dv), BF16),
        scratch_shapes=[pltpu.VMEM((tk, tq), F32), pltpu.VMEM((tk, tq), F32),
                        pltpu.VMEM((tk, tq), BF16), pltpu.VMEM((tk, tq), BF16),
                        pltpu.VMEM((dv, tq), F32)],
        compiler_params=_params(3), name="mla_attention")(qt, k, vt)


def kernel(x, p, positions, norm_gains, ffn_w_in, ffn_w_out, ple_w_gate, ple_w_in, a_w_in,
           a_lb_logits, a_out_gain, a_w_out, kv_norm_in, kv_w_down, kv_latent_norm, kv_w_up,
           b_w_dq, b_q_norm, b_w_uq, b_w_out, final_norm):
    bsz, seq, d = x.shape
    depth = norm_gains.shape[0]
    n_a = a_w_in.shape[0]
    t = bsz * seq
    bf = lambda w: w.astype(BF16)
    row = lambda g: g.reshape(1, -1).astype(F32)

    a_key = a_out_gain.shape[1]
    a_heads = a_w_out.shape[1] // a_key
    kv_lora = kv_latent_norm.shape[0]
    rope = kv_w_down.shape[1] - kv_lora
    half = rope // 2
    vdim = 128
    heads = b_w_out.shape[1] // vdim
    nope = kv_w_up.shape[1] // heads - vdim
    qd = 256
    assert nope == 128 and rope == 64 and b_w_uq.shape[2] == heads * (nope + rope)

    inv_freq = 1.0 / (ROPE_THETA ** (jnp.arange(0, rope, 2, dtype=F32) / rope))
    freq_row = jnp.tile(inv_freq, 128 // half).reshape(1, 128)
    freq_col = inv_freq.reshape(half, 1)
    pos_col = positions.reshape(bsz, seq, 1)
    pos_row = positions.reshape(bsz, 1, seq)

    wd_c = bf(kv_w_down[:, :kv_lora])
    wd_r = bf(jnp.concatenate([kv_w_down[:, kv_lora:]] * (128 // rope), axis=1))
    w_up = kv_w_up.reshape(kv_lora, heads, nope + vdim)
    wk = bf(w_up[:, :, :nope].reshape(kv_lora, heads * nope))
    wvt = bf(w_up[:, :, nope:].reshape(kv_lora, heads * vdim).T)

    def shared_kv(xs):
        return _kv_call(xs.reshape(bsz, seq, d), pos_col, row(kv_norm_in), wd_c, wd_r,
                        row(kv_latent_norm), wk, wvt, freq_row,
                        heads=heads, nope=nope, vdim=vdim, half=half)

    xf = x.reshape(t, d)
    shared = shared_kv(xf) if n_a == 0 else None
    attn_scale = float(nope + rope) ** -0.5 * math.log2(math.e)
    for li in range(depth):
        g = norm_gains[li]
        pre = None
        if li < n_a:
            x1 = _block_call(xf, row(g[0]), bf(ffn_w_in[li, 0]), bf(ffn_w_out[li, 0]))
            q, k, v, lf, sg = _hgrn_proj_call(x1, row(g[1]), bf(a_w_in[li]),
                                              a_lb_logits.astype(F32), layer=li, key_dim=a_key)
            r3 = lambda a: a.reshape(bsz, seq, -1)
            o = _hgrn_rec_call(r3(q), r3(k), r3(v), r3(lf), r3(sg), row(a_out_gain[li]),
                               heads=a_heads)
            pre = (o.reshape(t, -1), bf(a_w_out[li]))
        else:
            bi = li - n_a
            x1 = _block_call(xf, row(g[0]), bf(ffn_w_in[li, 0]), bf(ffn_w_out[li, 0]))
            w_uq = b_w_uq[bi].reshape(-1, heads, nope + rope)
            w_uq = jnp.pad(w_uq, ((0, 0), (0, 0), (0, qd - nope - rope)))
            w_uqt = bf(w_uq.reshape(-1, heads * qd).T)
            qt = _q_call(x1.reshape(bsz, seq, d), pos_row, row(g[1]), bf(b_w_dq[bi]),
                         row(b_q_norm[bi]), w_uqt, freq_col, heads=heads, nope=nope, half=half,
                         qd=qd, scale=attn_scale)
            k_all, vt_all = shared
            o = _attn_call(qt, k_all, vt_all)
            pre = (o.reshape(t, -1), bf(b_w_out[bi]))
        ple = (row(g[3]), bf(ple_w_gate[li]), p[li].reshape(t, -1), bf(ple_w_in[li]))
        final = row(final_norm) if li == depth - 1 else None
        xf = _block_call(x1, row(g[2]), bf(ffn_w_in[li, 1]), bf(ffn_w_out[li, 1]),
                         pre=pre, ple=ple, final=final)
        if li == n_a - 1:
            shared = shared_kv(xf)
    return xf.reshape(bsz, seq, d)
```

```python
import functools
import math

import jax
import jax.numpy as jnp
from jax import lax
from jax.experimental import pallas as pl
from jax.experimental.pallas import tpu as pltpu

F32 = jnp.float32
BF16 = jnp.bfloat16

EPS = 1e-6
ROPE_THETA = 10000.0
A_CHUNK = 64
A_SUB = 16
NEG_BIG = -1e30
LOG2_E = math.log2(math.e)

V7X_VMEM_BYTES = 64 * 1024 * 1024
VMEM_LIMIT = V7X_VMEM_BYTES * 7 // 8
F32_SUBLANES = 8
BF16_SUBLANES = 16


def _rms(x, g):
    ms = jnp.mean(x * x, axis=-1, keepdims=True)
    return x * lax.rsqrt(ms + EPS) * g


def _sigmoid(x):
    return 1.0 / (1.0 + jnp.exp(-x))


def _dot(a, b):
    return jnp.dot(a, b, preferred_element_type=F32)


def _dot_nt(a, b):
    return lax.dot_general(a, b, (((1,), (1,)), ((), ())), preferred_element_type=F32)


def _dot_tn(a, b):
    return lax.dot_general(a, b, (((0,), (0,)), ((), ())), preferred_element_type=F32)


def _const_spec(shape):
    nd = len(shape)
    return pl.BlockSpec(shape, lambda *_: (0,) * nd, pipeline_mode=pl.Buffered(1))


def _params(n_axes):
    return pltpu.CompilerParams(
        dimension_semantics=("arbitrary",) * n_axes, vmem_limit_bytes=VMEM_LIMIT)


def _block_kernel(*refs, d_ff, has_pre, has_ple, has_final):
    it = iter(refs)
    x_ref = next(it)
    if has_pre:
        y_ref, wpre_ref = next(it), next(it)
    g_ref, win_ref, wout_ref = next(it), next(it), next(it)
    if has_ple:
        gp_ref, wg_ref, p_ref, wp_ref = next(it), next(it), next(it), next(it)
    if has_final:
        gf_ref = next(it)
    o_ref = next(it)

    x = x_ref[...]
    if has_pre:
        x = x + _dot(y_ref[...], wpre_ref[...])
    h = _rms(x, g_ref[...]).astype(BF16)
    gate = _dot(h, win_ref[:, :d_ff])
    up = _dot(h, win_ref[:, d_ff:])
    act = (gate * _sigmoid(gate) * up).astype(BF16)
    x = x + 0.5 * _dot(act, wout_ref[...])
    if has_ple:
        hg = _rms(x, gp_ref[...]).astype(BF16)
        emb_gate = _sigmoid(_dot(hg, wg_ref[...]))
        emb = _dot(p_ref[...].astype(BF16), wp_ref[...])
        x = x + emb_gate * emb
    if has_final:
        x = _rms(x, gf_ref[...])
    o_ref[...] = x


def _block_call(x, g, w_in, w_out, *, pre=None, ple=None, final=None, tm=256):
    t, d = x.shape
    d_ff = w_out.shape[0]
    row = lambda i: (i, 0)
    args, specs = [x], [pl.BlockSpec((tm, d), row)]
    if pre is not None:
        y, w_pre = pre
        args += [y, w_pre]
        specs += [pl.BlockSpec((tm, y.shape[1]), row), _const_spec(w_pre.shape)]
    args += [g, w_in, w_out]
    specs += [_const_spec(g.shape), _const_spec(w_in.shape), _const_spec(w_out.shape)]
    if ple is not None:
        gp, wg, p, wp = ple
        args += [gp, wg, p, wp]
        specs += [_const_spec(gp.shape), _const_spec(wg.shape),
                  pl.BlockSpec((tm, p.shape[1]), row), _const_spec(wp.shape)]
    if final is not None:
        args += [final]
        specs += [_const_spec(final.shape)]
    kern = functools.partial(_block_kernel, d_ff=d_ff, has_pre=pre is not None,
                             has_ple=ple is not None, has_final=final is not None)
    return pl.pallas_call(
        kern, grid=(t // tm,), in_specs=specs, out_specs=pl.BlockSpec((tm, d), row),
        out_shape=jax.ShapeDtypeStruct((t, d), F32), compiler_params=_params(1),
        name="token_block")(*args)


def _hgrn_proj_kernel(x_ref, g_ref, w_ref, lbl_ref, q_ref, k_ref, v_ref, lf_ref, sg_ref,
                      *, width, layer, q_scale):
    h = _rms(x_ref[...], g_ref[...]).astype(BF16)
    q = _dot(h, w_ref[:, 0 * width:1 * width])
    f = _dot(h, w_ref[:, 1 * width:2 * width])
    v = _dot(h, w_ref[:, 2 * width:3 * width])
    og = _dot(h, w_ref[:, 3 * width:4 * width])
    lbl = lbl_ref[...]
    e = jnp.exp(lbl - jnp.max(lbl, axis=0, keepdims=True))
    sm = e / jnp.sum(e, axis=0, keepdims=True)
    lb = jnp.sum(sm[:layer + 1], axis=0, keepdims=True)
    sf = _sigmoid(f)
    q_ref[...] = (q * q_scale).astype(BF16)
    k_ref[...] = ((1.0 - lb) * _sigmoid(-f)).astype(BF16)
    v_ref[...] = v.astype(BF16)
    lf_ref[...] = jnp.log(lb + (1.0 - lb) * sf)
    sg_ref[...] = _sigmoid(og).astype(BF16)


def _hgrn_proj_call(x, g, w_in, lb_logits, *, layer, key_dim, tm=256):
    t, d = x.shape
    width = w_in.shape[1] // 4
    row = lambda i: (i, 0)
    kern = functools.partial(_hgrn_proj_kernel, width=width, layer=layer,
                             q_scale=float(key_dim) ** -0.5)
    out_spec = pl.BlockSpec((tm, width), row)
    bf = jax.ShapeDtypeStruct((t, width), BF16)
    return pl.pallas_call(
        kern, grid=(t // tm,),
        in_specs=[pl.BlockSpec((tm, d), row), _const_spec(g.shape), _const_spec(w_in.shape),
                  _const_spec(lb_logits.shape)],
        out_specs=[out_spec] * 5,
        out_shape=[bf, bf, bf, jax.ShapeDtypeStruct((t, width), F32), bf],
        compiler_params=_params(1), name="hgrn_proj")(x, g, w_in, lb_logits)


def _hgrn_rec_constants(ts, dk):
    incl = jnp.arange(ts)[None, :] <= jnp.arange(ts)[:, None]
    prefix = jnp.concatenate([incl, incl, incl], axis=1).astype(BF16)
    col = jnp.arange(ts)[None, :]
    reduce_j = (jnp.arange(A_SUB * dk)[:, None] // dk) == (col % A_SUB)
    return prefix, reduce_j.astype(BF16)


def _hgrn_rec_kernel(q_ref, k_ref, v_ref, lf_ref, sg_ref, gain_ref, prefix_ref, reduce_ref,
                     o_ref, st_ref, k_scr, bs_scr, e_scr, *, ts):
    @pl.when(pl.program_id(2) == 0)
    def _():
        st_ref[...] = jnp.zeros_like(st_ref)

    dk = gain_ref.shape[-1]
    for hd in range(st_ref.shape[0]):
        lanes = slice(hd * dk, (hd + 1) * dk)
        o, st_ref[hd] = _hgrn_block(
            q_ref[0, :, lanes].astype(F32), k_ref[0, :, lanes].astype(F32), v_ref[0, :, lanes],
            lf_ref[0, :, lanes] * LOG2_E, st_ref[hd], prefix_ref[...], reduce_ref[...],
            k_scr.at[hd], bs_scr.at[hd], e_scr.at[hd], ts=ts)
        o = _rms(o, gain_ref[...]) * sg_ref[0, :, lanes].astype(F32)
        o_ref[0, :, lanes] = o.astype(BF16)


def _hgrn_block(q, k, v, g, st, prefix, reduce_j, k_scr, bs_scr, e_scr, *, ts):
    C, c = A_CHUNK, A_SUB
    nc, nsub, ng, dk = ts // C, C // c, ts // c, q.shape[-1]

    g1 = g.astype(BF16)
    r1 = g - g1.astype(F32)
    g2 = r1.astype(BF16)
    g3 = (r1 - g2.astype(F32)).astype(BF16)
    big_b = _dot(prefix, jnp.concatenate([g1, g2, g3], axis=0))
    b_last = big_b[ts - 1:ts]
    after = b_last - big_b

    def since_start_of(size):
        groups = [big_b[:size]] + [big_b[lo:lo + size] - big_b[lo - 1:lo]
                                   for lo in range(size, ts, size)]
        return jnp.concatenate(groups, axis=0)

    b_chunk = since_start_of(C)
    b_sub = since_start_of(c)
    to_chunk_end = jnp.concatenate(
        [big_b[lo + C - 1:lo + C] - big_b[lo:lo + C] for lo in range(0, ts, C)],
        axis=0).reshape(nc, C, dk)

    q_blk = (q * jnp.exp2(big_b)).astype(BF16)
    k_blk = (k * jnp.exp2(after)).astype(BF16)
    q_chunk = (q * jnp.exp2(b_chunk)).astype(BF16)
    q_sub = (q * jnp.exp2(b_sub)).astype(BF16)

    zeros2 = lambda n: jnp.zeros((n, dk), BF16)
    qs_parts, ks_parts = [], []
    for n in range(1, nc):
        lo, hi = n * C, (n + 1) * C
        decay = jnp.exp2(after[:lo] - after[lo - 1:lo])
        ks_parts.append(jnp.concatenate([(k[:lo] * decay).astype(BF16), zeros2(ts - lo)], axis=0))
        pieces = [zeros2(lo), q_chunk[lo:hi]] + ([zeros2(ts - hi)] if hi < ts else [])
        qs_parts.append(jnp.concatenate(pieces, axis=0))
    a_chunks = _dot_nt(jnp.concatenate(qs_parts, axis=1), jnp.concatenate(ks_parts, axis=1))

    k3 = k.reshape(nc, C, dk)
    qs3 = q_sub.reshape(nc, C, dk)
    zeros3 = lambda n: jnp.zeros((nc, n, dk), BF16)
    qs_parts, ks_parts = [], []
    for i in range(1, nsub):
        lo, hi = i * c, (i + 1) * c
        decay = jnp.exp2(to_chunk_end[:, :lo] - to_chunk_end[:, lo - 1:lo])
        ks_parts.append(jnp.concatenate([(k3[:, :lo] * decay).astype(BF16), zeros3(C - lo)], axis=1))
        pieces = [zeros3(lo), qs3[:, lo:hi]] + ([zeros3(C - hi)] if hi < C else [])
        qs_parts.append(jnp.concatenate(pieces, axis=1))
    a_subs = _dot_nt(jnp.concatenate(qs_parts, axis=2).reshape(ts, (nsub - 1) * dk),
                     jnp.concatenate(ks_parts, axis=2).reshape(ts, (nsub - 1) * dk))

    k_scr[...] = k
    bs_scr[...] = b_sub
    zero_tile = jnp.zeros((F32_SUBLANES, dk), F32)
    for n in range(ng):
        tiles = [slice(n * c + lo, n * c + lo + F32_SUBLANES) for lo in range(0, c, F32_SUBLANES)]
        for j in range(c):
            row = slice(n * c + j, n * c + j + 1)
            k_j, b_j = k_scr[row, :], bs_scr[row, :]
            e = [q[t] * k_j * jnp.exp2(jnp.minimum(b_sub[t] - b_j, 0.0))
                 if t.stop > n * c + j else zero_tile for t in tiles]
            e_scr[n * c:(n + 1) * c, j * dk:(j + 1) * dk] = jnp.concatenate(e, axis=0).astype(BF16)
    a_rep = _dot(e_scr[...], reduce_j)

    rr = lax.broadcasted_iota(jnp.int32, (ts, ts), 0)
    ll = lax.broadcasted_iota(jnp.int32, (ts, ts), 1)
    lc, ls = C.bit_length() - 1, c.bit_length() - 1
    same_chunk = (rr >> lc) == (ll >> lc)
    own_sub = ((rr >> ls) == (ll >> ls)) & ((ll & (c - 1)) <= (rr & (c - 1)))
    a = jnp.where(own_sub, a_rep, 0.0) + jnp.where(same_chunk, a_subs, 0.0) + a_chunks

    o = _dot(a.astype(BF16), v) + _dot_nt(q_blk, st.astype(BF16))
    return o, st * jnp.exp2(b_last) + _dot_tn(v, k_blk)


def _hgrn_rec_call(q, k, v, lf, sg, gain, *, heads, ts=256, heads_per_step=2):
    bsz, seq, width = q.shape
    dk = width // heads
    blk = pl.BlockSpec((1, ts, heads_per_step * dk), lambda b, h, s: (b, s, h))
    prefix, reduce_j = _hgrn_rec_constants(ts, dk)
    kern = functools.partial(_hgrn_rec_kernel, ts=ts)
    return pl.pallas_call(
        kern, grid=(bsz, heads // heads_per_step, seq // ts),
        in_specs=[blk, blk, blk, blk, blk, _const_spec(gain.shape), _const_spec(prefix.shape),
                  _const_spec(reduce_j.shape)],
        out_specs=blk, out_shape=jax.ShapeDtypeStruct((bsz, seq, width), BF16),
        scratch_shapes=[pltpu.VMEM((heads_per_step, dk, dk), F32),
                        pltpu.VMEM((heads_per_step, ts, dk), F32),
                        pltpu.VMEM((heads_per_step, ts, dk), F32),
                        pltpu.VMEM((heads_per_step, ts, A_SUB * dk), BF16)],
        compiler_params=_params(3), name="hgrn_rec")(q, k, v, lf, sg, gain, prefix, reduce_j)


def _rope_tables_lanes(pos_row, inv_freq_col, half):
    ang = inv_freq_col * pos_row.astype(F32)
    cos, sin = jnp.cos(ang), jnp.sin(ang)
    zeros = jnp.zeros((128 - 2 * half, ang.shape[1]), F32)
    c_tab = jnp.transpose(jnp.concatenate([cos, cos, zeros], axis=0))
    s_tab = jnp.transpose(jnp.concatenate([-sin, sin, zeros], axis=0))
    return c_tab, s_tab


def _kv_kernel(x_ref, pos_ref, g_ref, wdc_ref, wdr_ref, gl_ref, wk_ref, wvt_ref, freq_ref,
               k_out, vt_out, *, heads, nope, vdim, half):
    h = _rms(x_ref[0], g_ref[...]).astype(BF16)
    c_kv = _rms(_dot(h, wdc_ref[...]), gl_ref[...]).astype(BF16)
    kr = _dot(h, wdr_ref[...])
    c_tab, s_tab = _rope_tables_lanes(pos_ref[0], freq_ref[...], half)
    k_rope = (kr * c_tab + pltpu.roll(kr, half, 1) * s_tab).astype(BF16)
    k_nope = _dot(c_kv, wk_ref[...])
    v_t = _dot_nt(wvt_ref[...], c_kv)
    ones = jnp.ones((vt_out.shape[2] - vdim, v_t.shape[1]), BF16)
    for hd in range(heads):
        k_out[0, hd, :, 0:nope] = k_nope[:, hd * nope:(hd + 1) * nope].astype(BF16)
        k_out[0, hd, :, nope:] = k_rope
        vt_out[0, hd, 0:vdim, :] = v_t[hd * vdim:(hd + 1) * vdim].astype(BF16)
        vt_out[0, hd, vdim:, :] = ones


def _kv_call(x, pos_row, g, wd_c, wd_r, gl, wk, wvt, freq_col, *, heads, nope, vdim, half, tm=256):
    bsz, seq, d = x.shape
    kern = functools.partial(_kv_kernel, heads=heads, nope=nope, vdim=vdim, half=half)
    kd = nope + 128
    vrows = vdim + BF16_SUBLANES
    return pl.pallas_call(
        kern, grid=(bsz, seq // tm),
        in_specs=[pl.BlockSpec((1, tm, d), lambda b, s: (b, s, 0)),
                  pl.BlockSpec((1, 1, tm), lambda b, s: (b, 0, s)),
                  _const_spec(g.shape), _const_spec(wd_c.shape), _const_spec(wd_r.shape),
                  _const_spec(gl.shape), _const_spec(wk.shape), _const_spec(wvt.shape),
                  _const_spec(freq_col.shape)],
        out_specs=[pl.BlockSpec((1, heads, tm, kd), lambda b, s: (b, 0, s, 0)),
                   pl.BlockSpec((1, heads, vrows, tm), lambda b, s: (b, 0, 0, s))],
        out_shape=[jax.ShapeDtypeStruct((bsz, heads, seq, kd), BF16),
                   jax.ShapeDtypeStruct((bsz, heads, vrows, seq), BF16)],
        compiler_params=_params(2), name="mla_shared_kv")(
            x, pos_row, g, wd_c, wd_r, gl, wk, wvt, freq_col)


def _q_kernel(x_ref, pos_ref, g_ref, wdq_ref, gq_ref, wuqt_ref, freq_ref, qt_out,
              *, heads, nope, half, qd, scale):
    h = _rms(x_ref[0], g_ref[...]).astype(BF16)
    c_q = (_rms(_dot(h, wdq_ref[...]), gq_ref[...]) * scale).astype(BF16)
    q_t = _dot_nt(wuqt_ref[...], c_q)
    ang = freq_ref[...] * pos_ref[0].astype(F32)
    cos, sin = jnp.cos(ang), jnp.sin(ang)
    zeros = jnp.zeros((qd - nope - 2 * half, q_t.shape[1]), F32)
    for hd in range(heads):
        base = hd * qd
        x1 = q_t[base + nope:base + nope + half]
        x2 = q_t[base + nope + half:base + nope + 2 * half]
        full = jnp.concatenate(
            [q_t[base:base + nope], x1 * cos - x2 * sin, x2 * cos + x1 * sin, zeros], axis=0)
        qt_out[0, hd] = full.astype(BF16)


def _q_call(x, pos_row, g, w_dq, gq, w_uqt, freq_col, *, heads, nope, half, qd, scale, tm=256):
    bsz, seq, d = x.shape
    kern = functools.partial(_q_kernel, heads=heads, nope=nope, half=half, qd=qd, scale=scale)
    return pl.pallas_call(
        kern, grid=(bsz, seq // tm),
        in_specs=[pl.BlockSpec((1, tm, d), lambda b, s: (b, s, 0)),
                  pl.BlockSpec((1, 1, tm), lambda b, s: (b, 0, s)),
                  _const_spec(g.shape), _const_spec(w_dq.shape), _const_spec(gq.shape),
                  _const_spec(w_uqt.shape), _const_spec(freq_col.shape)],
        out_specs=pl.BlockSpec((1, heads, qd, tm), lambda b, s: (b, 0, 0, s)),
        out_shape=jax.ShapeDtypeStruct((bsz, heads, qd, seq), BF16),
        compiler_params=_params(2), name="mla_q")(x, pos_row, g, w_dq, gq, w_uqt, freq_col)


def _attn_kernel(qt_ref, k_ref, vt_ref, o_ref, s0, s1, p0, p1, acc_ref, *, tq, tk, dv):
    qi = pl.program_id(2)

    def key_block(j):
        return k_ref[0, 0, pl.ds(pl.multiple_of(j * tk, tk), tk), :]

    def value_block(j):
        return vt_ref[0, 0, :, pl.ds(pl.multiple_of(j * tk, tk), tk)]

    def scores(j, s_out):
        s = _dot(key_block(j), qt_ref[0, 0])
        s_out[...] = s
        return jnp.max(s, axis=0, keepdims=True)

    def probabilities(s, m, m_blk):
        m_new = jnp.maximum(m, m_blk)
        return m_new, jnp.exp2(m - m_new), jnp.exp2((s - m_new).astype(BF16))

    def softmax_step(s_in, p_out, m, m_blk):
        m_new, alpha, p = probabilities(s_in[...], m, m_blk)
        p_out[...] = p
        return m_new, alpha

    def accumulate(j, p_in, alpha):
        acc_ref[...] = alpha * acc_ref[...] + _dot(value_block(j), p_in[...])

    def body(j, carry):
        a_prev, m, mb0 = carry
        mb1 = scores(2 * j + 1, s1)
        m, a0 = softmax_step(s0, p0, m, mb0)
        accumulate(jnp.maximum(2 * j - 1, 0), p1, a_prev)
        mb0 = scores(2 * j + 2, s0)
        m, a1 = softmax_step(s1, p1, m, mb1)
        accumulate(2 * j, p0, a0)
        return a1, m, mb0

    acc_ref[...] = jnp.zeros_like(acc_ref)
    p1[...] = jnp.zeros_like(p1)
    init = (jnp.ones((1, tq), F32), jnp.full((1, tq), NEG_BIG, F32), scores(0, s0))
    a_prev, m, _ = lax.fori_loop(0, qi, body, init)

    causal = (lax.broadcasted_iota(jnp.int32, (tk, tk), 0)
              <= lax.broadcasted_iota(jnp.int32, (tk, tk), 1))
    s_b = _dot(key_block(2 * qi + 1), qt_ref[0, 0, :, tk:])
    s_b = jnp.where(causal, s_b, NEG_BIG)
    s_a = s0[...]
    s_a = jnp.concatenate([jnp.where(causal, s_a[:, :tk], NEG_BIG), s_a[:, tk:]], axis=1)
    m, a0, p_a = probabilities(s_a, m, jnp.max(s_a, axis=0, keepdims=True))
    p0[...] = p_a
    accumulate(jnp.maximum(2 * qi - 1, 0), p1, a_prev)
    accumulate(2 * qi, p0, a0)
    _, a1, p_b = probabilities(s_b, m[:, tk:], jnp.max(s_b, axis=0, keepdims=True))
    acc_ref[:, tk:] = a1 * acc_ref[:, tk:] + _dot(value_block(2 * qi + 1), p_b)
    acc = acc_ref[...]
    o_ref[0] = jnp.transpose(acc[:dv] / acc[dv:dv + 1]).astype(BF16)


def _attn_call(qt, k, vt, *, tq=1024):
    tk = tq // 2
    bsz, heads, qd, seq = qt.shape
    vrows = vt.shape[2]
    dv = vrows - BF16_SUBLANES
    kern = functools.partial(_attn_kernel, tq=tq, tk=tk, dv=dv)
    return pl.pallas_call(
        kern, grid=(bsz, heads, seq // tq),
        in_specs=[pl.BlockSpec((1, 1, qd, tq), lambda b, h, i: (b, h, 0, i)),
                  pl.BlockSpec((1, 1, seq, k.shape[3]), lambda b, h, i: (b, h, 0, 0)),
                  pl.BlockSpec((1, 1, vrows, seq), lambda b, h, i: (b, h, 0, 0))],
        out_specs=pl.BlockSpec((1, tq, dv), lambda b, h, i: (b, i, h)),
        out_shape=jax.ShapeDtypeStruct((bsz, seq, heads * dv), BF16),
        scratch_shapes=[pltpu.VMEM((tk, tq), F32), pltpu.VMEM((tk, tq), F32),
                        pltpu.VMEM((tk, tq), BF16), pltpu.VMEM((tk, tq), BF16),
                        pltpu.VMEM((vrows, tq), F32)],
        compiler_params=_params(3), name="mla_attention")(qt, k, vt)


def kernel(x, p, positions, norm_gains, ffn_w_in, ffn_w_out, ple_w_gate, ple_w_in, a_w_in,
           a_lb_logits, a_out_gain, a_w_out, kv_norm_in, kv_w_down, kv_latent_norm, kv_w_up,
           b_w_dq, b_q_norm, b_w_uq, b_w_out, final_norm):
    bsz, seq, d = x.shape
    depth = norm_gains.shape[0]
    n_a = a_w_in.shape[0]
    t = bsz * seq
    bf = lambda w: w.astype(BF16)
    row = lambda g: g.reshape(1, -1).astype(F32)

    a_key = a_out_gain.shape[1]
    a_heads = a_w_out.shape[1] // a_key
    kv_lora = kv_latent_norm.shape[0]
    rope = kv_w_down.shape[1] - kv_lora
    half = rope // 2
    vdim = 128
    heads = b_w_out.shape[1] // vdim
    nope = kv_w_up.shape[1] // heads - vdim
    qd = 256
    assert nope == 128 and rope == 64 and b_w_uq.shape[2] == heads * (nope + rope)

    inv_freq = 1.0 / (ROPE_THETA ** (jnp.arange(0, rope, 2, dtype=F32) / rope))
    freq_col = inv_freq.reshape(half, 1)
    pos_row = positions.reshape(bsz, 1, seq)

    wd_c = bf(kv_w_down[:, :kv_lora])
    wd_r = bf(jnp.concatenate([kv_w_down[:, kv_lora:]] * (128 // rope), axis=1))
    w_up = kv_w_up.reshape(kv_lora, heads, nope + vdim)
    wk = bf(w_up[:, :, :nope].reshape(kv_lora, heads * nope))
    wvt = bf(w_up[:, :, nope:].reshape(kv_lora, heads * vdim).T)

    def shared_kv(xs):
        return _kv_call(xs.reshape(bsz, seq, d), pos_row, row(kv_norm_in), wd_c, wd_r,
                        row(kv_latent_norm), wk, wvt, freq_col,
                        heads=heads, nope=nope, vdim=vdim, half=half)

    xf = x.reshape(t, d)
    shared = shared_kv(xf) if n_a == 0 else None
    attn_scale = float(nope + rope) ** -0.5 * LOG2_E
    for li in range(depth):
        g = norm_gains[li]
        pre = None
        if li < n_a:
            x1 = _block_call(xf, row(g[0]), bf(ffn_w_in[li, 0]), bf(ffn_w_out[li, 0]))
            q, k, v, lf, sg = _hgrn_proj_call(x1, row(g[1]), bf(a_w_in[li]),
                                              a_lb_logits.astype(F32), layer=li, key_dim=a_key)
            r3 = lambda a: a.reshape(bsz, seq, -1)
            o = _hgrn_rec_call(r3(q), r3(k), r3(v), r3(lf), r3(sg), row(a_out_gain[li]),
                               heads=a_heads)
            pre = (o.reshape(t, -1), bf(a_w_out[li]))
        else:
            bi = li - n_a
            x1 = _block_call(xf, row(g[0]), bf(ffn_w_in[li, 0]), bf(ffn_w_out[li, 0]))
            w_uq = b_w_uq[bi].reshape(-1, heads, nope + rope)
            w_uq = jnp.pad(w_uq, ((0, 0), (0, 0), (0, qd - nope - rope)))
            w_uqt = bf(w_uq.reshape(-1, heads * qd).T)
            qt = _q_call(x1.reshape(bsz, seq, d), pos_row, row(g[1]), bf(b_w_dq[bi]),
                         row(b_q_norm[bi]), w_uqt, freq_col, heads=heads, nope=nope, half=half,
                         qd=qd, scale=attn_scale)
            k_all, vt_all = shared
            o = _attn_call(qt, k_all, vt_all)
            pre = (o.reshape(t, -1), bf(b_w_out[bi]))
        ple = (row(g[3]), bf(ple_w_gate[li]), p[li].reshape(t, -1), bf(ple_w_in[li]))
        final = row(final_norm) if li == depth - 1 else None
        xf = _block_call(x1, row(g[2]), bf(ffn_w_in[li, 1]), bf(ffn_w_out[li, 1]),
                         pre=pre, ple=ple, final=final)
        if li == n_a - 1:
            shared = shared_kv(xf)
    return xf.reshape(bsz, seq, d)
```

```python
import functools
import math

import jax
import jax.numpy as jnp
from jax import lax
from jax.experimental import pallas as pl
from jax.experimental.pallas import tpu as pltpu

F32 = jnp.float32
BF16 = jnp.bfloat16

EPS = 1e-6
ROPE_THETA = 10000.0
A_CHUNK = 64
A_SUB = 16
A_HEADS_PER_STEP = 2
NEG_BIG = -1e30
LOG2_E = math.log2(math.e)

V7X_VMEM_BYTES = 64 * 1024 * 1024
VMEM_LIMIT = V7X_VMEM_BYTES * 7 // 8
F32_SUBLANES = 8
BF16_SUBLANES = 16


def _rms(x, g):
    ms = jnp.mean(x * x, axis=-1, keepdims=True)
    return x * lax.rsqrt(ms + EPS) * g


def _sigmoid(x):
    return 1.0 / (1.0 + jnp.exp(-x))


def _dot(a, b):
    return jnp.dot(a, b, preferred_element_type=F32)


def _dot_nt(a, b):
    return lax.dot_general(a, b, (((1,), (1,)), ((), ())), preferred_element_type=F32)


def _dot_tn(a, b):
    return lax.dot_general(a, b, (((0,), (0,)), ((), ())), preferred_element_type=F32)


def _const_spec(shape):
    nd = len(shape)
    return pl.BlockSpec(shape, lambda *_: (0,) * nd, pipeline_mode=pl.Buffered(1))


def _params(n_axes):
    return pltpu.CompilerParams(
        dimension_semantics=("arbitrary",) * n_axes, vmem_limit_bytes=VMEM_LIMIT)


def _block_kernel(*refs, d_ff, has_pre, has_ple, has_final):
    it = iter(refs)
    x_ref = next(it)
    if has_pre:
        y_ref, wpre_ref = next(it), next(it)
    g_ref, win_ref, wout_ref = next(it), next(it), next(it)
    if has_ple:
        gp_ref, wg_ref, p_ref, wp_ref = next(it), next(it), next(it), next(it)
    if has_final:
        gf_ref = next(it)
    o_ref = next(it)

    x = x_ref[...]
    if has_pre:
        if len(y_ref.shape) == 4:
            y = jnp.concatenate([y_ref[0, gi] for gi in range(y_ref.shape[1])], axis=1)
        else:
            y = y_ref[...]
        x = x + _dot(y, wpre_ref[...])
    h = _rms(x, g_ref[...]).astype(BF16)
    gate = _dot(h, win_ref[:, :d_ff])
    up = _dot(h, win_ref[:, d_ff:])
    act = (gate * _sigmoid(gate) * up).astype(BF16)
    x = x + 0.5 * _dot(act, wout_ref[...])
    if has_ple:
        hg = _rms(x, gp_ref[...]).astype(BF16)
        emb_gate = _sigmoid(_dot(hg, wg_ref[...]))
        emb = _dot(p_ref[...].astype(BF16), wp_ref[...])
        x = x + emb_gate * emb
    if has_final:
        x = _rms(x, gf_ref[...])
    o_ref[...] = x


def _block_call(x, g, w_in, w_out, *, pre=None, ple=None, final=None, tm=256):
    t, d = x.shape
    d_ff = w_out.shape[0]
    row = lambda i: (i, 0)
    args, specs = [x], [pl.BlockSpec((tm, d), row)]
    if pre is not None:
        y, w_pre = pre
        args += [y, w_pre]
        if y.ndim == 4:
            steps = y.shape[2] // tm
            y_spec = pl.BlockSpec((1, y.shape[1], tm, y.shape[3]),
                                  lambda i: (i // steps, 0, i % steps, 0))
        else:
            y_spec = pl.BlockSpec((tm, y.shape[1]), row)
        specs += [y_spec, _const_spec(w_pre.shape)]
    args += [g, w_in, w_out]
    specs += [_const_spec(g.shape), _const_spec(w_in.shape), _const_spec(w_out.shape)]
    if ple is not None:
        gp, wg, p, wp = ple
        args += [gp, wg, p, wp]
        specs += [_const_spec(gp.shape), _const_spec(wg.shape),
                  pl.BlockSpec((tm, p.shape[1]), row), _const_spec(wp.shape)]
    if final is not None:
        args += [final]
        specs += [_const_spec(final.shape)]
    kern = functools.partial(_block_kernel, d_ff=d_ff, has_pre=pre is not None,
                             has_ple=ple is not None, has_final=final is not None)
    return pl.pallas_call(
        kern, grid=(t // tm,), in_specs=specs, out_specs=pl.BlockSpec((tm, d), row),
        out_shape=jax.ShapeDtypeStruct((t, d), F32), compiler_params=_params(1),
        name="token_block")(*args)


def _hgrn_proj_kernel(x_ref, g_ref, w_ref, lbl_ref, q_ref, k_ref, v_ref, lf_ref, sg_ref,
                      *, width, layer, q_scale):
    h = _rms(x_ref[...], g_ref[...]).astype(BF16)
    q = _dot(h, w_ref[:, 0 * width:1 * width])
    f = _dot(h, w_ref[:, 1 * width:2 * width])
    v = _dot(h, w_ref[:, 2 * width:3 * width])
    og = _dot(h, w_ref[:, 3 * width:4 * width])
    lbl = lbl_ref[...]
    e = jnp.exp(lbl - jnp.max(lbl, axis=0, keepdims=True))
    sm = e / jnp.sum(e, axis=0, keepdims=True)
    lb = jnp.sum(sm[:layer + 1], axis=0, keepdims=True)
    sf = _sigmoid(f)
    outs = ((q_ref, q * q_scale), (k_ref, (1.0 - lb) * _sigmoid(-f)), (v_ref, v),
            (lf_ref, jnp.log(lb + (1.0 - lb) * sf)), (sg_ref, _sigmoid(og)))
    gw = q_ref.shape[3]
    for ref, val in outs:
        for gi in range(ref.shape[1]):
            ref[0, gi] = val[:, gi * gw:(gi + 1) * gw].astype(ref.dtype)


def _hgrn_proj_call(x, g, w_in, lb_logits, *, bsz, layer, key_dim, group_width, tm=256):
    t, d = x.shape
    seq = t // bsz
    steps = seq // tm
    width = w_in.shape[1] // 4
    groups = width // group_width
    kern = functools.partial(_hgrn_proj_kernel, width=width, layer=layer,
                             q_scale=float(key_dim) ** -0.5)
    out_spec = pl.BlockSpec((1, groups, tm, group_width), lambda i: (i // steps, 0, i % steps, 0))
    shape = (bsz, groups, seq, group_width)
    bf = jax.ShapeDtypeStruct(shape, BF16)
    return pl.pallas_call(
        kern, grid=(t // tm,),
        in_specs=[pl.BlockSpec((tm, d), lambda i: (i, 0)), _const_spec(g.shape),
                  _const_spec(w_in.shape), _const_spec(lb_logits.shape)],
        out_specs=[out_spec] * 5,
        out_shape=[bf, bf, bf, jax.ShapeDtypeStruct(shape, F32), bf],
        compiler_params=_params(1), name="hgrn_proj")(x, g, w_in, lb_logits)


def _hgrn_rec_constants(ts, dk):
    incl = jnp.arange(ts)[None, :] <= jnp.arange(ts)[:, None]
    prefix = jnp.concatenate([incl, incl, incl], axis=1).astype(BF16)
    col = jnp.arange(ts)[None, :]
    reduce_j = (jnp.arange(A_SUB * dk)[:, None] // dk) == (col % A_SUB)
    return prefix, reduce_j.astype(BF16)


def _hgrn_rec_kernel(q_ref, k_ref, v_ref, lf_ref, sg_ref, gain_ref, prefix_ref, reduce_ref,
                     o_ref, st_ref, k_scr, bs_scr, e_scr, *, ts):
    @pl.when(pl.program_id(2) == 0)
    def _():
        st_ref[...] = jnp.zeros_like(st_ref)

    dk = gain_ref.shape[-1]
    for hd in range(st_ref.shape[0]):
        lanes = slice(hd * dk, (hd + 1) * dk)
        o, st_ref[hd] = _hgrn_block(
            q_ref[0, 0, :, lanes].astype(F32), k_ref[0, 0, :, lanes].astype(F32),
            v_ref[0, 0, :, lanes], lf_ref[0, 0, :, lanes] * LOG2_E, st_ref[hd],
            prefix_ref[...], reduce_ref[...], k_scr.at[hd], bs_scr.at[hd], e_scr.at[hd], ts=ts)
        o = _rms(o, gain_ref[...]) * sg_ref[0, 0, :, lanes].astype(F32)
        o_ref[0, 0, :, lanes] = o.astype(BF16)


def _hgrn_block(q, k, v, g, st, prefix, reduce_j, k_scr, bs_scr, e_scr, *, ts):
    C, c = A_CHUNK, A_SUB
    nc, nsub, ng, dk = ts // C, C // c, ts // c, q.shape[-1]

    g1 = g.astype(BF16)
    r1 = g - g1.astype(F32)
    g2 = r1.astype(BF16)
    g3 = (r1 - g2.astype(F32)).astype(BF16)
    big_b = _dot(prefix, jnp.concatenate([g1, g2, g3], axis=0))
    b_last = big_b[ts - 1:ts]
    after = b_last - big_b

    def since_start_of(size):
        groups = [big_b[:size]] + [big_b[lo:lo + size] - big_b[lo - 1:lo]
                                   for lo in range(size, ts, size)]
        return jnp.concatenate(groups, axis=0)

    b_chunk = since_start_of(C)
    b_sub = since_start_of(c)
    to_chunk_end = jnp.concatenate(
        [big_b[lo + C - 1:lo + C] - big_b[lo:lo + C] for lo in range(0, ts, C)],
        axis=0).reshape(nc, C, dk)

    q_blk = (q * jnp.exp2(big_b)).astype(BF16)
    k_blk = (k * jnp.exp2(after)).astype(BF16)
    q_chunk = (q * jnp.exp2(b_chunk)).astype(BF16)
    q_sub = (q * jnp.exp2(b_sub)).astype(BF16)

    zeros2 = lambda n: jnp.zeros((n, dk), BF16)
    qs_parts, ks_parts = [], []
    for n in range(1, nc):
        lo, hi = n * C, (n + 1) * C
        decay = jnp.exp2(after[:lo] - after[lo - 1:lo])
        ks_parts.append(jnp.concatenate([(k[:lo] * decay).astype(BF16), zeros2(ts - lo)], axis=0))
        pieces = [zeros2(lo), q_chunk[lo:hi]] + ([zeros2(ts - hi)] if hi < ts else [])
        qs_parts.append(jnp.concatenate(pieces, axis=0))
    a_chunks = _dot_nt(jnp.concatenate(qs_parts, axis=1), jnp.concatenate(ks_parts, axis=1))

    k3 = k.reshape(nc, C, dk)
    qs3 = q_sub.reshape(nc, C, dk)
    zeros3 = lambda n: jnp.zeros((nc, n, dk), BF16)
    qs_parts, ks_parts = [], []
    for i in range(1, nsub):
        lo, hi = i * c, (i + 1) * c
        decay = jnp.exp2(to_chunk_end[:, :lo] - to_chunk_end[:, lo - 1:lo])
        ks_parts.append(jnp.concatenate([(k3[:, :lo] * decay).astype(BF16), zeros3(C - lo)], axis=1))
        pieces = [zeros3(lo), qs3[:, lo:hi]] + ([zeros3(C - hi)] if hi < C else [])
        qs_parts.append(jnp.concatenate(pieces, axis=1))
    a_subs = _dot_nt(jnp.concatenate(qs_parts, axis=2).reshape(ts, (nsub - 1) * dk),
                     jnp.concatenate(ks_parts, axis=2).reshape(ts, (nsub - 1) * dk))

    k_scr[...] = k
    bs_scr[...] = b_sub
    zero_tile = jnp.zeros((F32_SUBLANES, dk), F32)
    for n in range(ng):
        tiles = [slice(n * c + lo, n * c + lo + F32_SUBLANES) for lo in range(0, c, F32_SUBLANES)]
        for j in range(c):
            row = slice(n * c + j, n * c + j + 1)
            k_j, b_j = k_scr[row, :], bs_scr[row, :]
            e = [q[t] * k_j * jnp.exp2(jnp.minimum(b_sub[t] - b_j, 0.0))
                 if t.stop > n * c + j else zero_tile for t in tiles]
            e_scr[n * c:(n + 1) * c, j * dk:(j + 1) * dk] = jnp.concatenate(e, axis=0).astype(BF16)
    a_rep = _dot(e_scr[...], reduce_j)

    rr = lax.broadcasted_iota(jnp.int32, (ts, ts), 0)
    ll = lax.broadcasted_iota(jnp.int32, (ts, ts), 1)
    lc, ls = C.bit_length() - 1, c.bit_length() - 1
    same_chunk = (rr >> lc) == (ll >> lc)
    own_sub = ((rr >> ls) == (ll >> ls)) & ((ll & (c - 1)) <= (rr & (c - 1)))
    a = jnp.where(own_sub, a_rep, 0.0) + jnp.where(same_chunk, a_subs, 0.0) + a_chunks

    o = _dot(a.astype(BF16), v) + _dot_nt(q_blk, st.astype(BF16))
    return o, st * jnp.exp2(b_last) + _dot_tn(v, k_blk)


def _hgrn_rec_call(q, k, v, lf, sg, gain, *, ts=256):
    bsz, groups, seq, group_width = q.shape
    dk = gain.shape[-1]
    heads_per_step = group_width // dk
    blk = pl.BlockSpec((1, 1, ts, group_width), lambda b, h, s: (b, h, s, 0))
    prefix, reduce_j = _hgrn_rec_constants(ts, dk)
    kern = functools.partial(_hgrn_rec_kernel, ts=ts)
    return pl.pallas_call(
        kern, grid=(bsz, groups, seq // ts),
        in_specs=[blk, blk, blk, blk, blk, _const_spec(gain.shape), _const_spec(prefix.shape),
                  _const_spec(reduce_j.shape)],
        out_specs=blk, out_shape=jax.ShapeDtypeStruct(q.shape, BF16),
        scratch_shapes=[pltpu.VMEM((heads_per_step, dk, dk), F32),
                        pltpu.VMEM((heads_per_step, ts, dk), F32),
                        pltpu.VMEM((heads_per_step, ts, dk), F32),
                        pltpu.VMEM((heads_per_step, ts, A_SUB * dk), BF16)],
        compiler_params=_params(3), name="hgrn_rec")(q, k, v, lf, sg, gain, prefix, reduce_j)


def _rope_tables_lanes(pos_row, inv_freq_col, half):
    ang = inv_freq_col * pos_row.astype(F32)
    cos, sin = jnp.cos(ang), jnp.sin(ang)
    zeros = jnp.zeros((128 - 2 * half, ang.shape[1]), F32)
    c_tab = jnp.transpose(jnp.concatenate([cos, cos, zeros], axis=0))
    s_tab = jnp.transpose(jnp.concatenate([-sin, sin, zeros], axis=0))
    return c_tab, s_tab


def _kv_kernel(x_ref, pos_ref, g_ref, wdc_ref, wdr_ref, gl_ref, wk_ref, wvt_ref, freq_ref,
               k_out, vt_out, *, heads, nope, vdim, half):
    h = _rms(x_ref[0], g_ref[...]).astype(BF16)
    c_kv = _rms(_dot(h, wdc_ref[...]), gl_ref[...]).astype(BF16)
    kr = _dot(h, wdr_ref[...])
    c_tab, s_tab = _rope_tables_lanes(pos_ref[0], freq_ref[...], half)
    k_rope = (kr * c_tab + pltpu.roll(kr, half, 1) * s_tab).astype(BF16)
    k_nope = _dot(c_kv, wk_ref[...])
    v_t = _dot_nt(wvt_ref[...], c_kv)
    ones = jnp.ones((vt_out.shape[2] - vdim, v_t.shape[1]), BF16)
    for hd in range(heads):
        k_out[0, hd, :, 0:nope] = k_nope[:, hd * nope:(hd + 1) * nope].astype(BF16)
        k_out[0, hd, :, nope:] = k_rope
        vt_out[0, hd, 0:vdim, :] = v_t[hd * vdim:(hd + 1) * vdim].astype(BF16)
        vt_out[0, hd, vdim:, :] = ones


def _kv_call(x, pos_row, g, wd_c, wd_r, gl, wk, wvt, freq_col, *, heads, nope, vdim, half, tm=256):
    bsz, seq, d = x.shape
    kern = functools.partial(_kv_kernel, heads=heads, nope=nope, vdim=vdim, half=half)
    kd = nope + 128
    vrows = vdim + BF16_SUBLANES
    return pl.pallas_call(
        kern, grid=(bsz, seq // tm),
        in_specs=[pl.BlockSpec((1, tm, d), lambda b, s: (b, s, 0)),
                  pl.BlockSpec((1, 1, tm), lambda b, s: (b, 0, s)),
                  _const_spec(g.shape), _const_spec(wd_c.shape), _const_spec(wd_r.shape),
                  _const_spec(gl.shape), _const_spec(wk.shape), _const_spec(wvt.shape),
                  _const_spec(freq_col.shape)],
        out_specs=[pl.BlockSpec((1, heads, tm, kd), lambda b, s: (b, 0, s, 0)),
                   pl.BlockSpec((1, heads, vrows, tm), lambda b, s: (b, 0, 0, s))],
        out_shape=[jax.ShapeDtypeStruct((bsz, heads, seq, kd), BF16),
                   jax.ShapeDtypeStruct((bsz, heads, vrows, seq), BF16)],
        compiler_params=_params(2), name="mla_shared_kv")(
            x, pos_row, g, wd_c, wd_r, gl, wk, wvt, freq_col)


def _q_kernel(x_ref, pos_ref, g_ref, wdq_ref, gq_ref, wuqt_ref, freq_ref, qt_out,
              *, heads, nope, half, qd, scale):
    h = _rms(x_ref[0], g_ref[...]).astype(BF16)
    c_q = (_rms(_dot(h, wdq_ref[...]), gq_ref[...]) * scale).astype(BF16)
    q_t = _dot_nt(wuqt_ref[...], c_q)
    ang = freq_ref[...] * pos_ref[0].astype(F32)
    cos, sin = jnp.cos(ang), jnp.sin(ang)
    zeros = jnp.zeros((qd - nope - 2 * half, q_t.shape[1]), F32)
    for hd in range(heads):
        base = hd * qd
        x1 = q_t[base + nope:base + nope + half]
        x2 = q_t[base + nope + half:base + nope + 2 * half]
        full = jnp.concatenate(
            [q_t[base:base + nope], x1 * cos - x2 * sin, x2 * cos + x1 * sin, zeros], axis=0)
        qt_out[0, hd] = full.astype(BF16)


def _q_call(x, pos_row, g, w_dq, gq, w_uqt, freq_col, *, heads, nope, half, qd, scale, tm=256):
    bsz, seq, d = x.shape
    kern = functools.partial(_q_kernel, heads=heads, nope=nope, half=half, qd=qd, scale=scale)
    return pl.pallas_call(
        kern, grid=(bsz, seq // tm),
        in_specs=[pl.BlockSpec((1, tm, d), lambda b, s: (b, s, 0)),
                  pl.BlockSpec((1, 1, tm), lambda b, s: (b, 0, s)),
                  _const_spec(g.shape), _const_spec(w_dq.shape), _const_spec(gq.shape),
                  _const_spec(w_uqt.shape), _const_spec(freq_col.shape)],
        out_specs=pl.BlockSpec((1, heads, qd, tm), lambda b, s: (b, 0, 0, s)),
        out_shape=jax.ShapeDtypeStruct((bsz, heads, qd, seq), BF16),
        compiler_params=_params(2), name="mla_q")(x, pos_row, g, w_dq, gq, w_uqt, freq_col)


def _attn_kernel(qt_ref, k_ref, vt_ref, o_ref, s0, s1, p0, p1, acc_ref, *, tq, tk, dv):
    qi = pl.program_id(2)

    def key_block(j):
        return k_ref[0, 0, pl.ds(pl.multiple_of(j * tk, tk), tk), :]

    def value_block(j):
        return vt_ref[0, 0, :, pl.ds(pl.multiple_of(j * tk, tk), tk)]

    def scores(j, s_out):
        s = _dot(key_block(j), qt_ref[0, 0])
        s_out[...] = s
        return jnp.max(s, axis=0, keepdims=True)

    def probabilities(s, m, m_blk):
        m_new = jnp.maximum(m, m_blk)
        return m_new, jnp.exp2(m - m_new), jnp.exp2((s - m_new).astype(BF16))

    def softmax_step(s_in, p_out, m, m_blk):
        m_new, alpha, p = probabilities(s_in[...], m, m_blk)
        p_out[...] = p
        return m_new, alpha

    def accumulate(j, p_in, alpha):
        acc_ref[...] = alpha * acc_ref[...] + _dot(value_block(j), p_in[...])

    def body(j, carry):
        a_prev, m, mb0 = carry
        mb1 = scores(2 * j + 1, s1)
        m, a0 = softmax_step(s0, p0, m, mb0)
        accumulate(jnp.maximum(2 * j - 1, 0), p1, a_prev)
        mb0 = scores(2 * j + 2, s0)
        m, a1 = softmax_step(s1, p1, m, mb1)
        accumulate(2 * j, p0, a0)
        return a1, m, mb0

    n_diag = tq // tk
    n_full = n_diag * qi
    acc_ref[...] = jnp.zeros_like(acc_ref)
    p1[...] = jnp.zeros_like(p1)
    init = (jnp.ones((1, tq), F32), jnp.full((1, tq), NEG_BIG, F32), scores(0, s0))
    a_prev, m, _ = lax.fori_loop(0, (n_diag // 2) * qi, body, init)
    accumulate(jnp.maximum(n_full - 1, 0), p1, a_prev)

    causal = (lax.broadcasted_iota(jnp.int32, (tk, tk), 0)
              <= lax.broadcasted_iota(jnp.int32, (tk, tk), 1))
    for d in range(n_diag):
        lo = d * tk
        s = s0[...] if d == 0 else _dot(key_block(n_full + d), qt_ref[0, 0, :, lo:])
        masked = jnp.where(causal, s[:, :tk], NEG_BIG)
        s = jnp.concatenate([masked, s[:, tk:]], axis=1) if lo + tk < tq else masked
        m_d, a_d, p_d = probabilities(s, m[:, lo:], jnp.max(s, axis=0, keepdims=True))
        m = jnp.concatenate([m[:, :lo], m_d], axis=1) if lo else m_d
        acc_ref[:, lo:] = a_d * acc_ref[:, lo:] + _dot(value_block(n_full + d), p_d)
    acc = acc_ref[...]
    o_ref[0] = jnp.transpose(acc[:dv] / acc[dv:dv + 1]).astype(BF16)


def _attn_call(qt, k, vt, *, tq=2048, tk=512):
    assert (tq // tk) % 2 == 0
    bsz, heads, qd, seq = qt.shape
    vrows = vt.shape[2]
    dv = vrows - BF16_SUBLANES
    kern = functools.partial(_attn_kernel, tq=tq, tk=tk, dv=dv)
    return pl.pallas_call(
        kern, grid=(bsz, heads, seq // tq),
        in_specs=[pl.BlockSpec((1, 1, qd, tq), lambda b, h, i: (b, h, 0, i)),
                  pl.BlockSpec((1, 1, seq, k.shape[3]), lambda b, h, i: (b, h, 0, 0)),
                  pl.BlockSpec((1, 1, vrows, seq), lambda b, h, i: (b, h, 0, 0))],
        out_specs=pl.BlockSpec((1, tq, dv), lambda b, h, i: (b, i, h)),
        out_shape=jax.ShapeDtypeStruct((bsz, seq, heads * dv), BF16),
        scratch_shapes=[pltpu.VMEM((tk, tq), F32), pltpu.VMEM((tk, tq), F32),
                        pltpu.VMEM((tk, tq), BF16), pltpu.VMEM((tk, tq), BF16),
                        pltpu.VMEM((vrows, tq), F32)],
        compiler_params=_params(3), name="mla_attention")(qt, k, vt)


def kernel(x, p, positions, norm_gains, ffn_w_in, ffn_w_out, ple_w_gate, ple_w_in, a_w_in,
           a_lb_logits, a_out_gain, a_w_out, kv_norm_in, kv_w_down, kv_latent_norm, kv_w_up,
           b_w_dq, b_q_norm, b_w_uq, b_w_out, final_norm):
    bsz, seq, d = x.shape
    depth = norm_gains.shape[0]
    n_a = a_w_in.shape[0]
    t = bsz * seq
    bf = lambda w: w.astype(BF16)
    row = lambda g: g.reshape(1, -1).astype(F32)

    a_key = a_out_gain.shape[1]
    kv_lora = kv_latent_norm.shape[0]
    rope = kv_w_down.shape[1] - kv_lora
    half = rope // 2
    vdim = 128
    heads = b_w_out.shape[1] // vdim
    nope = kv_w_up.shape[1] // heads - vdim
    qd = 256
    assert nope == 128 and rope == 64 and b_w_uq.shape[2] == heads * (nope + rope)

    inv_freq = 1.0 / (ROPE_THETA ** (jnp.arange(0, rope, 2, dtype=F32) / rope))
    freq_col = inv_freq.reshape(half, 1)
    pos_row = positions.reshape(bsz, 1, seq)

    wd_c = bf(kv_w_down[:, :kv_lora])
    wd_r = bf(jnp.concatenate([kv_w_down[:, kv_lora:]] * (128 // rope), axis=1))
    w_up = kv_w_up.reshape(kv_lora, heads, nope + vdim)
    wk = bf(w_up[:, :, :nope].reshape(kv_lora, heads * nope))
    wvt = bf(w_up[:, :, nope:].reshape(kv_lora, heads * vdim).T)

    def shared_kv(xs):
        return _kv_call(xs.reshape(bsz, seq, d), pos_row, row(kv_norm_in), wd_c, wd_r,
                        row(kv_latent_norm), wk, wvt, freq_col,
                        heads=heads, nope=nope, vdim=vdim, half=half)

    xf = x.reshape(t, d)
    shared = shared_kv(xf) if n_a == 0 else None
    attn_scale = float(nope + rope) ** -0.5 * LOG2_E
    for li in range(depth):
        g = norm_gains[li]
        pre = None
        if li < n_a:
            x1 = _block_call(xf, row(g[0]), bf(ffn_w_in[li, 0]), bf(ffn_w_out[li, 0]))
            q, k, v, lf, sg = _hgrn_proj_call(
                x1, row(g[1]), bf(a_w_in[li]), a_lb_logits.astype(F32), bsz=bsz, layer=li,
                key_dim=a_key, group_width=A_HEADS_PER_STEP * a_key)
            o = _hgrn_rec_call(q, k, v, lf, sg, row(a_out_gain[li]))
            pre = (o, bf(a_w_out[li]))
        else:
            bi = li - n_a
            x1 = _block_call(xf, row(g[0]), bf(ffn_w_in[li, 0]), bf(ffn_w_out[li, 0]))
            w_uq = b_w_uq[bi].reshape(-1, heads, nope + rope)
            w_uq = jnp.pad(w_uq, ((0, 0), (0, 0), (0, qd - nope - rope)))
            w_uqt = bf(w_uq.reshape(-1, heads * qd).T)
            qt = _q_call(x1.reshape(bsz, seq, d), pos_row, row(g[1]), bf(b_w_dq[bi]),
                         row(b_q_norm[bi]), w_uqt, freq_col, heads=heads, nope=nope, half=half,
                         qd=qd, scale=attn_scale)
            k_all, vt_all = shared
            o = _attn_call(qt, k_all, vt_all)
            pre = (o.reshape(t, -1), bf(b_w_out[bi]))
        ple = (row(g[3]), bf(ple_w_gate[li]), p[li].reshape(t, -1), bf(ple_w_in[li]))
        final = row(final_norm) if li == depth - 1 else None
        xf = _block_call(x1, row(g[2]), bf(ffn_w_in[li, 1]), bf(ffn_w_out[li, 1]),
                         pre=pre, ple=ple, final=final)
        if li == n_a - 1:
            shared = shared_kv(xf)
    return xf.reshape(bsz, seq, d)
```

```python
import functools
import math

import jax
import jax.numpy as jnp
from jax import lax
from jax.experimental import pallas as pl
from jax.experimental.pallas import tpu as pltpu

F32 = jnp.float32
BF16 = jnp.bfloat16

EPS = 1e-6
ROPE_THETA = 10000.0
A_CHUNK = 64
A_SUB = 16
A_HEADS_PER_STEP = 2
A_SAFE_LOG2_DECAY = 96.0
NEG_BIG = -1e30
LOG2_E = math.log2(math.e)

V7X_VMEM_BYTES = 64 * 1024 * 1024
VMEM_LIMIT = V7X_VMEM_BYTES * 7 // 8
F32_SUBLANES = 8
BF16_SUBLANES = 16


def _rms(x, g):
    ms = jnp.mean(x * x, axis=-1, keepdims=True)
    return x * lax.rsqrt(ms + EPS) * g


def _sigmoid(x):
    return 1.0 / (1.0 + jnp.exp(-x))


def _dot(a, b):
    return jnp.dot(a, b, preferred_element_type=F32)


def _dot_nt(a, b):
    return lax.dot_general(a, b, (((1,), (1,)), ((), ())), preferred_element_type=F32)


def _dot_tn(a, b):
    return lax.dot_general(a, b, (((0,), (0,)), ((), ())), preferred_element_type=F32)


def _const_spec(shape):
    nd = len(shape)
    return pl.BlockSpec(shape, lambda *_: (0,) * nd, pipeline_mode=pl.Buffered(1))


def _pick(w):
    arr, lead = w if isinstance(w, tuple) else (w, ())
    rest = arr.shape[len(lead):]
    index = tuple(lead) + (0,) * len(rest)
    spec = pl.BlockSpec((None,) * len(lead) + rest, lambda *_: index,
                        pipeline_mode=pl.Buffered(1))
    return arr, spec


def _params(n_axes):
    return pltpu.CompilerParams(
        dimension_semantics=("arbitrary",) * n_axes, vmem_limit_bytes=VMEM_LIMIT)


def _block_kernel(*refs, d_ff, has_pre, has_ple, has_final):
    it = iter(refs)
    x_ref = next(it)
    if has_pre:
        y_ref, wpre_ref = next(it), next(it)
    g_ref, win_ref, wout_ref = next(it), next(it), next(it)
    if has_ple:
        gp_ref, wg_ref, p_ref, wp_ref = next(it), next(it), next(it), next(it)
    if has_final:
        gf_ref = next(it)
    o_ref = next(it)

    x = x_ref[...]
    if has_pre:
        if len(y_ref.shape) == 4:
            y = jnp.concatenate([y_ref[0, gi] for gi in range(y_ref.shape[1])], axis=1)
        else:
            y = y_ref[...]
        x = x + _dot(y, wpre_ref[...])
    h = _rms(x, g_ref[...]).astype(BF16)
    gate = _dot(h, win_ref[:, :d_ff])
    up = _dot(h, win_ref[:, d_ff:])
    act = (gate * _sigmoid(gate) * up).astype(BF16)
    x = x + 0.5 * _dot(act, wout_ref[...])
    if has_ple:
        hg = _rms(x, gp_ref[...]).astype(BF16)
        emb_gate = _sigmoid(_dot(hg, wg_ref[...]))
        emb = _dot(p_ref[...].astype(BF16), wp_ref[...])
        x = x + emb_gate * emb
    if has_final:
        x = _rms(x, gf_ref[...])
    o_ref[...] = x


def _block_call(x, g, w_in, w_out, *, pre=None, ple=None, final=None, tm=512):
    t, d = x.shape
    row = lambda i: (i, 0)
    args, specs = [x], [pl.BlockSpec((tm, d), row)]

    def add(w):
        arr, spec = _pick(w)
        args.append(arr)
        specs.append(spec)
        return spec.block_shape

    if pre is not None:
        y, w_pre = pre
        args.append(y)
        if y.ndim == 4:
            steps = y.shape[2] // tm
            specs.append(pl.BlockSpec((1, y.shape[1], tm, y.shape[3]),
                                      lambda i: (i // steps, 0, i % steps, 0)))
        else:
            specs.append(pl.BlockSpec((tm, y.shape[1]), row))
        add(w_pre)
    add(g)
    add(w_in)
    d_ff = add(w_out)[-2]
    if ple is not None:
        gp, wg, p, wp = ple
        add(gp)
        add(wg)
        args.append(p[0])
        specs.append(pl.BlockSpec((None, tm, p[0].shape[2]), lambda i: (p[1], i, 0)))
        add(wp)
    if final is not None:
        add(final)
    kern = functools.partial(_block_kernel, d_ff=d_ff, has_pre=pre is not None,
                             has_ple=ple is not None, has_final=final is not None)
    return pl.pallas_call(
        kern, grid=(t // tm,), in_specs=specs, out_specs=pl.BlockSpec((tm, d), row),
        out_shape=jax.ShapeDtypeStruct((t, d), F32), compiler_params=_params(1),
        name="token_block")(*args)


def _hgrn_proj_kernel(x_ref, g_ref, w_ref, lbl_ref, q_ref, k_ref, v_ref, lf_ref, sg_ref,
                      *, width, layer, q_scale):
    h = _rms(x_ref[...], g_ref[...]).astype(BF16)
    q = _dot(h, w_ref[:, 0 * width:1 * width])
    f = _dot(h, w_ref[:, 1 * width:2 * width])
    v = _dot(h, w_ref[:, 2 * width:3 * width])
    og = _dot(h, w_ref[:, 3 * width:4 * width])
    lbl = lbl_ref[...]
    e = jnp.exp(lbl - jnp.max(lbl, axis=0, keepdims=True))
    sm = e / jnp.sum(e, axis=0, keepdims=True)
    lb = jnp.sum(sm[:layer + 1], axis=0, keepdims=True)
    sf = _sigmoid(f)
    outs = ((q_ref, q * q_scale), (k_ref, (1.0 - lb) * _sigmoid(-f)), (v_ref, v),
            (lf_ref, jnp.log(lb + (1.0 - lb) * sf)), (sg_ref, _sigmoid(og)))
    gw = q_ref.shape[3]
    for ref, val in outs:
        for gi in range(ref.shape[1]):
            ref[0, gi] = val[:, gi * gw:(gi + 1) * gw].astype(ref.dtype)


def _hgrn_proj_call(x, g, w_in, lb_logits, *, bsz, layer, key_dim, group_width, tm=512):
    t, d = x.shape
    seq = t // bsz
    steps = seq // tm
    width = w_in.shape[1] // 4
    groups = width // group_width
    kern = functools.partial(_hgrn_proj_kernel, width=width, layer=layer,
                             q_scale=float(key_dim) ** -0.5)
    out_spec = pl.BlockSpec((1, groups, tm, group_width), lambda i: (i // steps, 0, i % steps, 0))
    shape = (bsz, groups, seq, group_width)
    bf = jax.ShapeDtypeStruct(shape, BF16)
    return pl.pallas_call(
        kern, grid=(t // tm,),
        in_specs=[pl.BlockSpec((tm, d), lambda i: (i, 0)), _const_spec(g.shape),
                  _const_spec(w_in.shape), _const_spec(lb_logits.shape)],
        out_specs=[out_spec] * 5,
        out_shape=[bf, bf, bf, jax.ShapeDtypeStruct(shape, F32), bf],
        compiler_params=_params(1), name="hgrn_proj")(x, g, w_in, lb_logits)


def _hgrn_rec_constants(ts, dk):
    incl = jnp.arange(ts)[None, :] <= jnp.arange(ts)[:, None]
    prefix = jnp.concatenate([incl, incl, incl], axis=1).astype(BF16)
    col = jnp.arange(ts)[None, :]
    reduce_j = (jnp.arange(A_SUB * dk)[:, None] // dk) == (col % A_SUB)
    return prefix, reduce_j.astype(BF16)


def _hgrn_rec_kernel(q_ref, k_ref, v_ref, lf_ref, sg_ref, gain_ref, prefix_ref, reduce_ref,
                     o_ref, st_ref, k_scr, bs_scr, e_scr, a_scr, *, ts):
    @pl.when(pl.program_id(2) == 0)
    def _():
        st_ref[...] = jnp.zeros_like(st_ref)

    dk = gain_ref.shape[-1]
    heads = range(st_ref.shape[0])
    lanes = [slice(hd * dk, (hd + 1) * dk) for hd in heads]
    parts = [_hgrn_decays(q_ref[0, 0, :, ln].astype(F32), k_ref[0, 0, :, ln].astype(F32),
                          lf_ref[0, 0, :, ln] * LOG2_E, prefix_ref[...], ts=ts) for ln in lanes]
    steepest = functools.reduce(jnp.maximum, [jnp.max(-pt["b_sub"]) for pt in parts])

    @pl.when(steepest <= A_SAFE_LOG2_DECAY)
    def _():
        for hd, pt in zip(heads, parts):
            k_up = (pt["k"] * jnp.exp2(-pt["b_sub"])).astype(BF16)
            a_scr[hd] = _dot_nt(pt["q_sub"], k_up)

    @pl.when(steepest > A_SAFE_LOG2_DECAY)
    def _():
        for hd, pt in zip(heads, parts):
            a_scr[hd] = _hgrn_exact_sub_scores(pt["q"], pt["k"], pt["b_sub"], reduce_ref[...],
                                                k_scr.at[hd], bs_scr.at[hd], e_scr.at[hd], ts=ts)

    for hd, pt, ln in zip(heads, parts, lanes):
        o, st_ref[hd] = _hgrn_outputs(pt, v_ref[0, 0, :, ln], st_ref[hd], a_scr[hd], ts=ts)
        o = _rms(o, gain_ref[...]) * sg_ref[0, 0, :, ln].astype(F32)
        o_ref[0, 0, :, ln] = o.astype(BF16)


def _hgrn_decays(q, k, g, prefix, *, ts):
    C, c = A_CHUNK, A_SUB
    nc, dk = ts // C, q.shape[-1]

    g1 = g.astype(BF16)
    r1 = g - g1.astype(F32)
    g2 = r1.astype(BF16)
    g3 = (r1 - g2.astype(F32)).astype(BF16)
    big_b = _dot(prefix, jnp.concatenate([g1, g2, g3], axis=0))
    b_last = big_b[ts - 1:ts]
    after = b_last - big_b

    def since_start_of(size):
        groups = [big_b[:size]] + [big_b[lo:lo + size] - big_b[lo - 1:lo]
                                   for lo in range(size, ts, size)]
        return jnp.concatenate(groups, axis=0)

    b_chunk = since_start_of(C)
    b_sub = since_start_of(c)
    to_chunk_end = jnp.concatenate(
        [big_b[lo + C - 1:lo + C] - big_b[lo:lo + C] for lo in range(0, ts, C)],
        axis=0).reshape(nc, C, dk)

    return dict(
        q=q, k=k, b_last=b_last, after=after, b_sub=b_sub, to_chunk_end=to_chunk_end,
        q_blk=(q * jnp.exp2(big_b)).astype(BF16),
        k_blk=(k * jnp.exp2(after)).astype(BF16),
        q_chunk=(q * jnp.exp2(b_chunk)).astype(BF16),
        q_sub=(q * jnp.exp2(b_sub)).astype(BF16))


def _hgrn_outputs(pt, v, st, a_rep, *, ts):
    C, c = A_CHUNK, A_SUB
    nc, nsub = ts // C, C // c
    k, after, to_chunk_end = pt["k"], pt["after"], pt["to_chunk_end"]
    q_chunk, q_sub = pt["q_chunk"], pt["q_sub"]
    dk = k.shape[-1]

    zeros2 = lambda n: jnp.zeros((n, dk), BF16)
    qs_parts, ks_parts = [], []
    for n in range(1, nc):
        lo, hi = n * C, (n + 1) * C
        decay = jnp.exp2(after[:lo] - after[lo - 1:lo])
        ks_parts.append(jnp.concatenate([(k[:lo] * decay).astype(BF16), zeros2(ts - lo)], axis=0))
        pieces = [zeros2(lo), q_chunk[lo:hi]] + ([zeros2(ts - hi)] if hi < ts else [])
        qs_parts.append(jnp.concatenate(pieces, axis=0))
    a_chunks = _dot_nt(jnp.concatenate(qs_parts, axis=1), jnp.concatenate(ks_parts, axis=1))

    k3 = k.reshape(nc, C, dk)
    qs3 = q_sub.reshape(nc, C, dk)
    zeros3 = lambda n: jnp.zeros((nc, n, dk), BF16)
    qs_parts, ks_parts = [], []
    for i in range(1, nsub):
        lo, hi = i * c, (i + 1) * c
        decay = jnp.exp2(to_chunk_end[:, :lo] - to_chunk_end[:, lo - 1:lo])
        ks_parts.append(jnp.concatenate([(k3[:, :lo] * decay).astype(BF16), zeros3(C - lo)], axis=1))
        pieces = [zeros3(lo), qs3[:, lo:hi]] + ([zeros3(C - hi)] if hi < C else [])
        qs_parts.append(jnp.concatenate(pieces, axis=1))
    a_subs = _dot_nt(jnp.concatenate(qs_parts, axis=2).reshape(ts, (nsub - 1) * dk),
                     jnp.concatenate(ks_parts, axis=2).reshape(ts, (nsub - 1) * dk))

    rr = lax.broadcasted_iota(jnp.int32, (ts, ts), 0)
    ll = lax.broadcasted_iota(jnp.int32, (ts, ts), 1)
    lc, ls = C.bit_length() - 1, c.bit_length() - 1
    same_chunk = (rr >> lc) == (ll >> lc)
    own_sub = ((rr >> ls) == (ll >> ls)) & ((ll & (c - 1)) <= (rr & (c - 1)))
    a = jnp.where(own_sub, a_rep, 0.0) + jnp.where(same_chunk, a_subs, 0.0) + a_chunks

    o = _dot(a.astype(BF16), v) + _dot_nt(pt["q_blk"], st.astype(BF16))
    return o, st * jnp.exp2(pt["b_last"]) + _dot_tn(v, pt["k_blk"])


def _hgrn_exact_sub_scores(q, k, b_sub, reduce_j, k_scr, bs_scr, e_scr, *, ts):
    c, dk = A_SUB, q.shape[-1]
    k_scr[...] = k
    bs_scr[...] = b_sub
    zero_tile = jnp.zeros((F32_SUBLANES, dk), F32)
    for n in range(ts // c):
        tiles = [slice(n * c + lo, n * c + lo + F32_SUBLANES) for lo in range(0, c, F32_SUBLANES)]
        for j in range(c):
            row = slice(n * c + j, n * c + j + 1)
            k_j, b_j = k_scr[row, :], bs_scr[row, :]
            e = [q[t] * k_j * jnp.exp2(jnp.minimum(b_sub[t] - b_j, 0.0))
                 if t.stop > n * c + j else zero_tile for t in tiles]
            e_scr[n * c:(n + 1) * c, j * dk:(j + 1) * dk] = jnp.concatenate(e, axis=0).astype(BF16)
    return _dot(e_scr[...], reduce_j)


def _hgrn_rec_call(q, k, v, lf, sg, gain, *, ts=256):
    bsz, groups, seq, group_width = q.shape
    dk = gain.shape[-1]
    heads_per_step = group_width // dk
    blk = pl.BlockSpec((1, 1, ts, group_width), lambda b, h, s: (b, h, s, 0))
    prefix, reduce_j = _hgrn_rec_constants(ts, dk)
    kern = functools.partial(_hgrn_rec_kernel, ts=ts)
    return pl.pallas_call(
        kern, grid=(bsz, groups, seq // ts),
        in_specs=[blk, blk, blk, blk, blk, _const_spec(gain.shape), _const_spec(prefix.shape),
                  _const_spec(reduce_j.shape)],
        out_specs=blk, out_shape=jax.ShapeDtypeStruct(q.shape, BF16),
        scratch_shapes=[pltpu.VMEM((heads_per_step, dk, dk), F32),
                        pltpu.VMEM((heads_per_step, ts, dk), F32),
                        pltpu.VMEM((heads_per_step, ts, dk), F32),
                        pltpu.VMEM((heads_per_step, ts, A_SUB * dk), BF16),
                        pltpu.VMEM((heads_per_step, ts, ts), F32)],
        compiler_params=_params(3), name="hgrn_rec")(q, k, v, lf, sg, gain, prefix, reduce_j)


def _rope_tables_lanes(pos_row, inv_freq_col, half):
    ang = inv_freq_col * pos_row.astype(F32)
    cos, sin = jnp.cos(ang), jnp.sin(ang)
    zeros = jnp.zeros((128 - 2 * half, ang.shape[1]), F32)
    c_tab = jnp.transpose(jnp.concatenate([cos, cos, zeros], axis=0))
    s_tab = jnp.transpose(jnp.concatenate([-sin, sin, zeros], axis=0))
    return c_tab, s_tab


def _kv_kernel(x_ref, pos_ref, g_ref, wdc_ref, wdr_ref, gl_ref, wk_ref, wvt_ref, freq_ref,
               k_out, vt_out, *, heads, nope, vdim, half):
    h = _rms(x_ref[0], g_ref[...]).astype(BF16)
    c_kv = _rms(_dot(h, wdc_ref[...]), gl_ref[...]).astype(BF16)
    kr = _dot(h, wdr_ref[...])
    c_tab, s_tab = _rope_tables_lanes(pos_ref[0], freq_ref[...], half)
    k_rope = (kr * c_tab + pltpu.roll(kr, half, 1) * s_tab).astype(BF16)
    k_nope = _dot(c_kv, wk_ref[...])
    v_t = _dot_nt(wvt_ref[...], c_kv)
    ones = jnp.ones((vt_out.shape[2] - vdim, v_t.shape[1]), BF16)
    for hd in range(heads):
        k_out[0, hd, :, 0:nope] = k_nope[:, hd * nope:(hd + 1) * nope].astype(BF16)
        k_out[0, hd, :, nope:] = k_rope
        vt_out[0, hd, 0:vdim, :] = v_t[hd * vdim:(hd + 1) * vdim].astype(BF16)
        vt_out[0, hd, vdim:, :] = ones


def _kv_call(x, pos_row, g, wd_c, wd_r, gl, wk, wvt, freq_col, *, heads, nope, vdim, half, tm=512):
    bsz, seq, d = x.shape
    kern = functools.partial(_kv_kernel, heads=heads, nope=nope, vdim=vdim, half=half)
    kd = nope + 128
    vrows = vdim + BF16_SUBLANES
    return pl.pallas_call(
        kern, grid=(bsz, seq // tm),
        in_specs=[pl.BlockSpec((1, tm, d), lambda b, s: (b, s, 0)),
                  pl.BlockSpec((1, 1, tm), lambda b, s: (b, 0, s)),
                  _const_spec(g.shape), _const_spec(wd_c.shape), _const_spec(wd_r.shape),
                  _const_spec(gl.shape), _const_spec(wk.shape), _const_spec(wvt.shape),
                  _const_spec(freq_col.shape)],
        out_specs=[pl.BlockSpec((1, heads, tm, kd), lambda b, s: (b, 0, s, 0)),
                   pl.BlockSpec((1, heads, vrows, tm), lambda b, s: (b, 0, 0, s))],
        out_shape=[jax.ShapeDtypeStruct((bsz, heads, seq, kd), BF16),
                   jax.ShapeDtypeStruct((bsz, heads, vrows, seq), BF16)],
        compiler_params=_params(2), name="mla_shared_kv")(
            x, pos_row, g, wd_c, wd_r, gl, wk, wvt, freq_col)


def _q_kernel(x_ref, pos_ref, g_ref, wdq_ref, gq_ref, wuqt_ref, freq_ref, qt_out,
              *, heads, nope, half, qd, scale):
    h = _rms(x_ref[0], g_ref[...]).astype(BF16)
    c_q = (_rms(_dot(h, wdq_ref[...]), gq_ref[...]) * scale).astype(BF16)
    q_t = _dot_nt(wuqt_ref[...], c_q)
    ang = freq_ref[...] * pos_ref[0].astype(F32)
    cos, sin = jnp.cos(ang), jnp.sin(ang)
    zeros = jnp.zeros((qd - nope - 2 * half, q_t.shape[1]), F32)
    for hd in range(heads):
        base = hd * qd
        x1 = q_t[base + nope:base + nope + half]
        x2 = q_t[base + nope + half:base + nope + 2 * half]
        full = jnp.concatenate(
            [q_t[base:base + nope], x1 * cos - x2 * sin, x2 * cos + x1 * sin, zeros], axis=0)
        qt_out[0, hd] = full.astype(BF16)


def _q_call(x, pos_row, g, w_dq, gq, w_uqt, freq_col, *, heads, nope, half, qd, scale, tm=512):
    bsz, seq, d = x.shape
    kern = functools.partial(_q_kernel, heads=heads, nope=nope, half=half, qd=qd, scale=scale)
    return pl.pallas_call(
        kern, grid=(bsz, seq // tm),
        in_specs=[pl.BlockSpec((1, tm, d), lambda b, s: (b, s, 0)),
                  pl.BlockSpec((1, 1, tm), lambda b, s: (b, 0, s)),
                  _const_spec(g.shape), _const_spec(w_dq.shape), _const_spec(gq.shape),
                  _const_spec(w_uqt.shape), _const_spec(freq_col.shape)],
        out_specs=pl.BlockSpec((1, heads, qd, tm), lambda b, s: (b, 0, 0, s)),
        out_shape=jax.ShapeDtypeStruct((bsz, heads, qd, seq), BF16),
        compiler_params=_params(2), name="mla_q")(x, pos_row, g, w_dq, gq, w_uqt, freq_col)


def _attn_kernel(qt_ref, k_ref, vt_ref, o_ref, s0, s1, p0, p1, acc_ref, *, tq, tk, dv):
    qi = pl.program_id(2)

    def key_block(j):
        return k_ref[0, 0, pl.ds(pl.multiple_of(j * tk, tk), tk), :]

    def value_block(j):
        return vt_ref[0, 0, :, pl.ds(pl.multiple_of(j * tk, tk), tk)]

    def scores(j, s_out):
        s = _dot(key_block(j), qt_ref[0, 0])
        s_out[...] = s
        return jnp.max(s, axis=0, keepdims=True)

    def probabilities(s, m, m_blk):
        m_new = jnp.maximum(m, m_blk)
        return m_new, jnp.exp2(m - m_new), jnp.exp2((s - m_new).astype(BF16))

    def softmax_step(s_in, p_out, m, m_blk):
        m_new, alpha, p = probabilities(s_in[...], m, m_blk)
        p_out[...] = p
        return m_new, alpha

    def accumulate(j, p_in, alpha):
        acc_ref[...] = alpha * acc_ref[...] + _dot(value_block(j), p_in[...])

    def body(j, carry):
        a_prev, m, mb0 = carry
        mb1 = scores(2 * j + 1, s1)
        m, a0 = softmax_step(s0, p0, m, mb0)
        accumulate(jnp.maximum(2 * j - 1, 0), p1, a_prev)
        mb0 = scores(2 * j + 2, s0)
        m, a1 = softmax_step(s1, p1, m, mb1)
        accumulate(2 * j, p0, a0)
        return a1, m, mb0

    n_diag = tq // tk
    n_full = n_diag * qi
    acc_ref[...] = jnp.zeros_like(acc_ref)
    p1[...] = jnp.zeros_like(p1)
    init = (jnp.ones((1, tq), F32), jnp.full((1, tq), NEG_BIG, F32), scores(0, s0))
    a_prev, m, _ = lax.fori_loop(0, (n_diag // 2) * qi, body, init)
    accumulate(jnp.maximum(n_full - 1, 0), p1, a_prev)

    causal = (lax.broadcasted_iota(jnp.int32, (tk, tk), 0)
              <= lax.broadcasted_iota(jnp.int32, (tk, tk), 1))
    for d in range(n_diag):
        lo = d * tk
        s = s0[...] if d == 0 else _dot(key_block(n_full + d), qt_ref[0, 0, :, lo:])
        masked = jnp.where(causal, s[:, :tk], NEG_BIG)
        s = jnp.concatenate([masked, s[:, tk:]], axis=1) if lo + tk < tq else masked
        m_d, a_d, p_d = probabilities(s, m[:, lo:], jnp.max(s, axis=0, keepdims=True))
        m = jnp.concatenate([m[:, :lo], m_d], axis=1) if lo else m_d
        acc_ref[:, lo:] = a_d * acc_ref[:, lo:] + _dot(value_block(n_full + d), p_d)
    acc = acc_ref[...]
    o_ref[0] = jnp.transpose(acc[:dv] / acc[dv:dv + 1]).astype(BF16)


def _attn_call(qt, k, vt, *, tq=2048, tk=512):
    assert (tq // tk) % 2 == 0
    bsz, heads, qd, seq = qt.shape
    vrows = vt.shape[2]
    dv = vrows - BF16_SUBLANES
    kern = functools.partial(_attn_kernel, tq=tq, tk=tk, dv=dv)
    return pl.pallas_call(
        kern, grid=(bsz, heads, seq // tq),
        in_specs=[pl.BlockSpec((1, 1, qd, tq), lambda b, h, i: (b, h, 0, i)),
                  pl.BlockSpec((1, 1, seq, k.shape[3]), lambda b, h, i: (b, h, 0, 0)),
                  pl.BlockSpec((1, 1, vrows, seq), lambda b, h, i: (b, h, 0, 0))],
        out_specs=pl.BlockSpec((1, tq, dv), lambda b, h, i: (b, i, h)),
        out_shape=jax.ShapeDtypeStruct((bsz, seq, heads * dv), BF16),
        scratch_shapes=[pltpu.VMEM((tk, tq), F32), pltpu.VMEM((tk, tq), F32),
                        pltpu.VMEM((tk, tq), BF16), pltpu.VMEM((tk, tq), BF16),
                        pltpu.VMEM((vrows, tq), F32)],
        compiler_params=_params(3), name="mla_attention")(qt, k, vt)


def kernel(x, p, positions, norm_gains, ffn_w_in, ffn_w_out, ple_w_gate, ple_w_in, a_w_in,
           a_lb_logits, a_out_gain, a_w_out, kv_norm_in, kv_w_down, kv_latent_norm, kv_w_up,
           b_w_dq, b_q_norm, b_w_uq, b_w_out, final_norm):
    bsz, seq, d = x.shape
    depth = norm_gains.shape[0]
    n_a = a_w_in.shape[0]
    t = bsz * seq
    bf = lambda w: w.astype(BF16)
    row = lambda g: g.reshape(1, -1).astype(F32)

    a_key = a_out_gain.shape[1]
    kv_lora = kv_latent_norm.shape[0]
    rope = kv_w_down.shape[1] - kv_lora
    half = rope // 2
    vdim = 128
    heads = b_w_out.shape[1] // vdim
    nope = kv_w_up.shape[1] // heads - vdim
    qd = 256
    assert nope == 128 and rope == 64 and b_w_uq.shape[2] == heads * (nope + rope)

    inv_freq = 1.0 / (ROPE_THETA ** (jnp.arange(0, rope, 2, dtype=F32) / rope))
    freq_col = inv_freq.reshape(half, 1)
    pos_row = positions.reshape(bsz, 1, seq)

    wd_c = bf(kv_w_down[:, :kv_lora])
    wd_r = bf(jnp.concatenate([kv_w_down[:, kv_lora:]] * (128 // rope), axis=1))
    w_up = kv_w_up.reshape(kv_lora, heads, nope + vdim)
    wk = bf(w_up[:, :, :nope].reshape(kv_lora, heads * nope))
    wvt = bf(w_up[:, :, nope:].reshape(kv_lora, heads * vdim).T)

    def shared_kv(xs):
        return _kv_call(xs.reshape(bsz, seq, d), pos_row, row(kv_norm_in), wd_c, wd_r,
                        row(kv_latent_norm), wk, wvt, freq_col,
                        heads=heads, nope=nope, vdim=vdim, half=half)

    ffn_in, ffn_out = bf(ffn_w_in), bf(ffn_w_out)
    ple_gate, ple_in = bf(ple_w_gate), bf(ple_w_in)
    gains = norm_gains.astype(F32).reshape(depth, norm_gains.shape[1], 1, d)
    p_tokens = p.reshape(depth, t, -1)

    xf = x.reshape(t, d)
    shared = shared_kv(xf) if n_a == 0 else None
    attn_scale = float(nope + rope) ** -0.5 * LOG2_E
    for li in range(depth):
        x1 = _block_call(xf, (gains, (li, 0)), (ffn_in, (li, 0)), (ffn_out, (li, 0)))
        if li < n_a:
            q, k, v, lf, sg = _hgrn_proj_call(
                x1, row(norm_gains[li, 1]), bf(a_w_in[li]), a_lb_logits.astype(F32), bsz=bsz,
                layer=li, key_dim=a_key, group_width=A_HEADS_PER_STEP * a_key)
            o = _hgrn_rec_call(q, k, v, lf, sg, row(a_out_gain[li]))
            pre = (o, bf(a_w_out[li]))
        else:
            bi = li - n_a
            w_uq = b_w_uq[bi].reshape(-1, heads, nope + rope)
            w_uq = jnp.pad(w_uq, ((0, 0), (0, 0), (0, qd - nope - rope)))
            w_uqt = bf(w_uq.reshape(-1, heads * qd).T)
            qt = _q_call(x1.reshape(bsz, seq, d), pos_row, row(norm_gains[li, 1]),
                         bf(b_w_dq[bi]), row(b_q_norm[bi]), w_uqt, freq_col, heads=heads,
                         nope=nope, half=half, qd=qd, scale=attn_scale)
            k_all, vt_all = shared
            o = _attn_call(qt, k_all, vt_all)
            pre = (o.reshape(t, -1), bf(b_w_out[bi]))
        ple = ((gains, (li, 3)), (ple_gate, (li,)), (p_tokens, li), (ple_in, (li,)))
        final = row(final_norm) if li == depth - 1 else None
        xf = _block_call(x1, (gains, (li, 2)), (ffn_in, (li, 1)), (ffn_out, (li, 1)),
                         pre=pre, ple=ple, final=final)
        if li == n_a - 1:
            shared = shared_kv(xf)
    return xf.reshape(bsz, seq, d)
```

```python
import functools
import math

import jax
import jax.numpy as jnp
from jax import lax
from jax.experimental import pallas as pl
from jax.experimental.pallas import tpu as pltpu

F32 = jnp.float32
BF16 = jnp.bfloat16

EPS = 1e-6
ROPE_THETA = 10000.0
A_CHUNK = 64
A_SUB = 16
A_HEADS_PER_STEP = 8
A_SAFE_LOG2_DECAY = 96.0
NEG_BIG = -1e30
LOG2_E = math.log2(math.e)

V7X_VMEM_BYTES = 64 * 1024 * 1024
VMEM_LIMIT = V7X_VMEM_BYTES * 7 // 8
F32_SUBLANES = 8
BF16_SUBLANES = 16


def _rms(x, g):
    ms = jnp.mean(x * x, axis=-1, keepdims=True)
    return x * lax.rsqrt(ms + EPS) * g


def _sigmoid(x):
    return 1.0 / (1.0 + jnp.exp(-x))


def _dot(a, b):
    return jnp.dot(a, b, preferred_element_type=F32)


def _dot_nt(a, b):
    return lax.dot_general(a, b, (((1,), (1,)), ((), ())), preferred_element_type=F32)


def _dot_tn(a, b):
    return lax.dot_general(a, b, (((0,), (0,)), ((), ())), preferred_element_type=F32)


def _const_spec(shape):
    nd = len(shape)
    return pl.BlockSpec(shape, lambda *_: (0,) * nd, pipeline_mode=pl.Buffered(1))


def _pick(w):
    arr, lead = w if isinstance(w, tuple) else (w, ())
    rest = arr.shape[len(lead):]
    index = tuple(lead) + (0,) * len(rest)
    spec = pl.BlockSpec((None,) * len(lead) + rest, lambda *_: index,
                        pipeline_mode=pl.Buffered(1))
    return arr, spec


def _params(n_axes):
    return pltpu.CompilerParams(
        dimension_semantics=("arbitrary",) * n_axes, vmem_limit_bytes=VMEM_LIMIT)


def _block_kernel(*refs, d_ff, has_pre, has_ple, has_final):
    it = iter(refs)
    x_ref = next(it)
    if has_pre:
        y_ref, wpre_ref = next(it), next(it)
    g_ref, win_ref, wout_ref = next(it), next(it), next(it)
    if has_ple:
        gp_ref, wg_ref, p_ref, wp_ref = next(it), next(it), next(it), next(it)
    if has_final:
        gf_ref = next(it)
    o_ref = next(it)

    x = x_ref[...]
    if has_pre:
        if len(y_ref.shape) == 4:
            y = jnp.concatenate([y_ref[0, gi] for gi in range(y_ref.shape[1])], axis=1)
        else:
            y = y_ref[...]
        x = x + _dot(y, wpre_ref[...])
    h = _rms(x, g_ref[...]).astype(BF16)
    gate = _dot(h, win_ref[:, :d_ff])
    up = _dot(h, win_ref[:, d_ff:])
    act = (gate * _sigmoid(gate) * up).astype(BF16)
    x = x + 0.5 * _dot(act, wout_ref[...])
    if has_ple:
        hg = _rms(x, gp_ref[...]).astype(BF16)
        emb_gate = _sigmoid(_dot(hg, wg_ref[...]))
        emb = _dot(p_ref[...].astype(BF16), wp_ref[...])
        x = x + emb_gate * emb
    if has_final:
        x = _rms(x, gf_ref[...])
    o_ref[...] = x


def _block_call(x, g, w_in, w_out, *, pre=None, ple=None, final=None, tm=512):
    t, d = x.shape
    row = lambda i: (i, 0)
    args, specs = [x], [pl.BlockSpec((tm, d), row)]

    def add(w):
        arr, spec = _pick(w)
        args.append(arr)
        specs.append(spec)
        return spec.block_shape

    if pre is not None:
        y, w_pre = pre
        args.append(y)
        if y.ndim == 4:
            steps = y.shape[2] // tm
            specs.append(pl.BlockSpec((1, y.shape[1], tm, y.shape[3]),
                                      lambda i: (i // steps, 0, i % steps, 0)))
        else:
            specs.append(pl.BlockSpec((tm, y.shape[1]), row))
        add(w_pre)
    add(g)
    add(w_in)
    d_ff = add(w_out)[-2]
    if ple is not None:
        gp, wg, p, wp = ple
        add(gp)
        add(wg)
        args.append(p[0])
        specs.append(pl.BlockSpec((None, tm, p[0].shape[2]), lambda i: (p[1], i, 0)))
        add(wp)
    if final is not None:
        add(final)
    kern = functools.partial(_block_kernel, d_ff=d_ff, has_pre=pre is not None,
                             has_ple=ple is not None, has_final=final is not None)
    return pl.pallas_call(
        kern, grid=(t // tm,), in_specs=specs, out_specs=pl.BlockSpec((tm, d), row),
        out_shape=jax.ShapeDtypeStruct((t, d), F32), compiler_params=_params(1),
        name="token_block")(*args)


def _hgrn_proj_kernel(x_ref, g_ref, w_ref, lbl_ref, q_ref, k_ref, v_ref, lf_ref, sg_ref,
                      *, width, layer, q_scale):
    h = _rms(x_ref[...], g_ref[...]).astype(BF16)
    q = _dot(h, w_ref[:, 0 * width:1 * width])
    f = _dot(h, w_ref[:, 1 * width:2 * width])
    v = _dot(h, w_ref[:, 2 * width:3 * width])
    og = _dot(h, w_ref[:, 3 * width:4 * width])
    lbl = lbl_ref[...]
    e = jnp.exp(lbl - jnp.max(lbl, axis=0, keepdims=True))
    sm = e / jnp.sum(e, axis=0, keepdims=True)
    lb = jnp.sum(sm[:layer + 1], axis=0, keepdims=True)
    sf = _sigmoid(f)
    outs = ((q_ref, q * q_scale), (k_ref, (1.0 - lb) * _sigmoid(-f)), (v_ref, v),
            (lf_ref, jnp.log(lb + (1.0 - lb) * sf)), (sg_ref, _sigmoid(og)))
    gw = q_ref.shape[3]
    for ref, val in outs:
        for gi in range(ref.shape[1]):
            ref[0, gi] = val[:, gi * gw:(gi + 1) * gw].astype(ref.dtype)


def _hgrn_proj_call(x, g, w_in, lb_logits, *, bsz, layer, key_dim, group_width, tm=512):
    t, d = x.shape
    seq = t // bsz
    steps = seq // tm
    width = w_in.shape[1] // 4
    groups = width // group_width
    kern = functools.partial(_hgrn_proj_kernel, width=width, layer=layer,
                             q_scale=float(key_dim) ** -0.5)
    out_spec = pl.BlockSpec((1, groups, tm, group_width), lambda i: (i // steps, 0, i % steps, 0))
    shape = (bsz, groups, seq, group_width)
    bf = jax.ShapeDtypeStruct(shape, BF16)
    return pl.pallas_call(
        kern, grid=(t // tm,),
        in_specs=[pl.BlockSpec((tm, d), lambda i: (i, 0)), _const_spec(g.shape),
                  _const_spec(w_in.shape), _const_spec(lb_logits.shape)],
        out_specs=[out_spec] * 5,
        out_shape=[bf, bf, bf, jax.ShapeDtypeStruct(shape, F32), bf],
        compiler_params=_params(1), name="hgrn_proj")(x, g, w_in, lb_logits)


def _hgrn_rec_constants(ts, dk):
    incl = jnp.arange(ts)[None, :] <= jnp.arange(ts)[:, None]
    prefix = jnp.concatenate([incl, incl, incl], axis=1).astype(BF16)
    col = jnp.arange(ts)[None, :]
    reduce_j = (jnp.arange(A_SUB * dk)[:, None] // dk) == (col % A_SUB)
    return prefix, reduce_j.astype(BF16)


def _hgrn_rec_kernel(q_ref, k_ref, v_ref, lf_ref, sg_ref, gain_ref, prefix_ref, reduce_ref,
                     o_ref, st_ref, k_scr, bs_scr, e_scr, a_scr, *, ts):
    @pl.when(pl.program_id(2) == 0)
    def _():
        st_ref[...] = jnp.zeros_like(st_ref)

    dk = gain_ref.shape[-1]
    heads = range(st_ref.shape[0])
    lanes = [slice(hd * dk, (hd + 1) * dk) for hd in heads]
    parts = [_hgrn_decays(q_ref[0, 0, :, ln].astype(F32), k_ref[0, 0, :, ln].astype(F32),
                          lf_ref[0, 0, :, ln] * LOG2_E, prefix_ref[...], ts=ts) for ln in lanes]
    steepest = functools.reduce(jnp.maximum, [jnp.max(-pt["b_sub"]) for pt in parts])

    @pl.when(steepest <= A_SAFE_LOG2_DECAY)
    def _():
        for hd, pt in zip(heads, parts):
            k_up = (pt["k"] * jnp.exp2(-pt["b_sub"])).astype(BF16)
            a_scr[hd] = _dot_nt(pt["q_sub"], k_up)

    @pl.when(steepest > A_SAFE_LOG2_DECAY)
    def _():
        for hd, pt in zip(heads, parts):
            a_scr[hd] = _hgrn_exact_sub_scores(pt["q"], pt["k"], pt["b_sub"], reduce_ref[...],
                                                k_scr.at[hd], bs_scr.at[hd], e_scr.at[hd], ts=ts)

    for hd, pt, ln in zip(heads, parts, lanes):
        o, st_ref[hd] = _hgrn_outputs(pt, v_ref[0, 0, :, ln], st_ref[hd], a_scr[hd], ts=ts)
        o = _rms(o, gain_ref[...]) * sg_ref[0, 0, :, ln].astype(F32)
        o_ref[0, 0, :, ln] = o.astype(BF16)


def _hgrn_decays(q, k, g, prefix, *, ts):
    C, c = A_CHUNK, A_SUB
    nc, dk = ts // C, q.shape[-1]

    g1 = g.astype(BF16)
    r1 = g - g1.astype(F32)
    g2 = r1.astype(BF16)
    g3 = (r1 - g2.astype(F32)).astype(BF16)
    big_b = _dot(prefix, jnp.concatenate([g1, g2, g3], axis=0))
    b_last = big_b[ts - 1:ts]
    after = b_last - big_b

    def since_start_of(size):
        groups = [big_b[:size]] + [big_b[lo:lo + size] - big_b[lo - 1:lo]
                                   for lo in range(size, ts, size)]
        return jnp.concatenate(groups, axis=0)

    b_chunk = since_start_of(C)
    b_sub = since_start_of(c)
    to_chunk_end = jnp.concatenate(
        [big_b[lo + C - 1:lo + C] - big_b[lo:lo + C] for lo in range(0, ts, C)],
        axis=0).reshape(nc, C, dk)

    return dict(
        q=q, k=k, b_last=b_last, after=after, b_sub=b_sub, to_chunk_end=to_chunk_end,
        q_blk=(q * jnp.exp2(big_b)).astype(BF16),
        k_blk=(k * jnp.exp2(after)).astype(BF16),
        q_chunk=(q * jnp.exp2(b_chunk)).astype(BF16),
        q_sub=(q * jnp.exp2(b_sub)).astype(BF16))


def _hgrn_outputs(pt, v, st, a_rep, *, ts):
    C, c = A_CHUNK, A_SUB
    nc, nsub = ts // C, C // c
    k, after, to_chunk_end = pt["k"], pt["after"], pt["to_chunk_end"]
    q_chunk, q_sub = pt["q_chunk"], pt["q_sub"]
    dk = k.shape[-1]

    zeros2 = lambda n: jnp.zeros((n, dk), BF16)
    qs_parts, ks_parts = [], []
    for n in range(1, nc):
        lo, hi = n * C, (n + 1) * C
        decay = jnp.exp2(after[:lo] - after[lo - 1:lo])
        ks_parts.append(jnp.concatenate([(k[:lo] * decay).astype(BF16), zeros2(ts - lo)], axis=0))
        pieces = [zeros2(lo), q_chunk[lo:hi]] + ([zeros2(ts - hi)] if hi < ts else [])
        qs_parts.append(jnp.concatenate(pieces, axis=0))
    a_chunks = _dot_nt(jnp.concatenate(qs_parts, axis=1), jnp.concatenate(ks_parts, axis=1))

    k3 = k.reshape(nc, C, dk)
    qs3 = q_sub.reshape(nc, C, dk)
    zeros3 = lambda n: jnp.zeros((nc, n, dk), BF16)
    qs_parts, ks_parts = [], []
    for i in range(1, nsub):
        lo, hi = i * c, (i + 1) * c
        decay = jnp.exp2(to_chunk_end[:, :lo] - to_chunk_end[:, lo - 1:lo])
        ks_parts.append(jnp.concatenate([(k3[:, :lo] * decay).astype(BF16), zeros3(C - lo)], axis=1))
        pieces = [zeros3(lo), qs3[:, lo:hi]] + ([zeros3(C - hi)] if hi < C else [])
        qs_parts.append(jnp.concatenate(pieces, axis=1))
    a_subs = _dot_nt(jnp.concatenate(qs_parts, axis=2).reshape(ts, (nsub - 1) * dk),
                     jnp.concatenate(ks_parts, axis=2).reshape(ts, (nsub - 1) * dk))

    rr = lax.broadcasted_iota(jnp.int32, (ts, ts), 0)
    ll = lax.broadcasted_iota(jnp.int32, (ts, ts), 1)
    lc, ls = C.bit_length() - 1, c.bit_length() - 1
    same_chunk = (rr >> lc) == (ll >> lc)
    own_sub = ((rr >> ls) == (ll >> ls)) & ((ll & (c - 1)) <= (rr & (c - 1)))
    a = jnp.where(own_sub, a_rep, 0.0) + jnp.where(same_chunk, a_subs, 0.0) + a_chunks

    o = _dot(a.astype(BF16), v) + _dot_nt(pt["q_blk"], st.astype(BF16))
    return o, st * jnp.exp2(pt["b_last"]) + _dot_tn(v, pt["k_blk"])


def _hgrn_exact_sub_scores(q, k, b_sub, reduce_j, k_scr, bs_scr, e_scr, *, ts):
    c, dk = A_SUB, q.shape[-1]
    k_scr[...] = k
    bs_scr[...] = b_sub
    zero_tile = jnp.zeros((F32_SUBLANES, dk), F32)
    for n in range(ts // c):
        tiles = [slice(n * c + lo, n * c + lo + F32_SUBLANES) for lo in range(0, c, F32_SUBLANES)]
        for j in range(c):
            row = slice(n * c + j, n * c + j + 1)
            k_j, b_j = k_scr[row, :], bs_scr[row, :]
            e = [q[t] * k_j * jnp.exp2(jnp.minimum(b_sub[t] - b_j, 0.0))
                 if t.stop > n * c + j else zero_tile for t in tiles]
            e_scr[n * c:(n + 1) * c, j * dk:(j + 1) * dk] = jnp.concatenate(e, axis=0).astype(BF16)
    return _dot(e_scr[...], reduce_j)


def _hgrn_rec_call(q, k, v, lf, sg, gain, *, ts=256):
    bsz, groups, seq, group_width = q.shape
    dk = gain.shape[-1]
    heads_per_step = group_width // dk
    blk = pl.BlockSpec((1, 1, ts, group_width), lambda b, h, s: (b, h, s, 0))
    prefix, reduce_j = _hgrn_rec_constants(ts, dk)
    kern = functools.partial(_hgrn_rec_kernel, ts=ts)
    return pl.pallas_call(
        kern, grid=(bsz, groups, seq // ts),
        in_specs=[blk, blk, blk, blk, blk, _const_spec(gain.shape), _const_spec(prefix.shape),
                  _const_spec(reduce_j.shape)],
        out_specs=blk, out_shape=jax.ShapeDtypeStruct(q.shape, BF16),
        scratch_shapes=[pltpu.VMEM((heads_per_step, dk, dk), F32),
                        pltpu.VMEM((heads_per_step, ts, dk), F32),
                        pltpu.VMEM((heads_per_step, ts, dk), F32),
                        pltpu.VMEM((heads_per_step, ts, A_SUB * dk), BF16),
                        pltpu.VMEM((heads_per_step, ts, ts), F32)],
        compiler_params=_params(3), name="hgrn_rec")(q, k, v, lf, sg, gain, prefix, reduce_j)


def _rope_tables_lanes(pos_row, inv_freq_col, half):
    ang = inv_freq_col * pos_row.astype(F32)
    cos, sin = jnp.cos(ang), jnp.sin(ang)
    zeros = jnp.zeros((128 - 2 * half, ang.shape[1]), F32)
    c_tab = jnp.transpose(jnp.concatenate([cos, cos, zeros], axis=0))
    s_tab = jnp.transpose(jnp.concatenate([-sin, sin, zeros], axis=0))
    return c_tab, s_tab


def _kv_kernel(x_ref, pos_ref, g_ref, wdc_ref, wdr_ref, gl_ref, wk_ref, wvt_ref, freq_ref,
               k_out, vt_out, *, heads, nope, vdim, half):
    h = _rms(x_ref[0], g_ref[...]).astype(BF16)
    c_kv = _rms(_dot(h, wdc_ref[...]), gl_ref[...]).astype(BF16)
    kr = _dot(h, wdr_ref[...])
    c_tab, s_tab = _rope_tables_lanes(pos_ref[0], freq_ref[...], half)
    k_rope = (kr * c_tab + pltpu.roll(kr, half, 1) * s_tab).astype(BF16)
    k_nope = _dot(c_kv, wk_ref[...])
    v_t = _dot_nt(wvt_ref[...], c_kv)
    ones = jnp.ones((vt_out.shape[2] - vdim, v_t.shape[1]), BF16)
    for hd in range(heads):
        k_out[0, hd, :, 0:nope] = k_nope[:, hd * nope:(hd + 1) * nope].astype(BF16)
        k_out[0, hd, :, nope:] = k_rope
        vt_out[0, hd, 0:vdim, :] = v_t[hd * vdim:(hd + 1) * vdim].astype(BF16)
        vt_out[0, hd, vdim:, :] = ones


def _kv_call(x, pos_row, g, wd_c, wd_r, gl, wk, wvt, freq_col, *, heads, nope, vdim, half, tm=512):
    bsz, seq, d = x.shape
    kern = functools.partial(_kv_kernel, heads=heads, nope=nope, vdim=vdim, half=half)
    kd = nope + 128
    vrows = vdim + BF16_SUBLANES
    return pl.pallas_call(
        kern, grid=(bsz, seq // tm),
        in_specs=[pl.BlockSpec((1, tm, d), lambda b, s: (b, s, 0)),
                  pl.BlockSpec((1, 1, tm), lambda b, s: (b, 0, s)),
                  _const_spec(g.shape), _const_spec(wd_c.shape), _const_spec(wd_r.shape),
                  _const_spec(gl.shape), _const_spec(wk.shape), _const_spec(wvt.shape),
                  _const_spec(freq_col.shape)],
        out_specs=[pl.BlockSpec((1, heads, tm, kd), lambda b, s: (b, 0, s, 0)),
                   pl.BlockSpec((1, heads, vrows, tm), lambda b, s: (b, 0, 0, s))],
        out_shape=[jax.ShapeDtypeStruct((bsz, heads, seq, kd), BF16),
                   jax.ShapeDtypeStruct((bsz, heads, vrows, seq), BF16)],
        compiler_params=_params(2), name="mla_shared_kv")(
            x, pos_row, g, wd_c, wd_r, gl, wk, wvt, freq_col)


def _q_kernel(x_ref, pos_ref, g_ref, wdq_ref, gq_ref, wuqt_ref, freq_ref, qt_out,
              *, heads, nope, half, qd, scale):
    h = _rms(x_ref[0], g_ref[...]).astype(BF16)
    c_q = (_rms(_dot(h, wdq_ref[...]), gq_ref[...]) * scale).astype(BF16)
    q_t = _dot_nt(wuqt_ref[...], c_q)
    ang = freq_ref[...] * pos_ref[0].astype(F32)
    cos, sin = jnp.cos(ang), jnp.sin(ang)
    zeros = jnp.zeros((qd - nope - 2 * half, q_t.shape[1]), F32)
    for hd in range(heads):
        base = hd * qd
        x1 = q_t[base + nope:base + nope + half]
        x2 = q_t[base + nope + half:base + nope + 2 * half]
        full = jnp.concatenate(
            [q_t[base:base + nope], x1 * cos - x2 * sin, x2 * cos + x1 * sin, zeros], axis=0)
        qt_out[0, hd] = full.astype(BF16)


def _q_call(x, pos_row, g, w_dq, gq, w_uqt, freq_col, *, heads, nope, half, qd, scale, tm=512):
    bsz, seq, d = x.shape
    kern = functools.partial(_q_kernel, heads=heads, nope=nope, half=half, qd=qd, scale=scale)
    return pl.pallas_call(
        kern, grid=(bsz, seq // tm),
        in_specs=[pl.BlockSpec((1, tm, d), lambda b, s: (b, s, 0)),
                  pl.BlockSpec((1, 1, tm), lambda b, s: (b, 0, s)),
                  _const_spec(g.shape), _const_spec(w_dq.shape), _const_spec(gq.shape),
                  _const_spec(w_uqt.shape), _const_spec(freq_col.shape)],
        out_specs=pl.BlockSpec((1, heads, qd, tm), lambda b, s: (b, 0, 0, s)),
        out_shape=jax.ShapeDtypeStruct((bsz, heads, qd, seq), BF16),
        compiler_params=_params(2), name="mla_q")(x, pos_row, g, w_dq, gq, w_uqt, freq_col)


def _attn_kernel(qt_ref, k_ref, vt_ref, o_ref, s0, s1, p0, p1, acc_ref, *, tq, tk, dv):
    qi = pl.program_id(2)

    def key_block(j):
        return k_ref[0, 0, pl.ds(pl.multiple_of(j * tk, tk), tk), :]

    def value_block(j):
        return vt_ref[0, 0, :, pl.ds(pl.multiple_of(j * tk, tk), tk)]

    def scores(j, s_out):
        s = _dot(key_block(j), qt_ref[0, 0])
        s_out[...] = s
        return jnp.max(s, axis=0, keepdims=True)

    def probabilities(s, m, m_blk):
        m_new = jnp.maximum(m, m_blk)
        return m_new, jnp.exp2(m - m_new), jnp.exp2((s - m_new).astype(BF16))

    def softmax_step(s_in, p_out, m, m_blk):
        m_new, alpha, p = probabilities(s_in[...], m, m_blk)
        p_out[...] = p
        return m_new, alpha

    def accumulate(j, p_in, alpha):
        acc_ref[...] = alpha * acc_ref[...] + _dot(value_block(j), p_in[...])

    def body(j, carry):
        a_prev, m, mb0 = carry
        mb1 = scores(2 * j + 1, s1)
        m, a0 = softmax_step(s0, p0, m, mb0)
        accumulate(jnp.maximum(2 * j - 1, 0), p1, a_prev)
        mb0 = scores(2 * j + 2, s0)
        m, a1 = softmax_step(s1, p1, m, mb1)
        accumulate(2 * j, p0, a0)
        return a1, m, mb0

    n_diag = tq // tk
    n_full = n_diag * qi
    acc_ref[...] = jnp.zeros_like(acc_ref)
    p1[...] = jnp.zeros_like(p1)
    init = (jnp.ones((1, tq), F32), jnp.full((1, tq), NEG_BIG, F32), scores(0, s0))
    a_prev, m, _ = lax.fori_loop(0, (n_diag // 2) * qi, body, init)
    accumulate(jnp.maximum(n_full - 1, 0), p1, a_prev)

    causal = (lax.broadcasted_iota(jnp.int32, (tk, tk), 0)
              <= lax.broadcasted_iota(jnp.int32, (tk, tk), 1))
    for d in range(n_diag):
        lo = d * tk
        s = s0[...] if d == 0 else _dot(key_block(n_full + d), qt_ref[0, 0, :, lo:])
        masked = jnp.where(causal, s[:, :tk], NEG_BIG)
        s = jnp.concatenate([masked, s[:, tk:]], axis=1) if lo + tk < tq else masked
        m_d, a_d, p_d = probabilities(s, m[:, lo:], jnp.max(s, axis=0, keepdims=True))
        m = jnp.concatenate([m[:, :lo], m_d], axis=1) if lo else m_d
        acc_ref[:, lo:] = a_d * acc_ref[:, lo:] + _dot(value_block(n_full + d), p_d)
    acc = acc_ref[...]
    o_ref[0] = jnp.transpose(acc[:dv] / acc[dv:dv + 1]).astype(BF16)


def _attn_call(qt, k, vt, *, tq=2048, tk=512):
    assert (tq // tk) % 2 == 0
    bsz, heads, qd, seq = qt.shape
    vrows = vt.shape[2]
    dv = vrows - BF16_SUBLANES
    kern = functools.partial(_attn_kernel, tq=tq, tk=tk, dv=dv)
    return pl.pallas_call(
        kern, grid=(bsz, heads, seq // tq),
        in_specs=[pl.BlockSpec((1, 1, qd, tq), lambda b, h, i: (b, h, 0, i)),
                  pl.BlockSpec((1, 1, seq, k.shape[3]), lambda b, h, i: (b, h, 0, 0)),
                  pl.BlockSpec((1, 1, vrows, seq), lambda b, h, i: (b, h, 0, 0))],
        out_specs=pl.BlockSpec((1, tq, dv), lambda b, h, i: (b, i, h)),
        out_shape=jax.ShapeDtypeStruct((bsz, seq, heads * dv), BF16),
        scratch_shapes=[pltpu.VMEM((tk, tq), F32), pltpu.VMEM((tk, tq), F32),
                        pltpu.VMEM((tk, tq), BF16), pltpu.VMEM((tk, tq), BF16),
                        pltpu.VMEM((vrows, tq), F32)],
        compiler_params=_params(3), name="mla_attention")(qt, k, vt)


def kernel(x, p, positions, norm_gains, ffn_w_in, ffn_w_out, ple_w_gate, ple_w_in, a_w_in,
           a_lb_logits, a_out_gain, a_w_out, kv_norm_in, kv_w_down, kv_latent_norm, kv_w_up,
           b_w_dq, b_q_norm, b_w_uq, b_w_out, final_norm):
    bsz, seq, d = x.shape
    depth = norm_gains.shape[0]
    n_a = a_w_in.shape[0]
    t = bsz * seq
    bf = lambda w: w.astype(BF16)
    row = lambda g: g.reshape(1, -1).astype(F32)

    a_key = a_out_gain.shape[1]
    kv_lora = kv_latent_norm.shape[0]
    rope = kv_w_down.shape[1] - kv_lora
    half = rope // 2
    vdim = 128
    heads = b_w_out.shape[1] // vdim
    nope = kv_w_up.shape[1] // heads - vdim
    qd = 256
    assert nope == 128 and rope == 64 and b_w_uq.shape[2] == heads * (nope + rope)

    inv_freq = 1.0 / (ROPE_THETA ** (jnp.arange(0, rope, 2, dtype=F32) / rope))
    freq_col = inv_freq.reshape(half, 1)
    pos_row = positions.reshape(bsz, 1, seq)

    wd_c = bf(kv_w_down[:, :kv_lora])
    wd_r = bf(jnp.concatenate([kv_w_down[:, kv_lora:]] * (128 // rope), axis=1))
    w_up = kv_w_up.reshape(kv_lora, heads, nope + vdim)
    wk = bf(w_up[:, :, :nope].reshape(kv_lora, heads * nope))
    wvt = bf(w_up[:, :, nope:].reshape(kv_lora, heads * vdim).T)

    def shared_kv(xs):
        return _kv_call(xs.reshape(bsz, seq, d), pos_row, row(kv_norm_in), wd_c, wd_r,
                        row(kv_latent_norm), wk, wvt, freq_col,
                        heads=heads, nope=nope, vdim=vdim, half=half)

    ffn_in, ffn_out = bf(ffn_w_in), bf(ffn_w_out)
    ple_gate, ple_in = bf(ple_w_gate), bf(ple_w_in)
    gains = norm_gains.astype(F32).reshape(depth, norm_gains.shape[1], 1, d)
    p_tokens = p.reshape(depth, t, -1)

    xf = x.reshape(t, d)
    shared = shared_kv(xf) if n_a == 0 else None
    attn_scale = float(nope + rope) ** -0.5 * LOG2_E
    for li in range(depth):
        x1 = _block_call(xf, (gains, (li, 0)), (ffn_in, (li, 0)), (ffn_out, (li, 0)))
        if li < n_a:
            q, k, v, lf, sg = _hgrn_proj_call(
                x1, row(norm_gains[li, 1]), bf(a_w_in[li]), a_lb_logits.astype(F32), bsz=bsz,
                layer=li, key_dim=a_key, group_width=A_HEADS_PER_STEP * a_key)
            o = _hgrn_rec_call(q, k, v, lf, sg, row(a_out_gain[li]))
            pre = (o, bf(a_w_out[li]))
        else:
            bi = li - n_a
            w_uq = b_w_uq[bi].reshape(-1, heads, nope + rope)
            w_uq = jnp.pad(w_uq, ((0, 0), (0, 0), (0, qd - nope - rope)))
            w_uqt = bf(w_uq.reshape(-1, heads * qd).T)
            qt = _q_call(x1.reshape(bsz, seq, d), pos_row, row(norm_gains[li, 1]),
                         bf(b_w_dq[bi]), row(b_q_norm[bi]), w_uqt, freq_col, heads=heads,
                         nope=nope, half=half, qd=qd, scale=attn_scale)
            k_all, vt_all = shared
            o = _attn_call(qt, k_all, vt_all)
            pre = (o.reshape(t, -1), bf(b_w_out[bi]))
        ple = ((gains, (li, 3)), (ple_gate, (li,)), (p_tokens, li), (ple_in, (li,)))
        final = row(final_norm) if li == depth - 1 else None
        xf = _block_call(x1, (gains, (li, 2)), (ffn_in, (li, 1)), (ffn_out, (li, 1)),
                         pre=pre, ple=ple, final=final)
        if li == n_a - 1:
            shared = shared_kv(xf)
    return xf.reshape(bsz, seq, d)
```

```python
import functools
import math

import jax
import jax.numpy as jnp
from jax import lax
from jax.experimental import pallas as pl
from jax.experimental.pallas import tpu as pltpu

F32 = jnp.float32
BF16 = jnp.bfloat16

EPS = 1e-6
ROPE_THETA = 10000.0
A_CHUNK = 64
A_SUB = 16
A_HEADS_PER_STEP = 8
A_SAFE_LOG2_DECAY = 96.0
NEG_BIG = -1e30
LOG2_E = math.log2(math.e)

V7X_VMEM_BYTES = 64 * 1024 * 1024
VMEM_LIMIT = V7X_VMEM_BYTES * 7 // 8
F32_SUBLANES = 8
BF16_SUBLANES = 16


def _rms(x, g):
    ms = jnp.mean(x * x, axis=-1, keepdims=True)
    return x * lax.rsqrt(ms + EPS) * g


def _sigmoid(x):
    return 1.0 / (1.0 + jnp.exp(-x))


def _dot(a, b):
    return jnp.dot(a, b, preferred_element_type=F32)


def _dot_nt(a, b):
    return lax.dot_general(a, b, (((1,), (1,)), ((), ())), preferred_element_type=F32)


def _dot_tn(a, b):
    return lax.dot_general(a, b, (((0,), (0,)), ((), ())), preferred_element_type=F32)


def _const_spec(shape):
    nd = len(shape)
    return pl.BlockSpec(shape, lambda *_: (0,) * nd, pipeline_mode=pl.Buffered(1))


def _pick(w):
    arr, lead = w if isinstance(w, tuple) else (w, ())
    rest = arr.shape[len(lead):]
    index = tuple(lead) + (0,) * len(rest)
    spec = pl.BlockSpec((None,) * len(lead) + rest, lambda *_: index,
                        pipeline_mode=pl.Buffered(1))
    return arr, spec


def _params(n_axes):
    return pltpu.CompilerParams(
        dimension_semantics=("arbitrary",) * n_axes, vmem_limit_bytes=VMEM_LIMIT)


def _block_kernel(*refs, d_ff, pre_layout, has_ple, has_final):
    it = iter(refs)
    x_ref = next(it)
    if pre_layout:
        y_ref, wpre_ref = next(it), next(it)
    g_ref, win_ref, wout_ref = next(it), next(it), next(it)
    if has_ple:
        gp_ref, wg_ref, p_ref, wp_ref = next(it), next(it), next(it), next(it)
    if has_final:
        gf_ref = next(it)
    o_ref = next(it)

    x = x_ref[...]
    if pre_layout == "groups":
        y = jnp.concatenate([y_ref[0, gi] for gi in range(y_ref.shape[1])], axis=1)
        x = x + _dot(y, wpre_ref[...])
    elif pre_layout == "transposed":
        n_g, gw, tm = y_ref.shape[1:]
        x = x + _dot_tn(y_ref[0].reshape(n_g * gw, tm), wpre_ref[...])
    h = _rms(x, g_ref[...]).astype(BF16)
    gate = _dot(h, win_ref[:, :d_ff])
    up = _dot(h, win_ref[:, d_ff:])
    act = (gate * _sigmoid(gate) * up).astype(BF16)
    x = x + 0.5 * _dot(act, wout_ref[...])
    if has_ple:
        hg = _rms(x, gp_ref[...]).astype(BF16)
        emb_gate = _sigmoid(_dot(hg, wg_ref[...]))
        emb = _dot(p_ref[...].astype(BF16), wp_ref[...])
        x = x + emb_gate * emb
    if has_final:
        x = _rms(x, gf_ref[...])
    o_ref[...] = x


def _block_call(x, g, w_in, w_out, *, pre=None, ple=None, final=None, tm=512):
    t, d = x.shape
    row = lambda i: (i, 0)
    args, specs = [x], [pl.BlockSpec((tm, d), row)]

    def add(w):
        arr, spec = _pick(w)
        args.append(arr)
        specs.append(spec)
        return spec.block_shape

    pre_layout = None
    if pre is not None:
        y, w_pre, pre_layout = pre
        args.append(y)
        steps = (t // y.shape[0]) // tm
        if pre_layout == "groups":
            specs.append(pl.BlockSpec((1, y.shape[1], tm, y.shape[3]),
                                      lambda i: (i // steps, 0, i % steps, 0)))
        else:
            specs.append(pl.BlockSpec((1, y.shape[1], y.shape[2], tm),
                                      lambda i: (i // steps, 0, 0, i % steps)))
        add(w_pre)
    add(g)
    add(w_in)
    d_ff = add(w_out)[-2]
    if ple is not None:
        gp, wg, p, wp = ple
        add(gp)
        add(wg)
        args.append(p[0])
        specs.append(pl.BlockSpec((None, tm, p[0].shape[2]), lambda i: (p[1], i, 0)))
        add(wp)
    if final is not None:
        add(final)
    kern = functools.partial(_block_kernel, d_ff=d_ff, pre_layout=pre_layout,
                             has_ple=ple is not None, has_final=final is not None)
    return pl.pallas_call(
        kern, grid=(t // tm,), in_specs=specs, out_specs=pl.BlockSpec((tm, d), row),
        out_shape=jax.ShapeDtypeStruct((t, d), F32), compiler_params=_params(1),
        name="token_block")(*args)


def _hgrn_proj_kernel(x_ref, g_ref, w_ref, lbl_ref, q_ref, k_ref, v_ref, lf_ref, sg_ref,
                      *, width, layer, q_scale):
    h = _rms(x_ref[...], g_ref[...]).astype(BF16)
    q = _dot(h, w_ref[:, 0 * width:1 * width])
    f = _dot(h, w_ref[:, 1 * width:2 * width])
    v = _dot(h, w_ref[:, 2 * width:3 * width])
    og = _dot(h, w_ref[:, 3 * width:4 * width])
    lbl = lbl_ref[...]
    e = jnp.exp(lbl - jnp.max(lbl, axis=0, keepdims=True))
    sm = e / jnp.sum(e, axis=0, keepdims=True)
    lb = jnp.sum(sm[:layer + 1], axis=0, keepdims=True)
    sf = _sigmoid(f)
    outs = ((q_ref, q * q_scale), (k_ref, (1.0 - lb) * _sigmoid(-f)), (v_ref, v),
            (lf_ref, jnp.log(lb + (1.0 - lb) * sf)), (sg_ref, _sigmoid(og)))
    gw = q_ref.shape[3]
    for ref, val in outs:
        for gi in range(ref.shape[1]):
            ref[0, gi] = val[:, gi * gw:(gi + 1) * gw].astype(ref.dtype)


def _hgrn_proj_call(x, g, w_in, lb_logits, *, bsz, layer, key_dim, group_width, tm=512):
    t, d = x.shape
    seq = t // bsz
    steps = seq // tm
    width = w_in.shape[1] // 4
    groups = width // group_width
    kern = functools.partial(_hgrn_proj_kernel, width=width, layer=layer,
                             q_scale=float(key_dim) ** -0.5)
    out_spec = pl.BlockSpec((1, groups, tm, group_width), lambda i: (i // steps, 0, i % steps, 0))
    shape = (bsz, groups, seq, group_width)
    bf = jax.ShapeDtypeStruct(shape, BF16)
    return pl.pallas_call(
        kern, grid=(t // tm,),
        in_specs=[pl.BlockSpec((tm, d), lambda i: (i, 0)), _const_spec(g.shape),
                  _const_spec(w_in.shape), _const_spec(lb_logits.shape)],
        out_specs=[out_spec] * 5,
        out_shape=[bf, bf, bf, jax.ShapeDtypeStruct(shape, F32), bf],
        compiler_params=_params(1), name="hgrn_proj")(x, g, w_in, lb_logits)


def _hgrn_rec_constants(ts, dk):
    incl = jnp.arange(ts)[None, :] <= jnp.arange(ts)[:, None]
    prefix = jnp.concatenate([incl, incl, incl], axis=1).astype(BF16)
    col = jnp.arange(ts)[None, :]
    reduce_j = (jnp.arange(A_SUB * dk)[:, None] // dk) == (col % A_SUB)
    return prefix, reduce_j.astype(BF16)


def _hgrn_rec_kernel(q_ref, k_ref, v_ref, lf_ref, sg_ref, gain_ref, prefix_ref, reduce_ref,
                     o_ref, st_ref, k_scr, bs_scr, e_scr, a_scr, *, ts):
    @pl.when(pl.program_id(2) == 0)
    def _():
        st_ref[...] = jnp.zeros_like(st_ref)

    dk = gain_ref.shape[-1]
    heads = range(st_ref.shape[0])
    lanes = [slice(hd * dk, (hd + 1) * dk) for hd in heads]

    g = lf_ref[0, 0] * LOG2_E
    g1 = g.astype(BF16)
    r1 = g - g1.astype(F32)
    g2 = r1.astype(BF16)
    g3 = (r1 - g2.astype(F32)).astype(BF16)
    big_b = _dot(prefix_ref[...], jnp.concatenate([g1, g2, g3], axis=0))
    parts = [_hgrn_decays(q_ref[0, 0, :, ln].astype(F32), k_ref[0, 0, :, ln].astype(F32),
                          big_b[:, ln], ts=ts) for ln in lanes]
    steepest = functools.reduce(jnp.maximum, [jnp.max(-pt["b_sub"]) for pt in parts])

    @pl.when(steepest <= A_SAFE_LOG2_DECAY)
    def _():
        for hd, pt in zip(heads, parts):
            k_up = (pt["k"] * jnp.exp2(-pt["b_sub"])).astype(BF16)
            a_scr[hd] = _dot_nt(pt["q_sub"], k_up)

    @pl.when(steepest > A_SAFE_LOG2_DECAY)
    def _():
        for hd, pt in zip(heads, parts):
            a_scr[hd] = _hgrn_exact_sub_scores(pt["q"], pt["k"], pt["b_sub"], reduce_ref[...],
                                                k_scr.at[hd], bs_scr.at[hd], e_scr.at[hd], ts=ts)

    for hd, pt, ln in zip(heads, parts, lanes):
        o, st_ref[hd] = _hgrn_outputs(pt, v_ref[0, 0, :, ln], st_ref[hd], a_scr[hd], ts=ts)
        o = _rms(o, gain_ref[...]) * sg_ref[0, 0, :, ln].astype(F32)
        o_ref[0, 0, :, ln] = o.astype(BF16)


def _hgrn_decays(q, k, big_b, *, ts):
    C, c = A_CHUNK, A_SUB
    nc, dk = ts // C, q.shape[-1]
    b_last = big_b[ts - 1:ts]
    after = b_last - big_b

    def since_start_of(size):
        groups = [big_b[:size]] + [big_b[lo:lo + size] - big_b[lo - 1:lo]
                                   for lo in range(size, ts, size)]
        return jnp.concatenate(groups, axis=0)

    b_chunk = since_start_of(C)
    b_sub = since_start_of(c)
    to_chunk_end = jnp.concatenate(
        [big_b[lo + C - 1:lo + C] - big_b[lo:lo + C] for lo in range(0, ts, C)],
        axis=0).reshape(nc, C, dk)

    return dict(
        q=q, k=k, b_last=b_last, after=after, b_sub=b_sub, to_chunk_end=to_chunk_end,
        q_blk=(q * jnp.exp2(big_b)).astype(BF16),
        k_blk=(k * jnp.exp2(after)).astype(BF16),
        q_chunk=(q * jnp.exp2(b_chunk)).astype(BF16),
        q_sub=(q * jnp.exp2(b_sub)).astype(BF16))


def _hgrn_outputs(pt, v, st, a_rep, *, ts):
    C, c = A_CHUNK, A_SUB
    nc, nsub = ts // C, C // c
    k, after, to_chunk_end = pt["k"], pt["after"], pt["to_chunk_end"]
    q_chunk, q_sub = pt["q_chunk"], pt["q_sub"]
    dk = k.shape[-1]

    zeros2 = lambda n: jnp.zeros((n, dk), BF16)
    qs_parts, ks_parts = [], []
    for n in range(1, nc):
        lo, hi = n * C, (n + 1) * C
        decay = jnp.exp2(after[:lo] - after[lo - 1:lo])
        ks_parts.append(jnp.concatenate([(k[:lo] * decay).astype(BF16), zeros2(ts - lo)], axis=0))
        pieces = [zeros2(lo), q_chunk[lo:hi]] + ([zeros2(ts - hi)] if hi < ts else [])
        qs_parts.append(jnp.concatenate(pieces, axis=0))
    a_chunks = _dot_nt(jnp.concatenate(qs_parts, axis=1), jnp.concatenate(ks_parts, axis=1))

    k3 = k.reshape(nc, C, dk)
    qs3 = q_sub.reshape(nc, C, dk)
    zeros3 = lambda n: jnp.zeros((nc, n, dk), BF16)
    qs_parts, ks_parts = [], []
    for i in range(1, nsub):
        lo, hi = i * c, (i + 1) * c
        decay = jnp.exp2(to_chunk_end[:, :lo] - to_chunk_end[:, lo - 1:lo])
        ks_parts.append(jnp.concatenate([(k3[:, :lo] * decay).astype(BF16), zeros3(C - lo)], axis=1))
        pieces = [zeros3(lo), qs3[:, lo:hi]] + ([zeros3(C - hi)] if hi < C else [])
        qs_parts.append(jnp.concatenate(pieces, axis=1))
    a_subs = _dot_nt(jnp.concatenate(qs_parts, axis=2).reshape(ts, (nsub - 1) * dk),
                     jnp.concatenate(ks_parts, axis=2).reshape(ts, (nsub - 1) * dk))

    rr = lax.broadcasted_iota(jnp.int32, (ts, ts), 0)
    ll = lax.broadcasted_iota(jnp.int32, (ts, ts), 1)
    lc, ls = C.bit_length() - 1, c.bit_length() - 1
    same_chunk = (rr >> lc) == (ll >> lc)
    own_sub = ((rr >> ls) == (ll >> ls)) & ((ll & (c - 1)) <= (rr & (c - 1)))
    a = jnp.where(own_sub, a_rep, 0.0) + jnp.where(same_chunk, a_subs, 0.0) + a_chunks

    o = _dot(a.astype(BF16), v) + _dot_nt(pt["q_blk"], st.astype(BF16))
    return o, st * jnp.exp2(pt["b_last"]) + _dot_tn(v, pt["k_blk"])


def _hgrn_exact_sub_scores(q, k, b_sub, reduce_j, k_scr, bs_scr, e_scr, *, ts):
    c, dk = A_SUB, q.shape[-1]
    k_scr[...] = k
    bs_scr[...] = b_sub
    zero_tile = jnp.zeros((F32_SUBLANES, dk), F32)
    for n in range(ts // c):
        tiles = [slice(n * c + lo, n * c + lo + F32_SUBLANES) for lo in range(0, c, F32_SUBLANES)]
        for j in range(c):
            row = slice(n * c + j, n * c + j + 1)
            k_j, b_j = k_scr[row, :], bs_scr[row, :]
            e = [q[t] * k_j * jnp.exp2(jnp.minimum(b_sub[t] - b_j, 0.0))
                 if t.stop > n * c + j else zero_tile for t in tiles]
            e_scr[n * c:(n + 1) * c, j * dk:(j + 1) * dk] = jnp.concatenate(e, axis=0).astype(BF16)
    return _dot(e_scr[...], reduce_j)


def _hgrn_rec_call(q, k, v, lf, sg, gain, *, ts=256):
    bsz, groups, seq, group_width = q.shape
    dk = gain.shape[-1]
    heads_per_step = group_width // dk
    blk = pl.BlockSpec((1, 1, ts, group_width), lambda b, h, s: (b, h, s, 0))
    prefix, reduce_j = _hgrn_rec_constants(ts, dk)
    kern = functools.partial(_hgrn_rec_kernel, ts=ts)
    return pl.pallas_call(
        kern, grid=(bsz, groups, seq // ts),
        in_specs=[blk, blk, blk, blk, blk, _const_spec(gain.shape), _const_spec(prefix.shape),
                  _const_spec(reduce_j.shape)],
        out_specs=blk, out_shape=jax.ShapeDtypeStruct(q.shape, BF16),
        scratch_shapes=[pltpu.VMEM((heads_per_step, dk, dk), F32),
                        pltpu.VMEM((heads_per_step, ts, dk), F32),
                        pltpu.VMEM((heads_per_step, ts, dk), F32),
                        pltpu.VMEM((heads_per_step, ts, A_SUB * dk), BF16),
                        pltpu.VMEM((heads_per_step, ts, ts), F32)],
        compiler_params=_params(3), name="hgrn_rec")(q, k, v, lf, sg, gain, prefix, reduce_j)


def _rope_tables_lanes(pos_row, inv_freq_col, half):
    ang = inv_freq_col * pos_row.astype(F32)
    cos, sin = jnp.cos(ang), jnp.sin(ang)
    zeros = jnp.zeros((128 - 2 * half, ang.shape[1]), F32)
    c_tab = jnp.transpose(jnp.concatenate([cos, cos, zeros], axis=0))
    s_tab = jnp.transpose(jnp.concatenate([-sin, sin, zeros], axis=0))
    return c_tab, s_tab


def _kv_kernel(x_ref, pos_ref, g_ref, wdc_ref, wdr_ref, gl_ref, wk_ref, wvt_ref, freq_ref,
               k_out, vt_out, *, heads, nope, vdim, half):
    h = _rms(x_ref[0], g_ref[...]).astype(BF16)
    c_kv = _rms(_dot(h, wdc_ref[...]), gl_ref[...]).astype(BF16)
    kr = _dot(h, wdr_ref[...])
    c_tab, s_tab = _rope_tables_lanes(pos_ref[0], freq_ref[...], half)
    k_rope = (kr * c_tab + pltpu.roll(kr, half, 1) * s_tab).astype(BF16)
    k_nope = _dot(c_kv, wk_ref[...])
    v_t = _dot_nt(wvt_ref[...], c_kv)
    ones = jnp.ones((vt_out.shape[2] - vdim, v_t.shape[1]), BF16)
    for hd in range(heads):
        k_out[0, hd, :, 0:nope] = k_nope[:, hd * nope:(hd + 1) * nope].astype(BF16)
        k_out[0, hd, :, nope:] = k_rope
        vt_out[0, hd, 0:vdim, :] = v_t[hd * vdim:(hd + 1) * vdim].astype(BF16)
        vt_out[0, hd, vdim:, :] = ones


def _kv_call(x, pos_row, g, wd_c, wd_r, gl, wk, wvt, freq_col, *, heads, nope, vdim, half, tm=512):
    bsz, seq, d = x.shape
    kern = functools.partial(_kv_kernel, heads=heads, nope=nope, vdim=vdim, half=half)
    kd = nope + 128
    vrows = vdim + BF16_SUBLANES
    return pl.pallas_call(
        kern, grid=(bsz, seq // tm),
        in_specs=[pl.BlockSpec((1, tm, d), lambda b, s: (b, s, 0)),
                  pl.BlockSpec((1, 1, tm), lambda b, s: (b, 0, s)),
                  _const_spec(g.shape), _const_spec(wd_c.shape), _const_spec(wd_r.shape),
                  _const_spec(gl.shape), _const_spec(wk.shape), _const_spec(wvt.shape),
                  _const_spec(freq_col.shape)],
        out_specs=[pl.BlockSpec((1, heads, tm, kd), lambda b, s: (b, 0, s, 0)),
                   pl.BlockSpec((1, heads, vrows, tm), lambda b, s: (b, 0, 0, s))],
        out_shape=[jax.ShapeDtypeStruct((bsz, heads, seq, kd), BF16),
                   jax.ShapeDtypeStruct((bsz, heads, vrows, seq), BF16)],
        compiler_params=_params(2), name="mla_shared_kv")(
            x, pos_row, g, wd_c, wd_r, gl, wk, wvt, freq_col)


def _q_kernel(x_ref, pos_ref, g_ref, wdq_ref, gq_ref, wuqt_ref, freq_ref, qt_out,
              *, heads, nope, half, qd, scale):
    h = _rms(x_ref[0], g_ref[...]).astype(BF16)
    c_q = (_rms(_dot(h, wdq_ref[...]), gq_ref[...]) * scale).astype(BF16)
    q_t = _dot_nt(wuqt_ref[...], c_q)
    ang = freq_ref[...] * pos_ref[0].astype(F32)
    cos, sin = jnp.cos(ang), jnp.sin(ang)
    zeros = jnp.zeros((qd - nope - 2 * half, q_t.shape[1]), F32)
    for hd in range(heads):
        base = hd * qd
        x1 = q_t[base + nope:base + nope + half]
        x2 = q_t[base + nope + half:base + nope + 2 * half]
        full = jnp.concatenate(
            [q_t[base:base + nope], x1 * cos - x2 * sin, x2 * cos + x1 * sin, zeros], axis=0)
        qt_out[0, hd] = full.astype(BF16)


def _q_call(x, pos_row, g, w_dq, gq, w_uqt, freq_col, *, heads, nope, half, qd, scale, tm=512):
    bsz, seq, d = x.shape
    kern = functools.partial(_q_kernel, heads=heads, nope=nope, half=half, qd=qd, scale=scale)
    return pl.pallas_call(
        kern, grid=(bsz, seq // tm),
        in_specs=[pl.BlockSpec((1, tm, d), lambda b, s: (b, s, 0)),
                  pl.BlockSpec((1, 1, tm), lambda b, s: (b, 0, s)),
                  _const_spec(g.shape), _const_spec(w_dq.shape), _const_spec(gq.shape),
                  _const_spec(w_uqt.shape), _const_spec(freq_col.shape)],
        out_specs=pl.BlockSpec((1, heads, qd, tm), lambda b, s: (b, 0, 0, s)),
        out_shape=jax.ShapeDtypeStruct((bsz, heads, qd, seq), BF16),
        compiler_params=_params(2), name="mla_q")(x, pos_row, g, w_dq, gq, w_uqt, freq_col)


def _attn_kernel(qt_ref, k_ref, vt_ref, o_ref, s0, s1, p0, p1, acc_ref, *, tq, tk, dv):
    qi = pl.program_id(2)

    def key_block(j):
        return k_ref[0, 0, pl.ds(pl.multiple_of(j * tk, tk), tk), :]

    def value_block(j):
        return vt_ref[0, 0, :, pl.ds(pl.multiple_of(j * tk, tk), tk)]

    def scores(j, s_out):
        s = _dot(key_block(j), qt_ref[0, 0])
        s_out[...] = s
        return jnp.max(s, axis=0, keepdims=True)

    def probabilities(s, m, m_blk):
        m_new = jnp.maximum(m, m_blk)
        return m_new, jnp.exp2(m - m_new), jnp.exp2((s - m_new).astype(BF16))

    def softmax_step(s_in, p_out, m, m_blk):
        m_new, alpha, p = probabilities(s_in[...], m, m_blk)
        p_out[...] = p
        return m_new, alpha

    def accumulate(j, p_in, alpha):
        acc_ref[...] = alpha * acc_ref[...] + _dot(value_block(j), p_in[...])

    def body(j, carry):
        a_prev, m, mb0 = carry
        mb1 = scores(2 * j + 1, s1)
        m, a0 = softmax_step(s0, p0, m, mb0)
        accumulate(jnp.maximum(2 * j - 1, 0), p1, a_prev)
        mb0 = scores(2 * j + 2, s0)
        m, a1 = softmax_step(s1, p1, m, mb1)
        accumulate(2 * j, p0, a0)
        return a1, m, mb0

    n_diag = tq // tk
    n_full = n_diag * qi
    acc_ref[...] = jnp.zeros_like(acc_ref)
    p1[...] = jnp.zeros_like(p1)
    init = (jnp.ones((1, tq), F32), jnp.full((1, tq), NEG_BIG, F32), scores(0, s0))
    a_prev, m, _ = lax.fori_loop(0, (n_diag // 2) * qi, body, init)
    accumulate(jnp.maximum(n_full - 1, 0), p1, a_prev)

    causal = (lax.broadcasted_iota(jnp.int32, (tk, tk), 0)
              <= lax.broadcasted_iota(jnp.int32, (tk, tk), 1))
    for d in range(n_diag):
        lo = d * tk
        s = s0[...] if d == 0 else _dot(key_block(n_full + d), qt_ref[0, 0, :, lo:])
        masked = jnp.where(causal, s[:, :tk], NEG_BIG)
        s = jnp.concatenate([masked, s[:, tk:]], axis=1) if lo + tk < tq else masked
        m_d, a_d, p_d = probabilities(s, m[:, lo:], jnp.max(s, axis=0, keepdims=True))
        m = jnp.concatenate([m[:, :lo], m_d], axis=1) if lo else m_d
        acc_ref[:, lo:] = a_d * acc_ref[:, lo:] + _dot(value_block(n_full + d), p_d)
    acc = acc_ref[...]
    o_ref[0, 0] = (acc[:dv] / acc[dv:dv + 1]).astype(BF16)


def _attn_call(qt, k, vt, *, tq=2048, tk=512):
    assert (tq // tk) % 2 == 0
    bsz, heads, qd, seq = qt.shape
    vrows = vt.shape[2]
    dv = vrows - BF16_SUBLANES
    kern = functools.partial(_attn_kernel, tq=tq, tk=tk, dv=dv)
    return pl.pallas_call(
        kern, grid=(bsz, heads, seq // tq),
        in_specs=[pl.BlockSpec((1, 1, qd, tq), lambda b, h, i: (b, h, 0, i)),
                  pl.BlockSpec((1, 1, seq, k.shape[3]), lambda b, h, i: (b, h, 0, 0)),
                  pl.BlockSpec((1, 1, vrows, seq), lambda b, h, i: (b, h, 0, 0))],
        out_specs=pl.BlockSpec((1, 1, dv, tq), lambda b, h, i: (b, h, 0, i)),
        out_shape=jax.ShapeDtypeStruct((bsz, heads, dv, seq), BF16),
        scratch_shapes=[pltpu.VMEM((tk, tq), F32), pltpu.VMEM((tk, tq), F32),
                        pltpu.VMEM((tk, tq), BF16), pltpu.VMEM((tk, tq), BF16),
                        pltpu.VMEM((vrows, tq), F32)],
        compiler_params=_params(3), name="mla_attention")(qt, k, vt)


def kernel(x, p, positions, norm_gains, ffn_w_in, ffn_w_out, ple_w_gate, ple_w_in, a_w_in,
           a_lb_logits, a_out_gain, a_w_out, kv_norm_in, kv_w_down, kv_latent_norm, kv_w_up,
           b_w_dq, b_q_norm, b_w_uq, b_w_out, final_norm):
    bsz, seq, d = x.shape
    depth = norm_gains.shape[0]
    n_a = a_w_in.shape[0]
    t = bsz * seq
    bf = lambda w: w.astype(BF16)
    row = lambda g: g.reshape(1, -1).astype(F32)

    a_key = a_out_gain.shape[1]
    kv_lora = kv_latent_norm.shape[0]
    rope = kv_w_down.shape[1] - kv_lora
    half = rope // 2
    vdim = 128
    heads = b_w_out.shape[1] // vdim
    nope = kv_w_up.shape[1] // heads - vdim
    qd = 256
    assert nope == 128 and rope == 64 and b_w_uq.shape[2] == heads * (nope + rope)

    inv_freq = 1.0 / (ROPE_THETA ** (jnp.arange(0, rope, 2, dtype=F32) / rope))
    freq_col = inv_freq.reshape(half, 1)
    pos_row = positions.reshape(bsz, 1, seq)

    wd_c = bf(kv_w_down[:, :kv_lora])
    wd_r = bf(jnp.concatenate([kv_w_down[:, kv_lora:]] * (128 // rope), axis=1))
    w_up = kv_w_up.reshape(kv_lora, heads, nope + vdim)
    wk = bf(w_up[:, :, :nope].reshape(kv_lora, heads * nope))
    wvt = bf(w_up[:, :, nope:].reshape(kv_lora, heads * vdim).T)

    def shared_kv(xs):
        return _kv_call(xs.reshape(bsz, seq, d), pos_row, row(kv_norm_in), wd_c, wd_r,
                        row(kv_latent_norm), wk, wvt, freq_col,
                        heads=heads, nope=nope, vdim=vdim, half=half)

    ffn_in, ffn_out = bf(ffn_w_in), bf(ffn_w_out)
    ple_gate, ple_in = bf(ple_w_gate), bf(ple_w_in)
    gains = norm_gains.astype(F32).reshape(depth, norm_gains.shape[1], 1, d)
    p_tokens = p.reshape(depth, t, -1)

    xf = x.reshape(t, d)
    shared = shared_kv(xf) if n_a == 0 else None
    attn_scale = float(nope + rope) ** -0.5 * LOG2_E
    for li in range(depth):
        x1 = _block_call(xf, (gains, (li, 0)), (ffn_in, (li, 0)), (ffn_out, (li, 0)))
        if li < n_a:
            q, k, v, lf, sg = _hgrn_proj_call(
                x1, row(norm_gains[li, 1]), bf(a_w_in[li]), a_lb_logits.astype(F32), bsz=bsz,
                layer=li, key_dim=a_key, group_width=A_HEADS_PER_STEP * a_key)
            o = _hgrn_rec_call(q, k, v, lf, sg, row(a_out_gain[li]))
            pre = (o, bf(a_w_out[li]), "groups")
        else:
            bi = li - n_a
            w_uq = b_w_uq[bi].reshape(-1, heads, nope + rope)
            w_uq = jnp.pad(w_uq, ((0, 0), (0, 0), (0, qd - nope - rope)))
            w_uqt = bf(w_uq.reshape(-1, heads * qd).T)
            qt = _q_call(x1.reshape(bsz, seq, d), pos_row, row(norm_gains[li, 1]),
                         bf(b_w_dq[bi]), row(b_q_norm[bi]), w_uqt, freq_col, heads=heads,
                         nope=nope, half=half, qd=qd, scale=attn_scale)
            k_all, vt_all = shared
            o = _attn_call(qt, k_all, vt_all)
            pre = (o, bf(b_w_out[bi]), "transposed")
        ple = ((gains, (li, 3)), (ple_gate, (li,)), (p_tokens, li), (ple_in, (li,)))
        final = row(final_norm) if li == depth - 1 else None
        xf = _block_call(x1, (gains, (li, 2)), (ffn_in, (li, 1)), (ffn_out, (li, 1)),
                         pre=pre, ple=ple, final=final)
        if li == n_a - 1:
            shared = shared_kv(xf)
    return xf.reshape(bsz, seq, d)
```

```python
import functools
import math

import jax
import jax.numpy as jnp
from jax import lax
from jax.experimental import pallas as pl
from jax.experimental.pallas import tpu as pltpu

F32 = jnp.float32
BF16 = jnp.bfloat16

EPS = 1e-6
ROPE_THETA = 10000.0
A_CHUNK = 64
A_SUB = 16
A_SAFE_LOG2_DECAY = 96.0
NEG_BIG = -1e30
LOG2_E = math.log2(math.e)

V7X_VMEM_BYTES = 64 * 1024 * 1024
VMEM_LIMIT = V7X_VMEM_BYTES * 7 // 8
F32_SUBLANES = 8
BF16_SUBLANES = 16


def _rms(x, g):
    ms = jnp.mean(x * x, axis=-1, keepdims=True)
    return x * lax.rsqrt(ms + EPS) * g


def _sigmoid(x):
    return 1.0 / (1.0 + jnp.exp(-x))


def _dot(a, b):
    return jnp.dot(a, b, preferred_element_type=F32)


def _dot_nt(a, b):
    return lax.dot_general(a, b, (((1,), (1,)), ((), ())), preferred_element_type=F32)


def _dot_tn(a, b):
    return lax.dot_general(a, b, (((0,), (0,)), ((), ())), preferred_element_type=F32)


def _const_spec(shape):
    nd = len(shape)
    return pl.BlockSpec(shape, lambda *_: (0,) * nd, pipeline_mode=pl.Buffered(1))


def _pick(w):
    arr, lead = w if isinstance(w, tuple) else (w, ())
    rest = arr.shape[len(lead):]
    index = tuple(lead) + (0,) * len(rest)
    spec = pl.BlockSpec((None,) * len(lead) + rest, lambda *_: index,
                        pipeline_mode=pl.Buffered(1))
    return arr, spec


def _params(n_axes):
    return pltpu.CompilerParams(
        dimension_semantics=("arbitrary",) * n_axes, vmem_limit_bytes=VMEM_LIMIT)


def _block_kernel(*refs, d_ff, pre_layout, has_ple, has_final):
    it = iter(refs)
    x_ref = next(it)
    if pre_layout:
        y_ref, wpre_ref = next(it), next(it)
    g_ref, win_ref, wout_ref = next(it), next(it), next(it)
    if has_ple:
        gp_ref, wg_ref, p_ref, wp_ref = next(it), next(it), next(it), next(it)
    if has_final:
        gf_ref = next(it)
    o_ref = next(it)

    x = x_ref[...]
    if pre_layout == "groups":
        y = jnp.concatenate([y_ref[0, gi] for gi in range(y_ref.shape[1])], axis=1)
        x = x + _dot(y, wpre_ref[...])
    elif pre_layout == "transposed":
        n_g, gw, tm = y_ref.shape[1:]
        x = x + _dot_tn(y_ref[0].reshape(n_g * gw, tm), wpre_ref[...])
    h = _rms(x, g_ref[...]).astype(BF16)
    gate = _dot(h, win_ref[:, :d_ff])
    up = _dot(h, win_ref[:, d_ff:])
    act = (gate * _sigmoid(gate) * up).astype(BF16)
    x = x + 0.5 * _dot(act, wout_ref[...])
    if has_ple:
        hg = _rms(x, gp_ref[...]).astype(BF16)
        emb_gate = _sigmoid(_dot(hg, wg_ref[...]))
        emb = _dot(p_ref[...].astype(BF16), wp_ref[...])
        x = x + emb_gate * emb
    if has_final:
        x = _rms(x, gf_ref[...])
    o_ref[...] = x


def _block_call(x, g, w_in, w_out, *, pre=None, ple=None, final=None, tm=512):
    t, d = x.shape
    row = lambda i: (i, 0)
    args, specs = [x], [pl.BlockSpec((tm, d), row)]

    def add(w):
        arr, spec = _pick(w)
        args.append(arr)
        specs.append(spec)
        return spec.block_shape

    pre_layout = None
    if pre is not None:
        y, w_pre, pre_layout = pre
        args.append(y)
        steps = (t // y.shape[0]) // tm
        if pre_layout == "groups":
            specs.append(pl.BlockSpec((1, y.shape[1], tm, y.shape[3]),
                                      lambda i: (i // steps, 0, i % steps, 0)))
        else:
            specs.append(pl.BlockSpec((1, y.shape[1], y.shape[2], tm),
                                      lambda i: (i // steps, 0, 0, i % steps)))
        add(w_pre)
    add(g)
    add(w_in)
    d_ff = add(w_out)[-2]
    if ple is not None:
        gp, wg, p, wp = ple
        add(gp)
        add(wg)
        args.append(p[0])
        specs.append(pl.BlockSpec((None, tm, p[0].shape[2]), lambda i: (p[1], i, 0)))
        add(wp)
    if final is not None:
        add(final)
    kern = functools.partial(_block_kernel, d_ff=d_ff, pre_layout=pre_layout,
                             has_ple=ple is not None, has_final=final is not None)
    return pl.pallas_call(
        kern, grid=(t // tm,), in_specs=specs, out_specs=pl.BlockSpec((tm, d), row),
        out_shape=jax.ShapeDtypeStruct((t, d), F32), compiler_params=_params(1),
        name="token_block")(*args)


def _hgrn_rec_constants(ts, dk):
    incl = jnp.arange(ts)[None, :] <= jnp.arange(ts)[:, None]
    prefix = jnp.concatenate([incl, incl, incl], axis=1).astype(BF16)
    col = jnp.arange(ts)[None, :]
    reduce_j = (jnp.arange(A_SUB * dk)[:, None] // dk) == (col % A_SUB)
    return prefix, reduce_j.astype(BF16)


def _hgrn_rec_kernel(x_ref, g_ref, w_ref, lbl_ref, gain_ref, prefix_ref, reduce_ref,
                     o_ref, st_ref, k_scr, bs_scr, e_scr, a_scr, *, ts, layer, q_scale):
    @pl.when(pl.program_id(2) == 0)
    def _():
        st_ref[...] = jnp.zeros_like(st_ref)

    dk = gain_ref.shape[-1]
    heads = range(st_ref.shape[0])
    lanes = [slice(hd * dk, (hd + 1) * dk) for hd in heads]
    width = len(heads) * dk

    h = _rms(x_ref[0], g_ref[...]).astype(BF16)
    q = _dot(h, w_ref[:, 0 * width:1 * width]) * q_scale
    f = _dot(h, w_ref[:, 1 * width:2 * width])
    v = _dot(h, w_ref[:, 2 * width:3 * width]).astype(BF16)
    out_gate = _sigmoid(_dot(h, w_ref[:, 3 * width:4 * width]))
    lbl = lbl_ref[...]
    e = jnp.exp(lbl - jnp.max(lbl, axis=0, keepdims=True))
    sm = e / jnp.sum(e, axis=0, keepdims=True)
    lb = jnp.sum(sm[:layer + 1], axis=0, keepdims=True)
    k = (1.0 - lb) * _sigmoid(-f)
    g = jnp.log(lb + (1.0 - lb) * _sigmoid(f)) * LOG2_E

    g1 = g.astype(BF16)
    r1 = g - g1.astype(F32)
    g2 = r1.astype(BF16)
    g3 = (r1 - g2.astype(F32)).astype(BF16)
    big_b = _dot(prefix_ref[...], jnp.concatenate([g1, g2, g3], axis=0))
    parts = [_hgrn_decays(q[:, ln], k[:, ln], big_b[:, ln], ts=ts) for ln in lanes]
    steepest = functools.reduce(jnp.maximum, [jnp.max(-pt["b_sub"]) for pt in parts])

    @pl.when(steepest <= A_SAFE_LOG2_DECAY)
    def _():
        for hd, pt in zip(heads, parts):
            k_up = (pt["k"] * jnp.exp2(-pt["b_sub"])).astype(BF16)
            a_scr[hd] = _dot_nt(pt["q_sub"], k_up)

    @pl.when(steepest > A_SAFE_LOG2_DECAY)
    def _():
        for hd, pt in zip(heads, parts):
            a_scr[hd] = _hgrn_exact_sub_scores(pt["q"], pt["k"], pt["b_sub"], reduce_ref[...],
                                                k_scr.at[hd], bs_scr.at[hd], e_scr.at[hd], ts=ts)

    for hd, pt, ln in zip(heads, parts, lanes):
        o, st_ref[hd] = _hgrn_outputs(pt, v[:, ln], st_ref[hd], a_scr[hd], ts=ts)
        o_ref[0, 0, :, ln] = (_rms(o, gain_ref[...]) * out_gate[:, ln]).astype(BF16)


def _hgrn_decays(q, k, big_b, *, ts):
    C, c = A_CHUNK, A_SUB
    nc, dk = ts // C, q.shape[-1]
    b_last = big_b[ts - 1:ts]
    after = b_last - big_b

    def since_start_of(size):
        groups = [big_b[:size]] + [big_b[lo:lo + size] - big_b[lo - 1:lo]
                                   for lo in range(size, ts, size)]
        return jnp.concatenate(groups, axis=0)

    b_chunk = since_start_of(C)
    b_sub = since_start_of(c)
    to_chunk_end = jnp.concatenate(
        [big_b[lo + C - 1:lo + C] - big_b[lo:lo + C] for lo in range(0, ts, C)],
        axis=0).reshape(nc, C, dk)

    return dict(
        q=q, k=k, b_last=b_last, after=after, b_sub=b_sub, to_chunk_end=to_chunk_end,
        q_blk=(q * jnp.exp2(big_b)).astype(BF16),
        k_blk=(k * jnp.exp2(after)).astype(BF16),
        q_chunk=(q * jnp.exp2(b_chunk)).astype(BF16),
        q_sub=(q * jnp.exp2(b_sub)).astype(BF16))


def _hgrn_outputs(pt, v, st, a_rep, *, ts):
    C, c = A_CHUNK, A_SUB
    nc, nsub = ts // C, C // c
    k, after, to_chunk_end = pt["k"], pt["after"], pt["to_chunk_end"]
    q_chunk, q_sub = pt["q_chunk"], pt["q_sub"]
    dk = k.shape[-1]

    zeros2 = lambda n: jnp.zeros((n, dk), BF16)
    qs_parts, ks_parts = [], []
    for n in range(1, nc):
        lo, hi = n * C, (n + 1) * C
        decay = jnp.exp2(after[:lo] - after[lo - 1:lo])
        ks_parts.append(jnp.concatenate([(k[:lo] * decay).astype(BF16), zeros2(ts - lo)], axis=0))
        pieces = [zeros2(lo), q_chunk[lo:hi]] + ([zeros2(ts - hi)] if hi < ts else [])
        qs_parts.append(jnp.concatenate(pieces, axis=0))
    a_chunks = _dot_nt(jnp.concatenate(qs_parts, axis=1), jnp.concatenate(ks_parts, axis=1))

    k3 = k.reshape(nc, C, dk)
    qs3 = q_sub.reshape(nc, C, dk)
    zeros3 = lambda n: jnp.zeros((nc, n, dk), BF16)
    qs_parts, ks_parts = [], []
    for i in range(1, nsub):
        lo, hi = i * c, (i + 1) * c
        decay = jnp.exp2(to_chunk_end[:, :lo] - to_chunk_end[:, lo - 1:lo])
        ks_parts.append(jnp.concatenate([(k3[:, :lo] * decay).astype(BF16), zeros3(C - lo)], axis=1))
        pieces = [zeros3(lo), qs3[:, lo:hi]] + ([zeros3(C - hi)] if hi < C else [])
        qs_parts.append(jnp.concatenate(pieces, axis=1))
    a_subs = _dot_nt(jnp.concatenate(qs_parts, axis=2).reshape(ts, (nsub - 1) * dk),
                     jnp.concatenate(ks_parts, axis=2).reshape(ts, (nsub - 1) * dk))

    rr = lax.broadcasted_iota(jnp.int32, (ts, ts), 0)
    ll = lax.broadcasted_iota(jnp.int32, (ts, ts), 1)
    lc, ls = C.bit_length() - 1, c.bit_length() - 1
    same_chunk = (rr >> lc) == (ll >> lc)
    own_sub = ((rr >> ls) == (ll >> ls)) & ((ll & (c - 1)) <= (rr & (c - 1)))
    a = jnp.where(own_sub, a_rep, 0.0) + jnp.where(same_chunk, a_subs, 0.0) + a_chunks

    o = _dot(a.astype(BF16), v) + _dot_nt(pt["q_blk"], st.astype(BF16))
    return o, st * jnp.exp2(pt["b_last"]) + _dot_tn(v, pt["k_blk"])


def _hgrn_exact_sub_scores(q, k, b_sub, reduce_j, k_scr, bs_scr, e_scr, *, ts):
    c, dk = A_SUB, q.shape[-1]
    k_scr[...] = k
    bs_scr[...] = b_sub
    zero_tile = jnp.zeros((F32_SUBLANES, dk), F32)
    for n in range(ts // c):
        tiles = [slice(n * c + lo, n * c + lo + F32_SUBLANES) for lo in range(0, c, F32_SUBLANES)]
        for j in range(c):
            row = slice(n * c + j, n * c + j + 1)
            k_j, b_j = k_scr[row, :], bs_scr[row, :]
            e = [q[t] * k_j * jnp.exp2(jnp.minimum(b_sub[t] - b_j, 0.0))
                 if t.stop > n * c + j else zero_tile for t in tiles]
            e_scr[n * c:(n + 1) * c, j * dk:(j + 1) * dk] = jnp.concatenate(e, axis=0).astype(BF16)
    return _dot(e_scr[...], reduce_j)


def _hgrn_mixer_call(x, g, w_in, lb_logits, gain, *, layer, ts=256):
    bsz, seq, d = x.shape
    width = w_in.shape[1] // 4
    dk = gain.shape[-1]
    heads = width // dk
    prefix, reduce_j = _hgrn_rec_constants(ts, dk)
    kern = functools.partial(_hgrn_rec_kernel, ts=ts, layer=layer, q_scale=float(dk) ** -0.5)
    consts = (g, w_in, lb_logits, gain, prefix, reduce_j)
    return pl.pallas_call(
        kern, grid=(bsz, 1, seq // ts),
        in_specs=[pl.BlockSpec((1, ts, d), lambda b, h, s: (b, s, 0))]
        + [_const_spec(c.shape) for c in consts],
        out_specs=pl.BlockSpec((1, 1, ts, width), lambda b, h, s: (b, h, s, 0)),
        out_shape=jax.ShapeDtypeStruct((bsz, 1, seq, width), BF16),
        scratch_shapes=[pltpu.VMEM((heads, dk, dk), F32),
                        pltpu.VMEM((heads, ts, dk), F32),
                        pltpu.VMEM((heads, ts, dk), F32),
                        pltpu.VMEM((heads, ts, A_SUB * dk), BF16),
                        pltpu.VMEM((heads, ts, ts), F32)],
        compiler_params=_params(3), name="hgrn_mixer")(x, *consts)


def _rope_tables_lanes(pos_row, inv_freq_col, half):
    ang = inv_freq_col * pos_row.astype(F32)
    cos, sin = jnp.cos(ang), jnp.sin(ang)
    zeros = jnp.zeros((128 - 2 * half, ang.shape[1]), F32)
    c_tab = jnp.transpose(jnp.concatenate([cos, cos, zeros], axis=0))
    s_tab = jnp.transpose(jnp.concatenate([-sin, sin, zeros], axis=0))
    return c_tab, s_tab


def _kv_kernel(x_ref, pos_ref, g_ref, wdc_ref, wdr_ref, gl_ref, wk_ref, wvt_ref, freq_ref,
               k_out, vt_out, *, heads, nope, vdim, half):
    h = _rms(x_ref[0], g_ref[...]).astype(BF16)
    c_kv = _rms(_dot(h, wdc_ref[...]), gl_ref[...]).astype(BF16)
    kr = _dot(h, wdr_ref[...])
    c_tab, s_tab = _rope_tables_lanes(pos_ref[0], freq_ref[...], half)
    k_rope = (kr * c_tab + pltpu.roll(kr, half, 1) * s_tab).astype(BF16)
    k_nope = _dot(c_kv, wk_ref[...])
    v_t = _dot_nt(wvt_ref[...], c_kv)
    ones = jnp.ones((vt_out.shape[2] - vdim, v_t.shape[1]), BF16)
    for hd in range(heads):
        k_out[0, hd, :, 0:nope] = k_nope[:, hd * nope:(hd + 1) * nope].astype(BF16)
        k_out[0, hd, :, nope:] = k_rope
        vt_out[0, hd, 0:vdim, :] = v_t[hd * vdim:(hd + 1) * vdim].astype(BF16)
        vt_out[0, hd, vdim:, :] = ones


def _kv_call(x, pos_row, g, wd_c, wd_r, gl, wk, wvt, freq_col, *, heads, nope, vdim, half, tm=512):
    bsz, seq, d = x.shape
    kern = functools.partial(_kv_kernel, heads=heads, nope=nope, vdim=vdim, half=half)
    kd = nope + 128
    vrows = vdim + BF16_SUBLANES
    return pl.pallas_call(
        kern, grid=(bsz, seq // tm),
        in_specs=[pl.BlockSpec((1, tm, d), lambda b, s: (b, s, 0)),
                  pl.BlockSpec((1, 1, tm), lambda b, s: (b, 0, s)),
                  _const_spec(g.shape), _const_spec(wd_c.shape), _const_spec(wd_r.shape),
                  _const_spec(gl.shape), _const_spec(wk.shape), _const_spec(wvt.shape),
                  _const_spec(freq_col.shape)],
        out_specs=[pl.BlockSpec((1, heads, tm, kd), lambda b, s: (b, 0, s, 0)),
                   pl.BlockSpec((1, heads, vrows, tm), lambda b, s: (b, 0, 0, s))],
        out_shape=[jax.ShapeDtypeStruct((bsz, heads, seq, kd), BF16),
                   jax.ShapeDtypeStruct((bsz, heads, vrows, seq), BF16)],
        compiler_params=_params(2), name="mla_shared_kv")(
            x, pos_row, g, wd_c, wd_r, gl, wk, wvt, freq_col)


def _q_kernel(x_ref, pos_ref, g_ref, wdq_ref, gq_ref, wuqt_ref, freq_ref, qt_out,
              *, heads, nope, half, qd, scale):
    h = _rms(x_ref[0], g_ref[...]).astype(BF16)
    c_q = (_rms(_dot(h, wdq_ref[...]), gq_ref[...]) * scale).astype(BF16)
    q_t = _dot_nt(wuqt_ref[...], c_q)
    ang = freq_ref[...] * pos_ref[0].astype(F32)
    cos, sin = jnp.cos(ang), jnp.sin(ang)
    zeros = jnp.zeros((qd - nope - 2 * half, q_t.shape[1]), F32)
    for hd in range(heads):
        base = hd * qd
        x1 = q_t[base + nope:base + nope + half]
        x2 = q_t[base + nope + half:base + nope + 2 * half]
        full = jnp.concatenate(
            [q_t[base:base + nope], x1 * cos - x2 * sin, x2 * cos + x1 * sin, zeros], axis=0)
        qt_out[0, hd] = full.astype(BF16)


def _q_call(x, pos_row, g, w_dq, gq, w_uqt, freq_col, *, heads, nope, half, qd, scale, tm=512):
    bsz, seq, d = x.shape
    kern = functools.partial(_q_kernel, heads=heads, nope=nope, half=half, qd=qd, scale=scale)
    return pl.pallas_call(
        kern, grid=(bsz, seq // tm),
        in_specs=[pl.BlockSpec((1, tm, d), lambda b, s: (b, s, 0)),
                  pl.BlockSpec((1, 1, tm), lambda b, s: (b, 0, s)),
                  _const_spec(g.shape), _const_spec(w_dq.shape), _const_spec(gq.shape),
                  _const_spec(w_uqt.shape), _const_spec(freq_col.shape)],
        out_specs=pl.BlockSpec((1, heads, qd, tm), lambda b, s: (b, 0, 0, s)),
        out_shape=jax.ShapeDtypeStruct((bsz, heads, qd, seq), BF16),
        compiler_params=_params(2), name="mla_q")(x, pos_row, g, w_dq, gq, w_uqt, freq_col)


def _attn_kernel(qt_ref, k_ref, vt_ref, o_ref, s0, s1, p0, p1, acc_ref, *, tq, tk, dv):
    qi = pl.program_id(2)

    def key_block(j):
        return k_ref[0, 0, pl.ds(pl.multiple_of(j * tk, tk), tk), :]

    def value_block(j):
        return vt_ref[0, 0, :, pl.ds(pl.multiple_of(j * tk, tk), tk)]

    def scores(j, s_out):
        s = _dot(key_block(j), qt_ref[0, 0])
        s_out[...] = s
        return jnp.max(s, axis=0, keepdims=True)

    def probabilities(s, m, m_blk):
        m_new = jnp.maximum(m, m_blk)
        return m_new, jnp.exp2(m - m_new), jnp.exp2((s - m_new).astype(BF16))

    def softmax_step(s_in, p_out, m, m_blk):
        m_new, alpha, p = probabilities(s_in[...], m, m_blk)
        p_out[...] = p
        return m_new, alpha

    def accumulate(j, p_in, alpha):
        acc_ref[...] = alpha * acc_ref[...] + _dot(value_block(j), p_in[...])

    def body(j, carry):
        a_prev, m, mb0 = carry
        mb1 = scores(2 * j + 1, s1)
        m, a0 = softmax_step(s0, p0, m, mb0)
        accumulate(jnp.maximum(2 * j - 1, 0), p1, a_prev)
        mb0 = scores(2 * j + 2, s0)
        m, a1 = softmax_step(s1, p1, m, mb1)
        accumulate(2 * j, p0, a0)
        return a1, m, mb0

    n_diag = tq // tk
    n_full = n_diag * qi
    acc_ref[...] = jnp.zeros_like(acc_ref)
    p1[...] = jnp.zeros_like(p1)
    init = (jnp.ones((1, tq), F32), jnp.full((1, tq), NEG_BIG, F32), scores(0, s0))
    a_prev, m, _ = lax.fori_loop(0, (n_diag // 2) * qi, body, init)
    accumulate(jnp.maximum(n_full - 1, 0), p1, a_prev)

    causal = (lax.broadcasted_iota(jnp.int32, (tk, tk), 0)
              <= lax.broadcasted_iota(jnp.int32, (tk, tk), 1))
    for d in range(n_diag):
        lo = d * tk
        s = s0[...] if d == 0 else _dot(key_block(n_full + d), qt_ref[0, 0, :, lo:])
        masked = jnp.where(causal, s[:, :tk], NEG_BIG)
        s = jnp.concatenate([masked, s[:, tk:]], axis=1) if lo + tk < tq else masked
        m_d, a_d, p_d = probabilities(s, m[:, lo:], jnp.max(s, axis=0, keepdims=True))
        m = jnp.concatenate([m[:, :lo], m_d], axis=1) if lo else m_d
        acc_ref[:, lo:] = a_d * acc_ref[:, lo:] + _dot(value_block(n_full + d), p_d)
    acc = acc_ref[...]
    o_ref[0, 0] = (acc[:dv] / acc[dv:dv + 1]).astype(BF16)


def _attn_call(qt, k, vt, *, tq=2048, tk=512):
    assert (tq // tk) % 2 == 0
    bsz, heads, qd, seq = qt.shape
    vrows = vt.shape[2]
    dv = vrows - BF16_SUBLANES
    kern = functools.partial(_attn_kernel, tq=tq, tk=tk, dv=dv)
    return pl.pallas_call(
        kern, grid=(bsz, heads, seq // tq),
        in_specs=[pl.BlockSpec((1, 1, qd, tq), lambda b, h, i: (b, h, 0, i)),
                  pl.BlockSpec((1, 1, seq, k.shape[3]), lambda b, h, i: (b, h, 0, 0)),
                  pl.BlockSpec((1, 1, vrows, seq), lambda b, h, i: (b, h, 0, 0))],
        out_specs=pl.BlockSpec((1, 1, dv, tq), lambda b, h, i: (b, h, 0, i)),
        out_shape=jax.ShapeDtypeStruct((bsz, heads, dv, seq), BF16),
        scratch_shapes=[pltpu.VMEM((tk, tq), F32), pltpu.VMEM((tk, tq), F32),
                        pltpu.VMEM((tk, tq), BF16), pltpu.VMEM((tk, tq), BF16),
                        pltpu.VMEM((vrows, tq), F32)],
        compiler_params=_params(3), name="mla_attention")(qt, k, vt)


def kernel(x, p, positions, norm_gains, ffn_w_in, ffn_w_out, ple_w_gate, ple_w_in, a_w_in,
           a_lb_logits, a_out_gain, a_w_out, kv_norm_in, kv_w_down, kv_latent_norm, kv_w_up,
           b_w_dq, b_q_norm, b_w_uq, b_w_out, final_norm):
    bsz, seq, d = x.shape
    depth = norm_gains.shape[0]
    n_a = a_w_in.shape[0]
    t = bsz * seq
    bf = lambda w: w.astype(BF16)
    row = lambda g: g.reshape(1, -1).astype(F32)

    kv_lora = kv_latent_norm.shape[0]
    rope = kv_w_down.shape[1] - kv_lora
    half = rope // 2
    vdim = 128
    heads = b_w_out.shape[1] // vdim
    nope = kv_w_up.shape[1] // heads - vdim
    qd = 256
    assert nope == 128 and rope == 64 and b_w_uq.shape[2] == heads * (nope + rope)

    inv_freq = 1.0 / (ROPE_THETA ** (jnp.arange(0, rope, 2, dtype=F32) / rope))
    freq_col = inv_freq.reshape(half, 1)
    pos_row = positions.reshape(bsz, 1, seq)

    wd_c = bf(kv_w_down[:, :kv_lora])
    wd_r = bf(jnp.concatenate([kv_w_down[:, kv_lora:]] * (128 // rope), axis=1))
    w_up = kv_w_up.reshape(kv_lora, heads, nope + vdim)
    wk = bf(w_up[:, :, :nope].reshape(kv_lora, heads * nope))
    wvt = bf(w_up[:, :, nope:].reshape(kv_lora, heads * vdim).T)

    def shared_kv(xs):
        return _kv_call(xs.reshape(bsz, seq, d), pos_row, row(kv_norm_in), wd_c, wd_r,
                        row(kv_latent_norm), wk, wvt, freq_col,
                        heads=heads, nope=nope, vdim=vdim, half=half)

    ffn_in, ffn_out = bf(ffn_w_in), bf(ffn_w_out)
    ple_gate, ple_in = bf(ple_w_gate), bf(ple_w_in)
    gains = norm_gains.astype(F32).reshape(depth, norm_gains.shape[1], 1, d)
    p_tokens = p.reshape(depth, t, -1)

    xf = x.reshape(t, d)
    shared = shared_kv(xf) if n_a == 0 else None
    attn_scale = float(nope + rope) ** -0.5 * LOG2_E
    for li in range(depth):
        x1 = _block_call(xf, (gains, (li, 0)), (ffn_in, (li, 0)), (ffn_out, (li, 0)))
        if li < n_a:
            o = _hgrn_mixer_call(x1.reshape(bsz, seq, d), row(norm_gains[li, 1]), bf(a_w_in[li]),
                                 a_lb_logits.astype(F32), row(a_out_gain[li]), layer=li)
            pre = (o, bf(a_w_out[li]), "groups")
        else:
            bi = li - n_a
            w_uq = b_w_uq[bi].reshape(-1, heads, nope + rope)
            w_uq = jnp.pad(w_uq, ((0, 0), (0, 0), (0, qd - nope - rope)))
            w_uqt = bf(w_uq.reshape(-1, heads * qd).T)
            qt = _q_call(x1.reshape(bsz, seq, d), pos_row, row(norm_gains[li, 1]),
                         bf(b_w_dq[bi]), row(b_q_norm[bi]), w_uqt, freq_col, heads=heads,
                         nope=nope, half=half, qd=qd, scale=attn_scale)
            k_all, vt_all = shared
            o = _attn_call(qt, k_all, vt_all)
            pre = (o, bf(b_w_out[bi]), "transposed")
        ple = ((gains, (li, 3)), (ple_gate, (li,)), (p_tokens, li), (ple_in, (li,)))
        final = row(final_norm) if li == depth - 1 else None
        xf = _block_call(x1, (gains, (li, 2)), (ffn_in, (li, 1)), (ffn_out, (li, 1)),
                         pre=pre, ple=ple, final=final)
        if li == n_a - 1:
            shared = shared_kv(xf)
    return xf.reshape(bsz, seq, d)
```

```python
import functools
import math

import jax
import jax.numpy as jnp
from jax import lax
from jax.experimental import pallas as pl
from jax.experimental.pallas import tpu as pltpu

F32 = jnp.float32
BF16 = jnp.bfloat16

EPS = 1e-6
ROPE_THETA = 10000.0
A_CHUNK = 64
A_SUB = 16
A_SAFE_LOG2_DECAY = 96.0
NEG_BIG = -1e30
LOG2_E = math.log2(math.e)

V7X_VMEM_BYTES = 64 * 1024 * 1024
VMEM_LIMIT = V7X_VMEM_BYTES * 7 // 8
F32_SUBLANES = 8
BF16_SUBLANES = 16


def _rms(x, g):
    ms = jnp.mean(x * x, axis=-1, keepdims=True)
    return x * lax.rsqrt(ms + EPS) * g


def _sigmoid(x):
    return 1.0 / (1.0 + jnp.exp(-x))


def _dot(a, b):
    return jnp.dot(a, b, preferred_element_type=F32)


def _dot_nt(a, b):
    return lax.dot_general(a, b, (((1,), (1,)), ((), ())), preferred_element_type=F32)


def _dot_tn(a, b):
    return lax.dot_general(a, b, (((0,), (0,)), ((), ())), preferred_element_type=F32)


def _const_spec(shape):
    nd = len(shape)
    return pl.BlockSpec(shape, lambda *_: (0,) * nd, pipeline_mode=pl.Buffered(1))


def _pick(w):
    arr, lead = w if isinstance(w, tuple) else (w, ())
    rest = arr.shape[len(lead):]
    index = tuple(lead) + (0,) * len(rest)
    spec = pl.BlockSpec((None,) * len(lead) + rest, lambda *_: index,
                        pipeline_mode=pl.Buffered(1))
    return arr, spec


def _params(n_axes):
    return pltpu.CompilerParams(
        dimension_semantics=("arbitrary",) * n_axes, vmem_limit_bytes=VMEM_LIMIT)


def _block_kernel(*refs, d_ff, pre_layout, has_ple, has_final):
    it = iter(refs)
    x_ref = next(it)
    if pre_layout:
        y_ref, wpre_ref = next(it), next(it)
    g_ref, win_ref, wout_ref = next(it), next(it), next(it)
    if has_ple:
        gp_ref, wg_ref, p_ref, wp_ref = next(it), next(it), next(it), next(it)
    if has_final:
        gf_ref = next(it)
    o_ref = next(it)

    x = x_ref[...]
    if pre_layout == "groups":
        y = jnp.concatenate([y_ref[0, gi] for gi in range(y_ref.shape[1])], axis=1)
        x = x + _dot(y, wpre_ref[...])
    elif pre_layout == "transposed":
        n_g, gw, tm = y_ref.shape[1:]
        x = x + _dot_tn(y_ref[0].reshape(n_g * gw, tm), wpre_ref[...])
    h = _rms(x, g_ref[...]).astype(BF16)
    gate = _dot(h, win_ref[:, :d_ff])
    up = _dot(h, win_ref[:, d_ff:])
    act = (gate * _sigmoid(gate) * up).astype(BF16)
    x = x + 0.5 * _dot(act, wout_ref[...])
    if has_ple:
        hg = _rms(x, gp_ref[...]).astype(BF16)
        emb_gate = _sigmoid(_dot(hg, wg_ref[...]))
        emb = _dot(p_ref[...].astype(BF16), wp_ref[...])
        x = x + emb_gate * emb
    if has_final:
        x = _rms(x, gf_ref[...])
    o_ref[...] = x


def _block_call(x, g, w_in, w_out, *, pre=None, ple=None, final=None, tm=512):
    t, d = x.shape
    row = lambda i: (i, 0)
    args, specs = [x], [pl.BlockSpec((tm, d), row)]

    def add(w):
        arr, spec = _pick(w)
        args.append(arr)
        specs.append(spec)
        return spec.block_shape

    pre_layout = None
    if pre is not None:
        y, w_pre, pre_layout = pre
        args.append(y)
        steps = (t // y.shape[0]) // tm
        if pre_layout == "groups":
            specs.append(pl.BlockSpec((1, y.shape[1], tm, y.shape[3]),
                                      lambda i: (i // steps, 0, i % steps, 0)))
        else:
            specs.append(pl.BlockSpec((1, y.shape[1], y.shape[2], tm),
                                      lambda i: (i // steps, 0, 0, i % steps)))
        add(w_pre)
    add(g)
    add(w_in)
    d_ff = add(w_out)[-2]
    if ple is not None:
        gp, wg, p, wp = ple
        add(gp)
        add(wg)
        args.append(p[0])
        specs.append(pl.BlockSpec((None, tm, p[0].shape[2]), lambda i: (p[1], i, 0)))
        add(wp)
    if final is not None:
        add(final)
    kern = functools.partial(_block_kernel, d_ff=d_ff, pre_layout=pre_layout,
                             has_ple=ple is not None, has_final=final is not None)
    return pl.pallas_call(
        kern, grid=(t // tm,), in_specs=specs, out_specs=pl.BlockSpec((tm, d), row),
        out_shape=jax.ShapeDtypeStruct((t, d), F32), compiler_params=_params(1),
        name="token_block")(*args)


def _hgrn_rec_constants(ts, dk):
    incl = jnp.arange(ts)[None, :] <= jnp.arange(ts)[:, None]
    prefix = jnp.concatenate([incl, incl, incl], axis=1).astype(BF16)
    col = jnp.arange(ts)[None, :]
    reduce_j = (jnp.arange(A_SUB * dk)[:, None] // dk) == (col % A_SUB)
    return prefix, reduce_j.astype(BF16)


def _hgrn_rec_kernel(x_ref, g_ref, w_ref, lbl_ref, gain_ref, prefix_ref, reduce_ref,
                     o_ref, st_ref, k_scr, bs_scr, e_scr, a_scr, *, ts, layer, q_scale):
    @pl.when(pl.program_id(2) == 0)
    def _():
        st_ref[...] = jnp.zeros_like(st_ref)

    dk = gain_ref.shape[-1]
    heads = range(st_ref.shape[0])
    lanes = [slice(hd * dk, (hd + 1) * dk) for hd in heads]
    width = len(heads) * dk

    h = _rms(x_ref[0], g_ref[...]).astype(BF16)
    q = _dot(h, w_ref[:, 0 * width:1 * width]) * q_scale
    f = _dot(h, w_ref[:, 1 * width:2 * width])
    v = _dot(h, w_ref[:, 2 * width:3 * width]).astype(BF16)
    out_gate = _sigmoid(_dot(h, w_ref[:, 3 * width:4 * width]))
    lbl = lbl_ref[...]
    e = jnp.exp(lbl - jnp.max(lbl, axis=0, keepdims=True))
    sm = e / jnp.sum(e, axis=0, keepdims=True)
    lb = jnp.sum(sm[:layer + 1], axis=0, keepdims=True)
    k = (1.0 - lb) * _sigmoid(-f)
    g = jnp.log(lb + (1.0 - lb) * _sigmoid(f)) * LOG2_E

    g1 = g.astype(BF16)
    r1 = g - g1.astype(F32)
    g2 = r1.astype(BF16)
    g3 = (r1 - g2.astype(F32)).astype(BF16)
    big_b = _dot(prefix_ref[...], jnp.concatenate([g1, g2, g3], axis=0))
    parts = [_hgrn_decays(q[:, ln], k[:, ln], big_b[:, ln], ts=ts) for ln in lanes]
    steepest = functools.reduce(jnp.maximum, [jnp.max(-pt["b_sub"]) for pt in parts])

    @pl.when(steepest <= A_SAFE_LOG2_DECAY)
    def _():
        for hd, pt in zip(heads, parts):
            k_up = (pt["k"] * jnp.exp2(-pt["b_sub"])).astype(BF16)
            a_scr[hd] = _dot_nt(pt["q_sub"], k_up)

    @pl.when(steepest > A_SAFE_LOG2_DECAY)
    def _():
        for hd, pt in zip(heads, parts):
            a_scr[hd] = _hgrn_exact_sub_scores(pt["q"], pt["k"], pt["b_sub"], reduce_ref[...],
                                                k_scr.at[hd], bs_scr.at[hd], e_scr.at[hd], ts=ts)

    for hd, pt, ln in zip(heads, parts, lanes):
        o, st_ref[hd] = _hgrn_outputs(pt, v[:, ln], st_ref[hd], a_scr[hd], ts=ts)
        o_ref[0, 0, :, ln] = (_rms(o, gain_ref[...]) * out_gate[:, ln]).astype(BF16)


def _hgrn_decays(q, k, big_b, *, ts):
    C, c = A_CHUNK, A_SUB
    nc, dk = ts // C, q.shape[-1]
    b_last = big_b[ts - 1:ts]
    after = b_last - big_b

    def since_start_of(size):
        groups = [big_b[:size]] + [big_b[lo:lo + size] - big_b[lo - 1:lo]
                                   for lo in range(size, ts, size)]
        return jnp.concatenate(groups, axis=0)

    b_chunk = since_start_of(C)
    b_sub = since_start_of(c)
    to_chunk_end = jnp.concatenate(
        [big_b[lo + C - 1:lo + C] - big_b[lo:lo + C] for lo in range(0, ts, C)],
        axis=0).reshape(nc, C, dk)

    return dict(
        q=q, k=k, b_last=b_last, after=after, b_sub=b_sub, to_chunk_end=to_chunk_end,
        q_blk=(q * jnp.exp2(big_b)).astype(BF16),
        k_blk=(k * jnp.exp2(after)).astype(BF16),
        q_chunk=(q * jnp.exp2(b_chunk)).astype(BF16),
        q_sub=(q * jnp.exp2(b_sub)).astype(BF16))


def _hgrn_outputs(pt, v, st, a_rep, *, ts):
    C, c = A_CHUNK, A_SUB
    nc, nsub = ts // C, C // c
    k, after, to_chunk_end = pt["k"], pt["after"], pt["to_chunk_end"]
    q_chunk, q_sub = pt["q_chunk"], pt["q_sub"]
    dk = k.shape[-1]

    zeros2 = lambda n: jnp.zeros((n, dk), BF16)
    qs_parts, ks_parts = [], []
    for n in range(1, nc):
        lo, hi = n * C, (n + 1) * C
        decay = jnp.exp2(after[:lo] - after[lo - 1:lo])
        ks_parts.append(jnp.concatenate([(k[:lo] * decay).astype(BF16), zeros2(ts - lo)], axis=0))
        pieces = [zeros2(lo), q_chunk[lo:hi]] + ([zeros2(ts - hi)] if hi < ts else [])
        qs_parts.append(jnp.concatenate(pieces, axis=0))
    a_chunks = _dot_nt(jnp.concatenate(qs_parts, axis=1), jnp.concatenate(ks_parts, axis=1))

    k3 = k.reshape(nc, C, dk)
    qs3 = q_sub.reshape(nc, C, dk)
    zeros3 = lambda n: jnp.zeros((nc, n, dk), BF16)
    qs_parts, ks_parts = [], []
    for i in range(1, nsub):
        lo, hi = i * c, (i + 1) * c
        decay = jnp.exp2(to_chunk_end[:, :lo] - to_chunk_end[:, lo - 1:lo])
        ks_parts.append(jnp.concatenate([(k3[:, :lo] * decay).astype(BF16), zeros3(C - lo)], axis=1))
        pieces = [zeros3(lo), qs3[:, lo:hi]] + ([zeros3(C - hi)] if hi < C else [])
        qs_parts.append(jnp.concatenate(pieces, axis=1))
    a_subs = _dot_nt(jnp.concatenate(qs_parts, axis=2).reshape(ts, (nsub - 1) * dk),
                     jnp.concatenate(ks_parts, axis=2).reshape(ts, (nsub - 1) * dk))

    rr = lax.broadcasted_iota(jnp.int32, (ts, ts), 0)
    ll = lax.broadcasted_iota(jnp.int32, (ts, ts), 1)
    lc, ls = C.bit_length() - 1, c.bit_length() - 1
    same_chunk = (rr >> lc) == (ll >> lc)
    own_sub = ((rr >> ls) == (ll >> ls)) & ((ll & (c - 1)) <= (rr & (c - 1)))
    a = jnp.where(own_sub, a_rep, 0.0) + jnp.where(same_chunk, a_subs, 0.0) + a_chunks

    o = _dot(a.astype(BF16), v) + _dot_nt(pt["q_blk"], st.astype(BF16))
    return o, st * jnp.exp2(pt["b_last"]) + _dot_tn(v, pt["k_blk"])


def _hgrn_exact_sub_scores(q, k, b_sub, reduce_j, k_scr, bs_scr, e_scr, *, ts):
    c, dk = A_SUB, q.shape[-1]
    k_scr[...] = k
    bs_scr[...] = b_sub
    zero_tile = jnp.zeros((F32_SUBLANES, dk), F32)
    for n in range(ts // c):
        tiles = [slice(n * c + lo, n * c + lo + F32_SUBLANES) for lo in range(0, c, F32_SUBLANES)]
        for j in range(c):
            row = slice(n * c + j, n * c + j + 1)
            k_j, b_j = k_scr[row, :], bs_scr[row, :]
            e = [q[t] * k_j * jnp.exp2(jnp.minimum(b_sub[t] - b_j, 0.0))
                 if t.stop > n * c + j else zero_tile for t in tiles]
            e_scr[n * c:(n + 1) * c, j * dk:(j + 1) * dk] = jnp.concatenate(e, axis=0).astype(BF16)
    return _dot(e_scr[...], reduce_j)


def _hgrn_mixer_call(x, g, w_in, lb_logits, gain, *, layer, ts=256):
    bsz, seq, d = x.shape
    width = w_in.shape[1] // 4
    dk = gain.shape[-1]
    heads = width // dk
    prefix, reduce_j = _hgrn_rec_constants(ts, dk)
    kern = functools.partial(_hgrn_rec_kernel, ts=ts, layer=layer, q_scale=float(dk) ** -0.5)
    consts = (g, w_in, lb_logits, gain, prefix, reduce_j)
    return pl.pallas_call(
        kern, grid=(bsz, 1, seq // ts),
        in_specs=[pl.BlockSpec((1, ts, d), lambda b, h, s: (b, s, 0))]
        + [_const_spec(c.shape) for c in consts],
        out_specs=pl.BlockSpec((1, 1, ts, width), lambda b, h, s: (b, h, s, 0)),
        out_shape=jax.ShapeDtypeStruct((bsz, 1, seq, width), BF16),
        scratch_shapes=[pltpu.VMEM((heads, dk, dk), F32),
                        pltpu.VMEM((heads, ts, dk), F32),
                        pltpu.VMEM((heads, ts, dk), F32),
                        pltpu.VMEM((heads, ts, A_SUB * dk), BF16),
                        pltpu.VMEM((heads, ts, ts), F32)],
        compiler_params=_params(3), name="hgrn_mixer")(x, *consts)


def _rope_tables_lanes(pos_row, inv_freq_col, half):
    ang = inv_freq_col * pos_row.astype(F32)
    cos, sin = jnp.cos(ang), jnp.sin(ang)
    zeros = jnp.zeros((128 - 2 * half, ang.shape[1]), F32)
    c_tab = jnp.transpose(jnp.concatenate([cos, cos, zeros], axis=0))
    s_tab = jnp.transpose(jnp.concatenate([-sin, sin, zeros], axis=0))
    return c_tab, s_tab


def _kv_kernel(x_ref, pos_ref, g_ref, wdc_ref, wdr_ref, gl_ref, wk_ref, wvt_ref, freq_ref,
               k_out, vt_out, *, heads, nope, vdim, half):
    h = _rms(x_ref[0], g_ref[...]).astype(BF16)
    c_kv = _rms(_dot(h, wdc_ref[...]), gl_ref[...]).astype(BF16)
    kr = _dot(h, wdr_ref[...])
    c_tab, s_tab = _rope_tables_lanes(pos_ref[0], freq_ref[...], half)
    k_rope = (kr * c_tab + pltpu.roll(kr, half, 1) * s_tab).astype(BF16)
    k_nope = _dot(c_kv, wk_ref[...])
    v_t = _dot_nt(wvt_ref[...], c_kv)
    ones = jnp.ones((vt_out.shape[2] - vdim, v_t.shape[1]), BF16)
    for hd in range(heads):
        k_out[0, hd, :, 0:nope] = k_nope[:, hd * nope:(hd + 1) * nope].astype(BF16)
        k_out[0, hd, :, nope:] = k_rope
        vt_out[0, hd, 0:vdim, :] = v_t[hd * vdim:(hd + 1) * vdim].astype(BF16)
        vt_out[0, hd, vdim:, :] = ones


def _kv_call(x, pos_row, g, wd_c, wd_r, gl, wk, wvt, freq_col, *, heads, nope, vdim, half, tm=1024):
    bsz, seq, d = x.shape
    kern = functools.partial(_kv_kernel, heads=heads, nope=nope, vdim=vdim, half=half)
    kd = nope + 128
    vrows = vdim + BF16_SUBLANES
    return pl.pallas_call(
        kern, grid=(bsz, seq // tm),
        in_specs=[pl.BlockSpec((1, tm, d), lambda b, s: (b, s, 0)),
                  pl.BlockSpec((1, 1, tm), lambda b, s: (b, 0, s)),
                  _const_spec(g.shape), _const_spec(wd_c.shape), _const_spec(wd_r.shape),
                  _const_spec(gl.shape), _const_spec(wk.shape), _const_spec(wvt.shape),
                  _const_spec(freq_col.shape)],
        out_specs=[pl.BlockSpec((1, heads, tm, kd), lambda b, s: (b, 0, s, 0)),
                   pl.BlockSpec((1, heads, vrows, tm), lambda b, s: (b, 0, 0, s))],
        out_shape=[jax.ShapeDtypeStruct((bsz, heads, seq, kd), BF16),
                   jax.ShapeDtypeStruct((bsz, heads, vrows, seq), BF16)],
        compiler_params=_params(2), name="mla_shared_kv")(
            x, pos_row, g, wd_c, wd_r, gl, wk, wvt, freq_col)


def _q_kernel(x_ref, pos_ref, g_ref, wdq_ref, gq_ref, wuqt_ref, freq_ref, qt_out,
              *, heads, nope, half, qd, scale):
    h = _rms(x_ref[0], g_ref[...]).astype(BF16)
    c_q = (_rms(_dot(h, wdq_ref[...]), gq_ref[...]) * scale).astype(BF16)
    q_t = _dot_nt(wuqt_ref[...], c_q)
    ang = freq_ref[...] * pos_ref[0].astype(F32)
    cos, sin = jnp.cos(ang), jnp.sin(ang)
    zeros = jnp.zeros((qd - nope - 2 * half, q_t.shape[1]), F32)
    for hd in range(heads):
        base = hd * (nope + 2 * half)
        x1 = q_t[base + nope:base + nope + half]
        x2 = q_t[base + nope + half:base + nope + 2 * half]
        full = jnp.concatenate(
            [q_t[base:base + nope], x1 * cos - x2 * sin, x2 * cos + x1 * sin, zeros], axis=0)
        qt_out[0, hd] = full.astype(BF16)


def _q_call(x, pos_row, g, w_dq, gq, w_uqt, freq_col, *, heads, nope, half, qd, scale, tm=1024):
    bsz, seq, d = x.shape
    kern = functools.partial(_q_kernel, heads=heads, nope=nope, half=half, qd=qd, scale=scale)
    return pl.pallas_call(
        kern, grid=(bsz, seq // tm),
        in_specs=[pl.BlockSpec((1, tm, d), lambda b, s: (b, s, 0)),
                  pl.BlockSpec((1, 1, tm), lambda b, s: (b, 0, s)),
                  _const_spec(g.shape), _const_spec(w_dq.shape), _const_spec(gq.shape),
                  _const_spec(w_uqt.shape), _const_spec(freq_col.shape)],
        out_specs=pl.BlockSpec((1, heads, qd, tm), lambda b, s: (b, 0, 0, s)),
        out_shape=jax.ShapeDtypeStruct((bsz, heads, qd, seq), BF16),
        compiler_params=_params(2), name="mla_q")(x, pos_row, g, w_dq, gq, w_uqt, freq_col)


def _attn_kernel(qt_ref, k_ref, vt_ref, o_ref, s0, s1, p0, p1, acc_ref, *, tq, tk, dv):
    qi = pl.program_id(2)

    def key_block(j):
        return k_ref[0, 0, pl.ds(pl.multiple_of(j * tk, tk), tk), :]

    def value_block(j):
        return vt_ref[0, 0, :, pl.ds(pl.multiple_of(j * tk, tk), tk)]

    def scores(j, s_out):
        s = _dot(key_block(j), qt_ref[0, 0])
        s_out[...] = s
        return jnp.max(s, axis=0, keepdims=True)

    def probabilities(s, m, m_blk):
        m_new = jnp.maximum(m, m_blk)
        return m_new, jnp.exp2(m - m_new), jnp.exp2((s - m_new).astype(BF16))

    def softmax_step(s_in, p_out, m, m_blk):
        m_new, alpha, p = probabilities(s_in[...], m, m_blk)
        p_out[...] = p
        return m_new, alpha

    def accumulate(j, p_in, alpha):
        acc_ref[...] = alpha * acc_ref[...] + _dot(value_block(j), p_in[...])

    def body(j, carry):
        a_prev, m, mb0 = carry
        mb1 = scores(2 * j + 1, s1)
        m, a0 = softmax_step(s0, p0, m, mb0)
        accumulate(jnp.maximum(2 * j - 1, 0), p1, a_prev)
        mb0 = scores(2 * j + 2, s0)
        m, a1 = softmax_step(s1, p1, m, mb1)
        accumulate(2 * j, p0, a0)
        return a1, m, mb0

    n_diag = tq // tk
    n_full = n_diag * qi
    acc_ref[...] = jnp.zeros_like(acc_ref)
    p1[...] = jnp.zeros_like(p1)
    init = (jnp.ones((1, tq), F32), jnp.full((1, tq), NEG_BIG, F32), scores(0, s0))
    a_prev, m, _ = lax.fori_loop(0, (n_diag // 2) * qi, body, init)
    accumulate(jnp.maximum(n_full - 1, 0), p1, a_prev)

    causal = (lax.broadcasted_iota(jnp.int32, (tk, tk), 0)
              <= lax.broadcasted_iota(jnp.int32, (tk, tk), 1))
    for d in range(n_diag):
        lo = d * tk
        s = s0[...] if d == 0 else _dot(key_block(n_full + d), qt_ref[0, 0, :, lo:])
        masked = jnp.where(causal, s[:, :tk], NEG_BIG)
        s = jnp.concatenate([masked, s[:, tk:]], axis=1) if lo + tk < tq else masked
        m_d, a_d, p_d = probabilities(s, m[:, lo:], jnp.max(s, axis=0, keepdims=True))
        m = jnp.concatenate([m[:, :lo], m_d], axis=1) if lo else m_d
        acc_ref[:, lo:] = a_d * acc_ref[:, lo:] + _dot(value_block(n_full + d), p_d)
    acc = acc_ref[...]
    o_ref[0, 0] = (acc[:dv] / acc[dv:dv + 1]).astype(BF16)


def _attn_call(qt, k, vt, *, tq=2048, tk=512):
    assert (tq // tk) % 2 == 0
    bsz, heads, qd, seq = qt.shape
    vrows = vt.shape[2]
    dv = vrows - BF16_SUBLANES
    kern = functools.partial(_attn_kernel, tq=tq, tk=tk, dv=dv)
    return pl.pallas_call(
        kern, grid=(bsz, heads, seq // tq),
        in_specs=[pl.BlockSpec((1, 1, qd, tq), lambda b, h, i: (b, h, 0, i)),
                  pl.BlockSpec((1, 1, seq, k.shape[3]), lambda b, h, i: (b, h, 0, 0)),
                  pl.BlockSpec((1, 1, vrows, seq), lambda b, h, i: (b, h, 0, 0))],
        out_specs=pl.BlockSpec((1, 1, dv, tq), lambda b, h, i: (b, h, 0, i)),
        out_shape=jax.ShapeDtypeStruct((bsz, heads, dv, seq), BF16),
        scratch_shapes=[pltpu.VMEM((tk, tq), F32), pltpu.VMEM((tk, tq), F32),
                        pltpu.VMEM((tk, tq), BF16), pltpu.VMEM((tk, tq), BF16),
                        pltpu.VMEM((vrows, tq), F32)],
        compiler_params=_params(3), name="mla_attention")(qt, k, vt)


def kernel(x, p, positions, norm_gains, ffn_w_in, ffn_w_out, ple_w_gate, ple_w_in, a_w_in,
           a_lb_logits, a_out_gain, a_w_out, kv_norm_in, kv_w_down, kv_latent_norm, kv_w_up,
           b_w_dq, b_q_norm, b_w_uq, b_w_out, final_norm):
    bsz, seq, d = x.shape
    depth = norm_gains.shape[0]
    n_a = a_w_in.shape[0]
    t = bsz * seq
    bf = lambda w: w.astype(BF16)
    row = lambda g: g.reshape(1, -1).astype(F32)

    kv_lora = kv_latent_norm.shape[0]
    rope = kv_w_down.shape[1] - kv_lora
    half = rope // 2
    vdim = 128
    heads = b_w_out.shape[1] // vdim
    nope = kv_w_up.shape[1] // heads - vdim
    qd = 256
    assert nope == 128 and rope == 64 and b_w_uq.shape[2] == heads * (nope + rope)

    inv_freq = 1.0 / (ROPE_THETA ** (jnp.arange(0, rope, 2, dtype=F32) / rope))
    freq_col = inv_freq.reshape(half, 1)
    pos_row = positions.reshape(bsz, 1, seq)

    wd_c = bf(kv_w_down[:, :kv_lora])
    wd_r = bf(jnp.concatenate([kv_w_down[:, kv_lora:]] * (128 // rope), axis=1))
    w_up = kv_w_up.reshape(kv_lora, heads, nope + vdim)
    wk = bf(w_up[:, :, :nope].reshape(kv_lora, heads * nope))
    wvt = bf(w_up[:, :, nope:].reshape(kv_lora, heads * vdim).T)

    def shared_kv(xs):
        return _kv_call(xs.reshape(bsz, seq, d), pos_row, row(kv_norm_in), wd_c, wd_r,
                        row(kv_latent_norm), wk, wvt, freq_col,
                        heads=heads, nope=nope, vdim=vdim, half=half)

    ffn_in, ffn_out = bf(ffn_w_in), bf(ffn_w_out)
    ple_gate, ple_in = bf(ple_w_gate), bf(ple_w_in)
    gains = norm_gains.astype(F32).reshape(depth, norm_gains.shape[1], 1, d)
    p_tokens = p.reshape(depth, t, -1)

    xf = x.reshape(t, d)
    shared = shared_kv(xf) if n_a == 0 else None
    attn_scale = float(nope + rope) ** -0.5 * LOG2_E
    for li in range(depth):
        x1 = _block_call(xf, (gains, (li, 0)), (ffn_in, (li, 0)), (ffn_out, (li, 0)))
        if li < n_a:
            o = _hgrn_mixer_call(x1.reshape(bsz, seq, d), row(norm_gains[li, 1]), bf(a_w_in[li]),
                                 a_lb_logits.astype(F32), row(a_out_gain[li]), layer=li)
            pre = (o, bf(a_w_out[li]), "groups")
        else:
            bi = li - n_a
            w_uqt = bf(b_w_uq[bi].T)
            qt = _q_call(x1.reshape(bsz, seq, d), pos_row, row(norm_gains[li, 1]),
                         bf(b_w_dq[bi]), row(b_q_norm[bi]), w_uqt, freq_col, heads=heads,
                         nope=nope, half=half, qd=qd, scale=attn_scale)
            k_all, vt_all = shared
            o = _attn_call(qt, k_all, vt_all)
            pre = (o, bf(b_w_out[bi]), "transposed")
        ple = ((gains, (li, 3)), (ple_gate, (li,)), (p_tokens, li), (ple_in, (li,)))
        final = row(final_norm) if li == depth - 1 else None
        xf = _block_call(x1, (gains, (li, 2)), (ffn_in, (li, 1)), (ffn_out, (li, 1)),
                         pre=pre, ple=ple, final=final)
        if li == n_a - 1:
            shared = shared_kv(xf)
    return xf.reshape(bsz, seq, d)
```

```python
import functools
import math

import jax
import jax.numpy as jnp
from jax import lax
from jax.experimental import pallas as pl
from jax.experimental.pallas import tpu as pltpu

F32 = jnp.float32
BF16 = jnp.bfloat16

EPS = 1e-6
ROPE_THETA = 10000.0
A_CHUNK = 64
A_SUB = 16
A_SAFE_LOG2_DECAY = 96.0
NEG_BIG = -1e30
LOG2_E = math.log2(math.e)

V7X_VMEM_BYTES = 64 * 1024 * 1024
VMEM_LIMIT = V7X_VMEM_BYTES * 7 // 8
F32_SUBLANES = 8
BF16_SUBLANES = 16


def _rms(x, g):
    ms = jnp.mean(x * x, axis=-1, keepdims=True)
    return x * lax.rsqrt(ms + EPS) * g


def _sigmoid(x):
    return 1.0 / (1.0 + jnp.exp(-x))


def _dot(a, b):
    return jnp.dot(a, b, preferred_element_type=F32)


def _dot_nt(a, b):
    return lax.dot_general(a, b, (((1,), (1,)), ((), ())), preferred_element_type=F32)


def _dot_tn(a, b):
    return lax.dot_general(a, b, (((0,), (0,)), ((), ())), preferred_element_type=F32)


def _const_spec(shape):
    nd = len(shape)
    return pl.BlockSpec(shape, lambda *_: (0,) * nd, pipeline_mode=pl.Buffered(1))


def _pick(w):
    arr, lead = w if isinstance(w, tuple) else (w, ())
    rest = arr.shape[len(lead):]
    index = tuple(lead) + (0,) * len(rest)
    spec = pl.BlockSpec((None,) * len(lead) + rest, lambda *_: index,
                        pipeline_mode=pl.Buffered(1))
    return arr, spec


def _params(n_axes):
    return pltpu.CompilerParams(
        dimension_semantics=("arbitrary",) * n_axes, vmem_limit_bytes=VMEM_LIMIT)


def _block_kernel(*refs, d_ff, pre_layout, has_ple, has_final):
    it = iter(refs)
    x_ref = next(it)
    if pre_layout:
        y_ref, wpre_ref = next(it), next(it)
    g_ref, win_ref, wout_ref = next(it), next(it), next(it)
    if has_ple:
        gp_ref, wg_ref, p_ref, wp_ref = next(it), next(it), next(it), next(it)
    if has_final:
        gf_ref = next(it)
    o_ref = next(it)

    x = x_ref[...]
    if pre_layout == "groups":
        y = jnp.concatenate([y_ref[0, gi] for gi in range(y_ref.shape[1])], axis=1)
        x = x + _dot(y, wpre_ref[...])
    elif pre_layout == "transposed":
        n_g, gw, tm = y_ref.shape[1:]
        x = x + _dot_tn(y_ref[0].reshape(n_g * gw, tm), wpre_ref[...])
    h = _rms(x, g_ref[...]).astype(BF16)
    gate = _dot(h, win_ref[:, :d_ff])
    up = _dot(h, win_ref[:, d_ff:])
    act = (gate * _sigmoid(gate) * up).astype(BF16)
    x = x + 0.5 * _dot(act, wout_ref[...])
    if has_ple:
        hg = _rms(x, gp_ref[...]).astype(BF16)
        emb_gate = _sigmoid(_dot(hg, wg_ref[...]))
        emb = _dot(p_ref[...].astype(BF16), wp_ref[...])
        x = x + emb_gate * emb
    if has_final:
        x = _rms(x, gf_ref[...])
    o_ref[...] = x


def _block_call(x, g, w_in, w_out, *, pre=None, ple=None, final=None, tm=512):
    t, d = x.shape
    row = lambda i: (i, 0)
    args, specs = [x], [pl.BlockSpec((tm, d), row)]

    def add(w):
        arr, spec = _pick(w)
        args.append(arr)
        specs.append(spec)
        return spec.block_shape

    pre_layout = None
    if pre is not None:
        y, w_pre, pre_layout = pre
        args.append(y)
        steps = (t // y.shape[0]) // tm
        if pre_layout == "groups":
            specs.append(pl.BlockSpec((1, y.shape[1], tm, y.shape[3]),
                                      lambda i: (i // steps, 0, i % steps, 0)))
        else:
            specs.append(pl.BlockSpec((1, y.shape[1], y.shape[2], tm),
                                      lambda i: (i // steps, 0, 0, i % steps)))
        add(w_pre)
    add(g)
    add(w_in)
    d_ff = add(w_out)[-2]
    if ple is not None:
        gp, wg, p, wp = ple
        add(gp)
        add(wg)
        args.append(p[0])
        specs.append(pl.BlockSpec((None, tm, p[0].shape[2]), lambda i: (p[1], i, 0)))
        add(wp)
    if final is not None:
        add(final)
    kern = functools.partial(_block_kernel, d_ff=d_ff, pre_layout=pre_layout,
                             has_ple=ple is not None, has_final=final is not None)
    return pl.pallas_call(
        kern, grid=(t // tm,), in_specs=specs, out_specs=pl.BlockSpec((tm, d), row),
        out_shape=jax.ShapeDtypeStruct((t, d), F32), compiler_params=_params(1),
        name="token_block")(*args)


def _hgrn_rec_constants(ts, dk):
    incl = jnp.arange(ts)[None, :] <= jnp.arange(ts)[:, None]
    prefix = jnp.concatenate([incl, incl, incl], axis=1).astype(BF16)
    col = jnp.arange(ts)[None, :]
    reduce_j = (jnp.arange(A_SUB * dk)[:, None] // dk) == (col % A_SUB)
    return prefix, reduce_j.astype(BF16)


def _hgrn_rec_kernel(x_ref, g_ref, w_ref, lbl_ref, gain_ref, prefix_ref, reduce_ref,
                     o_ref, st_ref, k_scr, bs_scr, e_scr, a_scr, *, ts, layer, q_scale):
    @pl.when(pl.program_id(2) == 0)
    def _():
        st_ref[...] = jnp.zeros_like(st_ref)

    dk = gain_ref.shape[-1]
    heads = range(st_ref.shape[0])
    lanes = [slice(hd * dk, (hd + 1) * dk) for hd in heads]
    width = len(heads) * dk

    h = _rms(x_ref[0], g_ref[...]).astype(BF16)
    q = _dot(h, w_ref[:, 0 * width:1 * width]) * q_scale
    f = _dot(h, w_ref[:, 1 * width:2 * width])
    v = _dot(h, w_ref[:, 2 * width:3 * width]).astype(BF16)
    out_gate = _sigmoid(_dot(h, w_ref[:, 3 * width:4 * width]))
    lbl = lbl_ref[...]
    e = jnp.exp(lbl - jnp.max(lbl, axis=0, keepdims=True))
    sm = e / jnp.sum(e, axis=0, keepdims=True)
    lb = jnp.sum(sm[:layer + 1], axis=0, keepdims=True)
    k = (1.0 - lb) * _sigmoid(-f)
    g = jnp.log(lb + (1.0 - lb) * _sigmoid(f)) * LOG2_E

    g1 = g.astype(BF16)
    r1 = g - g1.astype(F32)
    g2 = r1.astype(BF16)
    g3 = (r1 - g2.astype(F32)).astype(BF16)
    big_b = _dot(prefix_ref[...], jnp.concatenate([g1, g2, g3], axis=0))
    parts = [_hgrn_decays(q[:, ln], k[:, ln], big_b[:, ln], ts=ts) for ln in lanes]
    steepest = functools.reduce(jnp.maximum, [jnp.max(-pt["b_sub"]) for pt in parts])

    @pl.when(steepest <= A_SAFE_LOG2_DECAY)
    def _():
        for hd, pt in zip(heads, parts):
            k_up = (pt["k"] * jnp.exp2(-pt["b_sub"])).astype(BF16)
            a_scr[hd] = _dot_nt(pt["q_sub"], k_up)

    @pl.when(steepest > A_SAFE_LOG2_DECAY)
    def _():
        for hd, pt in zip(heads, parts):
            a_scr[hd] = _hgrn_exact_sub_scores(pt["q"], pt["k"], pt["b_sub"], reduce_ref[...],
                                                k_scr.at[hd], bs_scr.at[hd], e_scr.at[hd], ts=ts)

    for hd, pt, ln in zip(heads, parts, lanes):
        o, st_ref[hd] = _hgrn_outputs(pt, v[:, ln], st_ref[hd], a_scr[hd], ts=ts)
        o_ref[0, 0, :, ln] = (_rms(o, gain_ref[...]) * out_gate[:, ln]).astype(BF16)


def _hgrn_decays(q, k, big_b, *, ts):
    C, c = A_CHUNK, A_SUB
    nc, dk = ts // C, q.shape[-1]
    b_last = big_b[ts - 1:ts]
    after = b_last - big_b

    def since_start_of(size):
        groups = [big_b[:size]] + [big_b[lo:lo + size] - big_b[lo - 1:lo]
                                   for lo in range(size, ts, size)]
        return jnp.concatenate(groups, axis=0)

    b_chunk = since_start_of(C)
    b_sub = since_start_of(c)
    to_chunk_end = jnp.concatenate(
        [big_b[lo + C - 1:lo + C] - big_b[lo:lo + C] for lo in range(0, ts, C)],
        axis=0).reshape(nc, C, dk)

    return dict(
        q=q, k=k, b_last=b_last, after=after, b_sub=b_sub, to_chunk_end=to_chunk_end,
        q_blk=(q * jnp.exp2(big_b)).astype(BF16),
        k_blk=(k * jnp.exp2(after)).astype(BF16),
        q_chunk=(q * jnp.exp2(b_chunk)).astype(BF16),
        q_sub=(q * jnp.exp2(b_sub)).astype(BF16))


def _hgrn_outputs(pt, v, st, a_rep, *, ts):
    C, c = A_CHUNK, A_SUB
    nc, nsub = ts // C, C // c
    k, after, to_chunk_end = pt["k"], pt["after"], pt["to_chunk_end"]
    q_chunk, q_sub = pt["q_chunk"], pt["q_sub"]
    dk = k.shape[-1]

    zeros2 = lambda n: jnp.zeros((n, dk), BF16)
    qs_parts, ks_parts = [], []
    for n in range(1, nc):
        lo, hi = n * C, (n + 1) * C
        decay = jnp.exp2(after[:lo] - after[lo - 1:lo])
        ks_parts.append(jnp.concatenate([(k[:lo] * decay).astype(BF16), zeros2(ts - lo)], axis=0))
        pieces = [zeros2(lo), q_chunk[lo:hi]] + ([zeros2(ts - hi)] if hi < ts else [])
        qs_parts.append(jnp.concatenate(pieces, axis=0))
    a_chunks = _dot_nt(jnp.concatenate(qs_parts, axis=1), jnp.concatenate(ks_parts, axis=1))

    k3 = k.reshape(nc, C, dk)
    qs3 = q_sub.reshape(nc, C, dk)
    zeros3 = lambda n: jnp.zeros((nc, n, dk), BF16)
    qs_parts, ks_parts = [], []
    for i in range(1, nsub):
        lo, hi = i * c, (i + 1) * c
        decay = jnp.exp2(to_chunk_end[:, :lo] - to_chunk_end[:, lo - 1:lo])
        ks_parts.append(jnp.concatenate([(k3[:, :lo] * decay).astype(BF16), zeros3(C - lo)], axis=1))
        pieces = [zeros3(lo), qs3[:, lo:hi]] + ([zeros3(C - hi)] if hi < C else [])
        qs_parts.append(jnp.concatenate(pieces, axis=1))
    a_subs = _dot_nt(jnp.concatenate(qs_parts, axis=2).reshape(ts, (nsub - 1) * dk),
                     jnp.concatenate(ks_parts, axis=2).reshape(ts, (nsub - 1) * dk))

    rr = lax.broadcasted_iota(jnp.int32, (ts, ts), 0)
    ll = lax.broadcasted_iota(jnp.int32, (ts, ts), 1)
    lc, ls = C.bit_length() - 1, c.bit_length() - 1
    same_chunk = (rr >> lc) == (ll >> lc)
    own_sub = ((rr >> ls) == (ll >> ls)) & ((ll & (c - 1)) <= (rr & (c - 1)))
    a = jnp.where(own_sub, a_rep, 0.0) + jnp.where(same_chunk, a_subs, 0.0) + a_chunks

    o = _dot(a.astype(BF16), v) + _dot_nt(pt["q_blk"], st.astype(BF16))
    return o, st * jnp.exp2(pt["b_last"]) + _dot_tn(v, pt["k_blk"])


def _hgrn_exact_sub_scores(q, k, b_sub, reduce_j, k_scr, bs_scr, e_scr, *, ts):
    c, dk = A_SUB, q.shape[-1]
    k_scr[...] = k
    bs_scr[...] = b_sub
    zero_tile = jnp.zeros((F32_SUBLANES, dk), F32)
    for n in range(ts // c):
        tiles = [slice(n * c + lo, n * c + lo + F32_SUBLANES) for lo in range(0, c, F32_SUBLANES)]
        for j in range(c):
            row = slice(n * c + j, n * c + j + 1)
            k_j, b_j = k_scr[row, :], bs_scr[row, :]
            e = [q[t] * k_j * jnp.exp2(jnp.minimum(b_sub[t] - b_j, 0.0))
                 if t.stop > n * c + j else zero_tile for t in tiles]
            e_scr[n * c:(n + 1) * c, j * dk:(j + 1) * dk] = jnp.concatenate(e, axis=0).astype(BF16)
    return _dot(e_scr[...], reduce_j)


def _hgrn_mixer_call(x, g, w_in, lb_logits, gain, *, layer, ts=256):
    bsz, seq, d = x.shape
    width = w_in.shape[1] // 4
    dk = gain.shape[-1]
    heads = width // dk
    prefix, reduce_j = _hgrn_rec_constants(ts, dk)
    kern = functools.partial(_hgrn_rec_kernel, ts=ts, layer=layer, q_scale=float(dk) ** -0.5)
    consts = (g, w_in, lb_logits, gain, prefix, reduce_j)
    return pl.pallas_call(
        kern, grid=(bsz, 1, seq // ts),
        in_specs=[pl.BlockSpec((1, ts, d), lambda b, h, s: (b, s, 0))]
        + [_const_spec(c.shape) for c in consts],
        out_specs=pl.BlockSpec((1, 1, ts, width), lambda b, h, s: (b, h, s, 0)),
        out_shape=jax.ShapeDtypeStruct((bsz, 1, seq, width), BF16),
        scratch_shapes=[pltpu.VMEM((heads, dk, dk), F32),
                        pltpu.VMEM((heads, ts, dk), F32),
                        pltpu.VMEM((heads, ts, dk), F32),
                        pltpu.VMEM((heads, ts, A_SUB * dk), BF16),
                        pltpu.VMEM((heads, ts, ts), F32)],
        compiler_params=_params(3), name="hgrn_mixer")(x, *consts)


def _rope_tables_lanes(pos_row, inv_freq_col, half):
    ang = inv_freq_col * pos_row.astype(F32)
    cos, sin = jnp.cos(ang), jnp.sin(ang)
    zeros = jnp.zeros((128 - 2 * half, ang.shape[1]), F32)
    c_tab = jnp.transpose(jnp.concatenate([cos, cos, zeros], axis=0))
    s_tab = jnp.transpose(jnp.concatenate([-sin, sin, zeros], axis=0))
    return c_tab, s_tab


def _kv_kernel(x_ref, pos_ref, g_ref, wdc_ref, wdr_ref, gl_ref, wk_ref, wvt_ref, freq_ref,
               k_out, vt_out, *, heads, nope, vdim, half):
    h = _rms(x_ref[0], g_ref[...]).astype(BF16)
    c_kv = _rms(_dot(h, wdc_ref[...]), gl_ref[...]).astype(BF16)
    kr = _dot(h, wdr_ref[...])
    c_tab, s_tab = _rope_tables_lanes(pos_ref[0], freq_ref[...], half)
    k_rope = (kr * c_tab + pltpu.roll(kr, half, 1) * s_tab).astype(BF16)
    k_nope = _dot(c_kv, wk_ref[...])
    v_t = _dot_nt(wvt_ref[...], c_kv)
    ones = jnp.ones((vt_out.shape[2] - vdim, v_t.shape[1]), BF16)
    for hd in range(heads):
        k_out[0, hd, :, 0:nope] = k_nope[:, hd * nope:(hd + 1) * nope].astype(BF16)
        k_out[0, hd, :, nope:] = k_rope
        vt_out[0, hd, 0:vdim, :] = v_t[hd * vdim:(hd + 1) * vdim].astype(BF16)
        vt_out[0, hd, vdim:, :] = ones


def _kv_call(x, pos_row, g, wd_c, wd_r, gl, wk, wvt, freq_col, *, heads, nope, vdim, half, tm=1024):
    bsz, seq, d = x.shape
    kern = functools.partial(_kv_kernel, heads=heads, nope=nope, vdim=vdim, half=half)
    kd = nope + 128
    vrows = vdim + BF16_SUBLANES
    return pl.pallas_call(
        kern, grid=(bsz, seq // tm),
        in_specs=[pl.BlockSpec((1, tm, d), lambda b, s: (b, s, 0)),
                  pl.BlockSpec((1, 1, tm), lambda b, s: (b, 0, s)),
                  _const_spec(g.shape), _const_spec(wd_c.shape), _const_spec(wd_r.shape),
                  _const_spec(gl.shape), _const_spec(wk.shape), _const_spec(wvt.shape),
                  _const_spec(freq_col.shape)],
        out_specs=[pl.BlockSpec((1, heads, tm, kd), lambda b, s: (b, 0, s, 0)),
                   pl.BlockSpec((1, heads, vrows, tm), lambda b, s: (b, 0, 0, s))],
        out_shape=[jax.ShapeDtypeStruct((bsz, heads, seq, kd), BF16),
                   jax.ShapeDtypeStruct((bsz, heads, vrows, seq), BF16)],
        compiler_params=_params(2), name="mla_shared_kv")(
            x, pos_row, g, wd_c, wd_r, gl, wk, wvt, freq_col)


def _q_kernel(x_ref, pos_ref, g_ref, wdq_ref, gq_ref, wuqt_ref, freq_ref, qt_out,
              *, heads, nope, half, qd, scale):
    h = _rms(x_ref[0], g_ref[...]).astype(BF16)
    c_q = (_rms(_dot(h, wdq_ref[...]), gq_ref[...]) * scale).astype(BF16)
    q_t = _dot_nt(wuqt_ref[...], c_q)
    ang = freq_ref[...] * pos_ref[0].astype(F32)
    cos, sin = jnp.cos(ang), jnp.sin(ang)
    zeros = jnp.zeros((qd - nope - 2 * half, q_t.shape[1]), F32)
    for hd in range(heads):
        base = hd * (nope + 2 * half)
        x1 = q_t[base + nope:base + nope + half]
        x2 = q_t[base + nope + half:base + nope + 2 * half]
        full = jnp.concatenate(
            [q_t[base:base + nope], x1 * cos - x2 * sin, x2 * cos + x1 * sin, zeros], axis=0)
        qt_out[0, hd] = full.astype(BF16)


def _q_call(x, pos_row, g, w_dq, gq, w_uqt, freq_col, *, heads, nope, half, qd, scale, tm=1024):
    bsz, seq, d = x.shape
    kern = functools.partial(_q_kernel, heads=heads, nope=nope, half=half, qd=qd, scale=scale)
    return pl.pallas_call(
        kern, grid=(bsz, seq // tm),
        in_specs=[pl.BlockSpec((1, tm, d), lambda b, s: (b, s, 0)),
                  pl.BlockSpec((1, 1, tm), lambda b, s: (b, 0, s)),
                  _const_spec(g.shape), _const_spec(w_dq.shape), _const_spec(gq.shape),
                  _const_spec(w_uqt.shape), _const_spec(freq_col.shape)],
        out_specs=pl.BlockSpec((1, heads, qd, tm), lambda b, s: (b, 0, 0, s)),
        out_shape=jax.ShapeDtypeStruct((bsz, heads, qd, seq), BF16),
        compiler_params=_params(2), name="mla_q")(x, pos_row, g, w_dq, gq, w_uqt, freq_col)


def _attn_kernel(qt_ref, k_ref, vt_ref, o_ref, s0, s1, p0, p1, acc_ref, *, tq, tk, dv):
    for hd in range(qt_ref.shape[1]):
        _attn_head(qt_ref.at[0, hd], k_ref.at[0, hd], vt_ref.at[0, hd], o_ref.at[0, hd],
                   s0, s1, p0, p1, acc_ref, pl.program_id(2), tq=tq, tk=tk, dv=dv)


def _attn_head(qt_ref, k_ref, vt_ref, o_ref, s0, s1, p0, p1, acc_ref, qi, *, tq, tk, dv):

    def key_block(j):
        return k_ref[pl.ds(pl.multiple_of(j * tk, tk), tk), :]

    def value_block(j):
        return vt_ref[:, pl.ds(pl.multiple_of(j * tk, tk), tk)]

    def scores(j, s_out):
        s = _dot(key_block(j), qt_ref[...])
        s_out[...] = s
        return jnp.max(s, axis=0, keepdims=True)

    def probabilities(s, m, m_blk):
        m_new = jnp.maximum(m, m_blk)
        return m_new, jnp.exp2(m - m_new), jnp.exp2((s - m_new).astype(BF16))

    def softmax_step(s_in, p_out, m, m_blk):
        m_new, alpha, p = probabilities(s_in[...], m, m_blk)
        p_out[...] = p
        return m_new, alpha

    def accumulate(j, p_in, alpha):
        acc_ref[...] = alpha * acc_ref[...] + _dot(value_block(j), p_in[...])

    def body(j, carry):
        a_prev, m, mb0 = carry
        mb1 = scores(2 * j + 1, s1)
        m, a0 = softmax_step(s0, p0, m, mb0)
        accumulate(jnp.maximum(2 * j - 1, 0), p1, a_prev)
        mb0 = scores(2 * j + 2, s0)
        m, a1 = softmax_step(s1, p1, m, mb1)
        accumulate(2 * j, p0, a0)
        return a1, m, mb0

    n_diag = tq // tk
    n_full = n_diag * qi
    acc_ref[...] = jnp.zeros_like(acc_ref)
    p1[...] = jnp.zeros_like(p1)
    init = (jnp.ones((1, tq), F32), jnp.full((1, tq), NEG_BIG, F32), scores(0, s0))
    a_prev, m, _ = lax.fori_loop(0, (n_diag // 2) * qi, body, init)
    accumulate(jnp.maximum(n_full - 1, 0), p1, a_prev)

    causal = (lax.broadcasted_iota(jnp.int32, (tk, tk), 0)
              <= lax.broadcasted_iota(jnp.int32, (tk, tk), 1))
    for d in range(n_diag):
        lo = d * tk
        s = s0[...] if d == 0 else _dot(key_block(n_full + d), qt_ref[:, lo:])
        masked = jnp.where(causal, s[:, :tk], NEG_BIG)
        s = jnp.concatenate([masked, s[:, tk:]], axis=1) if lo + tk < tq else masked
        m_d, a_d, p_d = probabilities(s, m[:, lo:], jnp.max(s, axis=0, keepdims=True))
        m = jnp.concatenate([m[:, :lo], m_d], axis=1) if lo else m_d
        acc_ref[:, lo:] = a_d * acc_ref[:, lo:] + _dot(value_block(n_full + d), p_d)
    acc = acc_ref[...]
    o_ref[...] = (acc[:dv] / acc[dv:dv + 1]).astype(BF16)


def _attn_call(qt, k, vt, *, tq=2048, tk=512, hps=2):
    assert (tq // tk) % 2 == 0
    bsz, heads, qd, seq = qt.shape
    vrows = vt.shape[2]
    dv = vrows - BF16_SUBLANES
    kern = functools.partial(_attn_kernel, tq=tq, tk=tk, dv=dv)
    return pl.pallas_call(
        kern, grid=(bsz, heads // hps, seq // tq),
        in_specs=[pl.BlockSpec((1, hps, qd, tq), lambda b, h, i: (b, h, 0, i)),
                  pl.BlockSpec((1, hps, seq, k.shape[3]), lambda b, h, i: (b, h, 0, 0)),
                  pl.BlockSpec((1, hps, vrows, seq), lambda b, h, i: (b, h, 0, 0))],
        out_specs=pl.BlockSpec((1, hps, dv, tq), lambda b, h, i: (b, h, 0, i)),
        out_shape=jax.ShapeDtypeStruct((bsz, heads, dv, seq), BF16),
        scratch_shapes=[pltpu.VMEM((tk, tq), F32), pltpu.VMEM((tk, tq), F32),
                        pltpu.VMEM((tk, tq), BF16), pltpu.VMEM((tk, tq), BF16),
                        pltpu.VMEM((vrows, tq), F32)],
        compiler_params=_params(3), name="mla_attention")(qt, k, vt)


def kernel(x, p, positions, norm_gains, ffn_w_in, ffn_w_out, ple_w_gate, ple_w_in, a_w_in,
           a_lb_logits, a_out_gain, a_w_out, kv_norm_in, kv_w_down, kv_latent_norm, kv_w_up,
           b_w_dq, b_q_norm, b_w_uq, b_w_out, final_norm):
    bsz, seq, d = x.shape
    depth = norm_gains.shape[0]
    n_a = a_w_in.shape[0]
    t = bsz * seq
    bf = lambda w: w.astype(BF16)
    row = lambda g: g.reshape(1, -1).astype(F32)

    kv_lora = kv_latent_norm.shape[0]
    rope = kv_w_down.shape[1] - kv_lora
    half = rope // 2
    vdim = 128
    heads = b_w_out.shape[1] // vdim
    nope = kv_w_up.shape[1] // heads - vdim
    qd = 256
    assert nope == 128 and rope == 64 and b_w_uq.shape[2] == heads * (nope + rope)

    inv_freq = 1.0 / (ROPE_THETA ** (jnp.arange(0, rope, 2, dtype=F32) / rope))
    freq_col = inv_freq.reshape(half, 1)
    pos_row = positions.reshape(bsz, 1, seq)

    wd_c = bf(kv_w_down[:, :kv_lora])
    wd_r = bf(jnp.concatenate([kv_w_down[:, kv_lora:]] * (128 // rope), axis=1))
    w_up = kv_w_up.reshape(kv_lora, heads, nope + vdim)
    wk = bf(w_up[:, :, :nope].reshape(kv_lora, heads * nope))
    wvt = bf(w_up[:, :, nope:].reshape(kv_lora, heads * vdim).T)

    def shared_kv(xs):
        return _kv_call(xs.reshape(bsz, seq, d), pos_row, row(kv_norm_in), wd_c, wd_r,
                        row(kv_latent_norm), wk, wvt, freq_col,
                        heads=heads, nope=nope, vdim=vdim, half=half)

    ffn_in, ffn_out = bf(ffn_w_in), bf(ffn_w_out)
    ple_gate, ple_in = bf(ple_w_gate), bf(ple_w_in)
    gains = norm_gains.astype(F32).reshape(depth, norm_gains.shape[1], 1, d)
    p_tokens = p.reshape(depth, t, -1)

    xf = x.reshape(t, d)
    shared = shared_kv(xf) if n_a == 0 else None
    attn_scale = float(nope + rope) ** -0.5 * LOG2_E
    for li in range(depth):
        x1 = _block_call(xf, (gains, (li, 0)), (ffn_in, (li, 0)), (ffn_out, (li, 0)))
        if li < n_a:
            o = _hgrn_mixer_call(x1.reshape(bsz, seq, d), row(norm_gains[li, 1]), bf(a_w_in[li]),
                                 a_lb_logits.astype(F32), row(a_out_gain[li]), layer=li)
            pre = (o, bf(a_w_out[li]), "groups")
        else:
            bi = li - n_a
            w_uqt = bf(b_w_uq[bi].T)
            qt = _q_call(x1.reshape(bsz, seq, d), pos_row, row(norm_gains[li, 1]),
                         bf(b_w_dq[bi]), row(b_q_norm[bi]), w_uqt, freq_col, heads=heads,
                         nope=nope, half=half, qd=qd, scale=attn_scale)
            k_all, vt_all = shared
            o = _attn_call(qt, k_all, vt_all)
            pre = (o, bf(b_w_out[bi]), "transposed")
        ple = ((gains, (li, 3)), (ple_gate, (li,)), (p_tokens, li), (ple_in, (li,)))
        final = row(final_norm) if li == depth - 1 else None
        xf = _block_call(x1, (gains, (li, 2)), (ffn_in, (li, 1)), (ffn_out, (li, 1)),
                         pre=pre, ple=ple, final=final)
        if li == n_a - 1:
            shared = shared_kv(xf)
    return xf.reshape(bsz, seq, d)
```

```python
import functools
import math

import jax
import jax.numpy as jnp
from jax import lax
from jax.experimental import pallas as pl
from jax.experimental.pallas import tpu as pltpu

F32 = jnp.float32
BF16 = jnp.bfloat16

EPS = 1e-6
ROPE_THETA = 10000.0
A_CHUNK = 64
A_SUB = 16
A_SAFE_LOG2_DECAY = 96.0
NEG_BIG = -1e30
LOG2_E = math.log2(math.e)

V7X_VMEM_BYTES = 64 * 1024 * 1024
VMEM_LIMIT = V7X_VMEM_BYTES * 7 // 8
F32_SUBLANES = 8
BF16_SUBLANES = 16


def _rms(x, g):
    ms = jnp.mean(x * x, axis=-1, keepdims=True)
    return x * lax.rsqrt(ms + EPS) * g


def _sigmoid(x):
    return 1.0 / (1.0 + jnp.exp(-x))


def _dot(a, b):
    return jnp.dot(a, b, preferred_element_type=F32)


def _dot_nt(a, b):
    return lax.dot_general(a, b, (((1,), (1,)), ((), ())), preferred_element_type=F32)


def _dot_tn(a, b):
    return lax.dot_general(a, b, (((0,), (0,)), ((), ())), preferred_element_type=F32)


def _const_spec(shape):
    nd = len(shape)
    return pl.BlockSpec(shape, lambda *_: (0,) * nd, pipeline_mode=pl.Buffered(1))


def _pick(w):
    arr, lead = w if isinstance(w, tuple) else (w, ())
    rest = arr.shape[len(lead):]
    index = tuple(lead) + (0,) * len(rest)
    spec = pl.BlockSpec((None,) * len(lead) + rest, lambda *_: index,
                        pipeline_mode=pl.Buffered(1))
    return arr, spec


def _params(n_axes):
    return pltpu.CompilerParams(
        dimension_semantics=("arbitrary",) * n_axes, vmem_limit_bytes=VMEM_LIMIT)


def _block_kernel(*refs, d_ff, pre_layout, has_ple, has_final):
    it = iter(refs)
    x_ref = next(it)
    if pre_layout:
        y_ref, wpre_ref = next(it), next(it)
    g_ref, win_ref, wout_ref = next(it), next(it), next(it)
    if has_ple:
        gp_ref, wg_ref, p_ref, wp_ref = next(it), next(it), next(it), next(it)
    if has_final:
        gf_ref = next(it)
    o_ref = next(it)

    x = x_ref[...]
    if pre_layout == "groups":
        y = jnp.concatenate([y_ref[0, gi] for gi in range(y_ref.shape[1])], axis=1)
        x = x + _dot(y, wpre_ref[...])
    elif pre_layout == "transposed":
        n_g, gw, tm = y_ref.shape[1:]
        x = x + _dot_tn(y_ref[0].reshape(n_g * gw, tm), wpre_ref[...])
    h = _rms(x, g_ref[...]).astype(BF16)
    gate = _dot(h, win_ref[:, :d_ff])
    up = _dot(h, win_ref[:, d_ff:])
    act = (gate * _sigmoid(gate) * up).astype(BF16)
    x = x + 0.5 * _dot(act, wout_ref[...])
    if has_ple:
        hg = _rms(x, gp_ref[...]).astype(BF16)
        emb_gate = _sigmoid(_dot(hg, wg_ref[...]))
        emb = _dot(p_ref[...].astype(BF16), wp_ref[...])
        x = x + emb_gate * emb
    if has_final:
        x = _rms(x, gf_ref[...])
    o_ref[...] = x


def _block_call(x, g, w_in, w_out, *, pre=None, ple=None, final=None, tm=512):
    t, d = x.shape
    row = lambda i: (i, 0)
    args, specs = [x], [pl.BlockSpec((tm, d), row)]

    def add(w):
        arr, spec = _pick(w)
        args.append(arr)
        specs.append(spec)
        return spec.block_shape

    pre_layout = None
    if pre is not None:
        y, w_pre, pre_layout = pre
        args.append(y)
        steps = (t // y.shape[0]) // tm
        if pre_layout == "groups":
            specs.append(pl.BlockSpec((1, y.shape[1], tm, y.shape[3]),
                                      lambda i: (i // steps, 0, i % steps, 0)))
        else:
            specs.append(pl.BlockSpec((1, y.shape[1], y.shape[2], tm),
                                      lambda i: (i // steps, 0, 0, i % steps)))
        add(w_pre)
    add(g)
    add(w_in)
    d_ff = add(w_out)[-2]
    if ple is not None:
        gp, wg, p, wp = ple
        add(gp)
        add(wg)
        args.append(p[0])
        specs.append(pl.BlockSpec((None, tm, p[0].shape[2]), lambda i: (p[1], i, 0)))
        add(wp)
    if final is not None:
        add(final)
    kern = functools.partial(_block_kernel, d_ff=d_ff, pre_layout=pre_layout,
                             has_ple=ple is not None, has_final=final is not None)
    return pl.pallas_call(
        kern, grid=(t // tm,), in_specs=specs, out_specs=pl.BlockSpec((tm, d), row),
        out_shape=jax.ShapeDtypeStruct((t, d), F32), compiler_params=_params(1),
        name="token_block")(*args)


def _hgrn_rec_constants(ts, dk):
    incl = jnp.arange(ts)[None, :] <= jnp.arange(ts)[:, None]
    prefix = jnp.concatenate([incl, incl, incl], axis=1).astype(BF16)
    col = jnp.arange(ts)[None, :]
    reduce_j = (jnp.arange(A_SUB * dk)[:, None] // dk) == (col % A_SUB)
    return prefix, reduce_j.astype(BF16)


def _hgrn_rec_kernel(x_ref, g_ref, w_ref, lbl_ref, gain_ref, prefix_ref, reduce_ref,
                     o_ref, st_ref, k_scr, bs_scr, e_scr, a_scr, *, ts, layer, q_scale):
    @pl.when(pl.program_id(2) == 0)
    def _():
        st_ref[...] = jnp.zeros_like(st_ref)

    dk = gain_ref.shape[-1]
    heads = range(st_ref.shape[0])
    lanes = [slice(hd * dk, (hd + 1) * dk) for hd in heads]
    width = len(heads) * dk

    h = _rms(x_ref[0], g_ref[...]).astype(BF16)
    q = _dot(h, w_ref[:, 0 * width:1 * width]) * q_scale
    f = _dot(h, w_ref[:, 1 * width:2 * width])
    v = _dot(h, w_ref[:, 2 * width:3 * width]).astype(BF16)
    out_gate = _sigmoid(_dot(h, w_ref[:, 3 * width:4 * width]))
    lbl = lbl_ref[...]
    e = jnp.exp(lbl - jnp.max(lbl, axis=0, keepdims=True))
    sm = e / jnp.sum(e, axis=0, keepdims=True)
    lb = jnp.sum(sm[:layer + 1], axis=0, keepdims=True)
    k = (1.0 - lb) * _sigmoid(-f)
    g = jnp.log(lb + (1.0 - lb) * _sigmoid(f)) * LOG2_E

    g1 = g.astype(BF16)
    r1 = g - g1.astype(F32)
    g2 = r1.astype(BF16)
    g3 = (r1 - g2.astype(F32)).astype(BF16)
    big_b = _dot(prefix_ref[...], jnp.concatenate([g1, g2, g3], axis=0))
    parts = [_hgrn_decays(q[:, ln], k[:, ln], big_b[:, ln], ts=ts) for ln in lanes]
    steepest = functools.reduce(jnp.maximum, [jnp.max(-pt["b_sub"]) for pt in parts])

    @pl.when(steepest <= A_SAFE_LOG2_DECAY)
    def _():
        for hd, pt in zip(heads, parts):
            k_up = (pt["k"] * jnp.exp2(-pt["b_sub"])).astype(BF16)
            a_scr[hd] = _dot_nt(pt["q_sub"], k_up)

    @pl.when(steepest > A_SAFE_LOG2_DECAY)
    def _():
        for hd, pt in zip(heads, parts):
            a_scr[hd] = _hgrn_exact_sub_scores(pt["q"], pt["k"], pt["b_sub"], reduce_ref[...],
                                                k_scr.at[hd], bs_scr.at[hd], e_scr.at[hd], ts=ts)

    for hd, pt, ln in zip(heads, parts, lanes):
        o, st_ref[hd] = _hgrn_outputs(pt, v[:, ln], st_ref[hd], a_scr[hd], ts=ts)
        o_ref[0, 0, :, ln] = (_rms(o, gain_ref[...]) * out_gate[:, ln]).astype(BF16)


def _hgrn_decays(q, k, big_b, *, ts):
    C, c = A_CHUNK, A_SUB
    nc, dk = ts // C, q.shape[-1]
    b_last = big_b[ts - 1:ts]
    after = b_last - big_b

    def since_start_of(size):
        groups = [big_b[:size]] + [big_b[lo:lo + size] - big_b[lo - 1:lo]
                                   for lo in range(size, ts, size)]
        return jnp.concatenate(groups, axis=0)

    b_chunk = since_start_of(C)
    b_sub = since_start_of(c)
    to_chunk_end = jnp.concatenate(
        [big_b[lo + C - 1:lo + C] - big_b[lo:lo + C] for lo in range(0, ts, C)],
        axis=0).reshape(nc, C, dk)

    return dict(
        q=q, k=k, b_last=b_last, after=after, b_sub=b_sub, to_chunk_end=to_chunk_end,
        q_blk=(q * jnp.exp2(big_b)).astype(BF16),
        k_blk=(k * jnp.exp2(after)).astype(BF16),
        q_chunk=(q * jnp.exp2(b_chunk)).astype(BF16),
        q_sub=(q * jnp.exp2(b_sub)).astype(BF16))


def _hgrn_outputs(pt, v, st, a_rep, *, ts):
    C, c = A_CHUNK, A_SUB
    nc, nsub = ts // C, C // c
    k, after, to_chunk_end = pt["k"], pt["after"], pt["to_chunk_end"]
    q_chunk, q_sub = pt["q_chunk"], pt["q_sub"]
    dk = k.shape[-1]

    zeros2 = lambda n: jnp.zeros((n, dk), BF16)
    qs_parts, ks_parts = [], []
    for n in range(1, nc):
        lo, hi = n * C, (n + 1) * C
        decay = jnp.exp2(after[:lo] - after[lo - 1:lo])
        ks_parts.append(jnp.concatenate([(k[:lo] * decay).astype(BF16), zeros2(ts - lo)], axis=0))
        pieces = [zeros2(lo), q_chunk[lo:hi]] + ([zeros2(ts - hi)] if hi < ts else [])
        qs_parts.append(jnp.concatenate(pieces, axis=0))
    a_chunks = _dot_nt(jnp.concatenate(qs_parts, axis=1), jnp.concatenate(ks_parts, axis=1))

    k3 = k.reshape(nc, C, dk)
    qs3 = q_sub.reshape(nc, C, dk)
    zeros3 = lambda n: jnp.zeros((nc, n, dk), BF16)
    qs_parts, ks_parts = [], []
    for i in range(1, nsub):
        lo, hi = i * c, (i + 1) * c
        decay = jnp.exp2(to_chunk_end[:, :lo] - to_chunk_end[:, lo - 1:lo])
        ks_parts.append(jnp.concatenate([(k3[:, :lo] * decay).astype(BF16), zeros3(C - lo)], axis=1))
        pieces = [zeros3(lo), qs3[:, lo:hi]] + ([zeros3(C - hi)] if hi < C else [])
        qs_parts.append(jnp.concatenate(pieces, axis=1))
    a_subs = _dot_nt(jnp.concatenate(qs_parts, axis=2).reshape(ts, (nsub - 1) * dk),
                     jnp.concatenate(ks_parts, axis=2).reshape(ts, (nsub - 1) * dk))

    rr = lax.broadcasted_iota(jnp.int32, (ts, ts), 0)
    ll = lax.broadcasted_iota(jnp.int32, (ts, ts), 1)
    lc, ls = C.bit_length() - 1, c.bit_length() - 1
    same_chunk = (rr >> lc) == (ll >> lc)
    own_sub = ((rr >> ls) == (ll >> ls)) & ((ll & (c - 1)) <= (rr & (c - 1)))
    a = jnp.where(own_sub, a_rep, 0.0) + jnp.where(same_chunk, a_subs, 0.0) + a_chunks

    o = _dot(a.astype(BF16), v) + _dot_nt(pt["q_blk"], st.astype(BF16))
    return o, st * jnp.exp2(pt["b_last"]) + _dot_tn(v, pt["k_blk"])


def _hgrn_exact_sub_scores(q, k, b_sub, reduce_j, k_scr, bs_scr, e_scr, *, ts):
    c, dk = A_SUB, q.shape[-1]
    k_scr[...] = k
    bs_scr[...] = b_sub
    zero_tile = jnp.zeros((F32_SUBLANES, dk), F32)
    for n in range(ts // c):
        tiles = [slice(n * c + lo, n * c + lo + F32_SUBLANES) for lo in range(0, c, F32_SUBLANES)]
        for j in range(c):
            row = slice(n * c + j, n * c + j + 1)
            k_j, b_j = k_scr[row, :], bs_scr[row, :]
            e = [q[t] * k_j * jnp.exp2(jnp.minimum(b_sub[t] - b_j, 0.0))
                 if t.stop > n * c + j else zero_tile for t in tiles]
            e_scr[n * c:(n + 1) * c, j * dk:(j + 1) * dk] = jnp.concatenate(e, axis=0).astype(BF16)
    return _dot(e_scr[...], reduce_j)


def _hgrn_mixer_call(x, g, w_in, lb_logits, gain, *, layer, ts=256):
    bsz, seq, d = x.shape
    width = w_in.shape[1] // 4
    dk = gain.shape[-1]
    heads = width // dk
    prefix, reduce_j = _hgrn_rec_constants(ts, dk)
    kern = functools.partial(_hgrn_rec_kernel, ts=ts, layer=layer, q_scale=float(dk) ** -0.5)
    consts = (g, w_in, lb_logits, gain, prefix, reduce_j)
    return pl.pallas_call(
        kern, grid=(bsz, 1, seq // ts),
        in_specs=[pl.BlockSpec((1, ts, d), lambda b, h, s: (b, s, 0))]
        + [_const_spec(c.shape) for c in consts],
        out_specs=pl.BlockSpec((1, 1, ts, width), lambda b, h, s: (b, h, s, 0)),
        out_shape=jax.ShapeDtypeStruct((bsz, 1, seq, width), BF16),
        scratch_shapes=[pltpu.VMEM((heads, dk, dk), F32),
                        pltpu.VMEM((heads, ts, dk), F32),
                        pltpu.VMEM((heads, ts, dk), F32),
                        pltpu.VMEM((heads, ts, A_SUB * dk), BF16),
                        pltpu.VMEM((heads, ts, ts), F32)],
        compiler_params=_params(3), name="hgrn_mixer")(x, *consts)


def _rope_tables_lanes(pos_row, inv_freq_col, half):
    ang = inv_freq_col * pos_row.astype(F32)
    cos, sin = jnp.cos(ang), jnp.sin(ang)
    zeros = jnp.zeros((128 - 2 * half, ang.shape[1]), F32)
    c_tab = jnp.transpose(jnp.concatenate([cos, cos, zeros], axis=0))
    s_tab = jnp.transpose(jnp.concatenate([-sin, sin, zeros], axis=0))
    return c_tab, s_tab


def _kv_kernel(x_ref, pos_ref, g_ref, wdc_ref, wdr_ref, gl_ref, wk_ref, wvt_ref, freq_ref,
               k_out, vt_out, *, heads, nope, vdim, half):
    h = _rms(x_ref[0], g_ref[...]).astype(BF16)
    c_kv = _rms(_dot(h, wdc_ref[...]), gl_ref[...]).astype(BF16)
    kr = _dot(h, wdr_ref[...])
    c_tab, s_tab = _rope_tables_lanes(pos_ref[0], freq_ref[...], half)
    k_rope = (kr * c_tab + pltpu.roll(kr, half, 1) * s_tab).astype(BF16)
    k_nope = _dot(c_kv, wk_ref[...])
    v_t = _dot_nt(wvt_ref[...], c_kv)
    ones = jnp.ones((vt_out.shape[2] - vdim, v_t.shape[1]), BF16)
    for hd in range(heads):
        k_out[0, hd, :, 0:nope] = k_nope[:, hd * nope:(hd + 1) * nope].astype(BF16)
        k_out[0, hd, :, nope:] = k_rope
        vt_out[0, hd, 0:vdim, :] = v_t[hd * vdim:(hd + 1) * vdim].astype(BF16)
        vt_out[0, hd, vdim:, :] = ones


def _kv_call(x, pos_row, g, wd_c, wd_r, gl, wk, wvt, freq_col, *, heads, nope, vdim, half, tm=1024):
    bsz, seq, d = x.shape
    kern = functools.partial(_kv_kernel, heads=heads, nope=nope, vdim=vdim, half=half)
    kd = nope + 128
    vrows = vdim + BF16_SUBLANES
    return pl.pallas_call(
        kern, grid=(bsz, seq // tm),
        in_specs=[pl.BlockSpec((1, tm, d), lambda b, s: (b, s, 0)),
                  pl.BlockSpec((1, 1, tm), lambda b, s: (b, 0, s)),
                  _const_spec(g.shape), _const_spec(wd_c.shape), _const_spec(wd_r.shape),
                  _const_spec(gl.shape), _const_spec(wk.shape), _const_spec(wvt.shape),
                  _const_spec(freq_col.shape)],
        out_specs=[pl.BlockSpec((1, heads, tm, kd), lambda b, s: (b, 0, s, 0)),
                   pl.BlockSpec((1, heads, vrows, tm), lambda b, s: (b, 0, 0, s))],
        out_shape=[jax.ShapeDtypeStruct((bsz, heads, seq, kd), BF16),
                   jax.ShapeDtypeStruct((bsz, heads, vrows, seq), BF16)],
        compiler_params=_params(2), name="mla_shared_kv")(
            x, pos_row, g, wd_c, wd_r, gl, wk, wvt, freq_col)


def _q_kernel(x_ref, pos_ref, g_ref, wdq_ref, gq_ref, wuqt_ref, freq_ref, qt_out,
              *, heads, nope, half, qd, scale):
    h = _rms(x_ref[0], g_ref[...]).astype(BF16)
    c_q = (_rms(_dot(h, wdq_ref[...]), gq_ref[...]) * scale).astype(BF16)
    q_t = _dot_nt(wuqt_ref[...], c_q)
    ang = freq_ref[...] * pos_ref[0].astype(F32)
    cos, sin = jnp.cos(ang), jnp.sin(ang)
    zeros = jnp.zeros((qd - nope - 2 * half, q_t.shape[1]), F32)
    for hd in range(heads):
        base = hd * (nope + 2 * half)
        x1 = q_t[base + nope:base + nope + half]
        x2 = q_t[base + nope + half:base + nope + 2 * half]
        full = jnp.concatenate(
            [q_t[base:base + nope], x1 * cos - x2 * sin, x2 * cos + x1 * sin, zeros], axis=0)
        qt_out[0, hd] = full.astype(BF16)


def _q_call(x, pos_row, g, w_dq, gq, w_uqt, freq_col, *, heads, nope, half, qd, scale, tm=1024):
    bsz, seq, d = x.shape
    kern = functools.partial(_q_kernel, heads=heads, nope=nope, half=half, qd=qd, scale=scale)
    return pl.pallas_call(
        kern, grid=(bsz, seq // tm),
        in_specs=[pl.BlockSpec((1, tm, d), lambda b, s: (b, s, 0)),
                  pl.BlockSpec((1, 1, tm), lambda b, s: (b, 0, s)),
                  _const_spec(g.shape), _const_spec(w_dq.shape), _const_spec(gq.shape),
                  _const_spec(w_uqt.shape), _const_spec(freq_col.shape)],
        out_specs=pl.BlockSpec((1, heads, qd, tm), lambda b, s: (b, 0, 0, s)),
        out_shape=jax.ShapeDtypeStruct((bsz, heads, qd, seq), BF16),
        compiler_params=_params(2), name="mla_q")(x, pos_row, g, w_dq, gq, w_uqt, freq_col)


def _attn_kernel(qt_ref, k_ref, vt_ref, o_ref, s0, s1, p0, p1, acc_ref, *, tq, tk, dv):
    for hd in range(qt_ref.shape[1]):
        for qi in range(qt_ref.shape[3] // tq):
            cols = pl.ds(qi * tq, tq)
            _attn_head(qt_ref.at[0, hd, :, cols], k_ref.at[0, hd], vt_ref.at[0, hd],
                       o_ref.at[0, hd, :, cols], s0, s1, p0, p1, acc_ref, qi, tq=tq, tk=tk, dv=dv)


def _attn_head(qt_ref, k_ref, vt_ref, o_ref, s0, s1, p0, p1, acc_ref, qi, *, tq, tk, dv):

    def key_block(j):
        return k_ref[pl.ds(pl.multiple_of(j * tk, tk), tk), :]

    def value_block(j):
        return vt_ref[:, pl.ds(pl.multiple_of(j * tk, tk), tk)]

    def scores(j, s_out):
        s = _dot(key_block(j), qt_ref[...])
        s_out[...] = s
        return jnp.max(s, axis=0, keepdims=True)

    def probabilities(s, m, m_blk):
        m_new = jnp.maximum(m, m_blk)
        return m_new, jnp.exp2(m - m_new), jnp.exp2((s - m_new).astype(BF16))

    def softmax_step(s_in, p_out, m, m_blk):
        m_new, alpha, p = probabilities(s_in[...], m, m_blk)
        p_out[...] = p
        return m_new, alpha

    def accumulate(j, p_in, alpha):
        acc_ref[...] = alpha * acc_ref[...] + _dot(value_block(j), p_in[...])

    def body(j, carry):
        a_prev, m, mb0 = carry
        mb1 = scores(2 * j + 1, s1)
        m, a0 = softmax_step(s0, p0, m, mb0)
        accumulate(jnp.maximum(2 * j - 1, 0), p1, a_prev)
        mb0 = scores(2 * j + 2, s0)
        m, a1 = softmax_step(s1, p1, m, mb1)
        accumulate(2 * j, p0, a0)
        return a1, m, mb0

    n_diag = tq // tk
    n_full = n_diag * qi
    acc_ref[...] = jnp.zeros_like(acc_ref)
    p1[...] = jnp.zeros_like(p1)
    init = (jnp.ones((1, tq), F32), jnp.full((1, tq), NEG_BIG, F32), scores(0, s0))
    a_prev, m, _ = lax.fori_loop(0, (n_diag // 2) * qi, body, init)
    accumulate(jnp.maximum(n_full - 1, 0), p1, a_prev)

    causal = (lax.broadcasted_iota(jnp.int32, (tk, tk), 0)
              <= lax.broadcasted_iota(jnp.int32, (tk, tk), 1))
    for d in range(n_diag):
        lo = d * tk
        s = s0[...] if d == 0 else _dot(key_block(n_full + d), qt_ref[:, lo:])
        masked = jnp.where(causal, s[:, :tk], NEG_BIG)
        s = jnp.concatenate([masked, s[:, tk:]], axis=1) if lo + tk < tq else masked
        m_d, a_d, p_d = probabilities(s, m[:, lo:], jnp.max(s, axis=0, keepdims=True))
        m = jnp.concatenate([m[:, :lo], m_d], axis=1) if lo else m_d
        acc_ref[:, lo:] = a_d * acc_ref[:, lo:] + _dot(value_block(n_full + d), p_d)
    acc = acc_ref[...]
    o_ref[...] = (acc[:dv] / acc[dv:dv + 1]).astype(BF16)


def _attn_call(qt, k, vt, *, tq=2048, tk=512, hps=1):
    assert (tq // tk) % 2 == 0
    bsz, heads, qd, seq = qt.shape
    vrows = vt.shape[2]
    dv = vrows - BF16_SUBLANES
    kern = functools.partial(_attn_kernel, tq=tq, tk=tk, dv=dv)
    return pl.pallas_call(
        kern, grid=(bsz, heads // hps, 1),
        in_specs=[pl.BlockSpec((1, hps, qd, seq), lambda b, h, i: (b, h, 0, 0)),
                  pl.BlockSpec((1, hps, seq, k.shape[3]), lambda b, h, i: (b, h, 0, 0)),
                  pl.BlockSpec((1, hps, vrows, seq), lambda b, h, i: (b, h, 0, 0))],
        out_specs=pl.BlockSpec((1, hps, dv, seq), lambda b, h, i: (b, h, 0, 0)),
        out_shape=jax.ShapeDtypeStruct((bsz, heads, dv, seq), BF16),
        scratch_shapes=[pltpu.VMEM((tk, tq), F32), pltpu.VMEM((tk, tq), F32),
                        pltpu.VMEM((tk, tq), BF16), pltpu.VMEM((tk, tq), BF16),
                        pltpu.VMEM((vrows, tq), F32)],
        compiler_params=_params(3), name="mla_attention")(qt, k, vt)


def kernel(x, p, positions, norm_gains, ffn_w_in, ffn_w_out, ple_w_gate, ple_w_in, a_w_in,
           a_lb_logits, a_out_gain, a_w_out, kv_norm_in, kv_w_down, kv_latent_norm, kv_w_up,
           b_w_dq, b_q_norm, b_w_uq, b_w_out, final_norm):
    bsz, seq, d = x.shape
    depth = norm_gains.shape[0]
    n_a = a_w_in.shape[0]
    t = bsz * seq
    bf = lambda w: w.astype(BF16)
    row = lambda g: g.reshape(1, -1).astype(F32)

    kv_lora = kv_latent_norm.shape[0]
    rope = kv_w_down.shape[1] - kv_lora
    half = rope // 2
    vdim = 128
    heads = b_w_out.shape[1] // vdim
    nope = kv_w_up.shape[1] // heads - vdim
    qd = 256
    assert nope == 128 and rope == 64 and b_w_uq.shape[2] == heads * (nope + rope)

    inv_freq = 1.0 / (ROPE_THETA ** (jnp.arange(0, rope, 2, dtype=F32) / rope))
    freq_col = inv_freq.reshape(half, 1)
    pos_row = positions.reshape(bsz, 1, seq)

    wd_c = bf(kv_w_down[:, :kv_lora])
    wd_r = bf(jnp.concatenate([kv_w_down[:, kv_lora:]] * (128 // rope), axis=1))
    w_up = kv_w_up.reshape(kv_lora, heads, nope + vdim)
    wk = bf(w_up[:, :, :nope].reshape(kv_lora, heads * nope))
    wvt = bf(w_up[:, :, nope:].reshape(kv_lora, heads * vdim).T)

    def shared_kv(xs):
        return _kv_call(xs.reshape(bsz, seq, d), pos_row, row(kv_norm_in), wd_c, wd_r,
                        row(kv_latent_norm), wk, wvt, freq_col,
                        heads=heads, nope=nope, vdim=vdim, half=half)

    ffn_in, ffn_out = bf(ffn_w_in), bf(ffn_w_out)
    ple_gate, ple_in = bf(ple_w_gate), bf(ple_w_in)
    gains = norm_gains.astype(F32).reshape(depth, norm_gains.shape[1], 1, d)
    p_tokens = p.reshape(depth, t, -1)

    xf = x.reshape(t, d)
    shared = shared_kv(xf) if n_a == 0 else None
    attn_scale = float(nope + rope) ** -0.5 * LOG2_E
    for li in range(depth):
        x1 = _block_call(xf, (gains, (li, 0)), (ffn_in, (li, 0)), (ffn_out, (li, 0)))
        if li < n_a:
            o = _hgrn_mixer_call(x1.reshape(bsz, seq, d), row(norm_gains[li, 1]), bf(a_w_in[li]),
                                 a_lb_logits.astype(F32), row(a_out_gain[li]), layer=li)
            pre = (o, bf(a_w_out[li]), "groups")
        else:
            bi = li - n_a
            w_uqt = bf(b_w_uq[bi].T)
            qt = _q_call(x1.reshape(bsz, seq, d), pos_row, row(norm_gains[li, 1]),
                         bf(b_w_dq[bi]), row(b_q_norm[bi]), w_uqt, freq_col, heads=heads,
                         nope=nope, half=half, qd=qd, scale=attn_scale)
            k_all, vt_all = shared
            o = _attn_call(qt, k_all, vt_all)
            pre = (o, bf(b_w_out[bi]), "transposed")
        ple = ((gains, (li, 3)), (ple_gate, (li,)), (p_tokens, li), (ple_in, (li,)))
        final = row(final_norm) if li == depth - 1 else None
        xf = _block_call(x1, (gains, (li, 2)), (ffn_in, (li, 1)), (ffn_out, (li, 1)),
                         pre=pre, ple=ple, final=final)
        if li == n_a - 1:
            shared = shared_kv(xf)
    return xf.reshape(bsz, seq, d)
```

```python
import functools
import math

import jax
import jax.numpy as jnp
from jax import lax
from jax.experimental import pallas as pl
from jax.experimental.pallas import tpu as pltpu

F32 = jnp.float32
BF16 = jnp.bfloat16

EPS = 1e-6
ROPE_THETA = 10000.0
A_CHUNK = 64
A_SUB = 16
A_SAFE_LOG2_DECAY = 96.0
NEG_BIG = -1e30
LOG2_E = math.log2(math.e)

V7X_VMEM_BYTES = 64 * 1024 * 1024
VMEM_LIMIT = V7X_VMEM_BYTES * 7 // 8
F32_SUBLANES = 8
BF16_SUBLANES = 16


def _rms(x, g):
    ms = jnp.mean(x * x, axis=-1, keepdims=True)
    return x * lax.rsqrt(ms + EPS) * g


def _sigmoid(x):
    return 1.0 / (1.0 + jnp.exp(-x))


def _dot(a, b):
    return jnp.dot(a, b, preferred_element_type=F32)


def _dot_nt(a, b):
    return lax.dot_general(a, b, (((1,), (1,)), ((), ())), preferred_element_type=F32)


def _dot_tn(a, b):
    return lax.dot_general(a, b, (((0,), (0,)), ((), ())), preferred_element_type=F32)


def _const_spec(shape):
    nd = len(shape)
    return pl.BlockSpec(shape, lambda *_: (0,) * nd, pipeline_mode=pl.Buffered(1))


def _pick(w):
    arr, lead = w if isinstance(w, tuple) else (w, ())
    rest = arr.shape[len(lead):]
    index = tuple(lead) + (0,) * len(rest)
    spec = pl.BlockSpec((None,) * len(lead) + rest, lambda *_: index,
                        pipeline_mode=pl.Buffered(1))
    return arr, spec


def _params(n_axes):
    return pltpu.CompilerParams(
        dimension_semantics=("arbitrary",) * n_axes, vmem_limit_bytes=VMEM_LIMIT)


def _block_kernel(*refs, d_ff, pre_layout, has_ple, has_final):
    it = iter(refs)
    x_ref = next(it)
    if pre_layout:
        y_ref, wpre_ref = next(it), next(it)
    g_ref, win_ref, wout_ref = next(it), next(it), next(it)
    if has_ple:
        gp_ref, wg_ref, p_ref, wp_ref = next(it), next(it), next(it), next(it)
    if has_final:
        gf_ref = next(it)
    o_ref = next(it)

    x = x_ref[...]
    if pre_layout == "groups":
        y = jnp.concatenate([y_ref[0, gi] for gi in range(y_ref.shape[1])], axis=1)
        x = x + _dot(y, wpre_ref[...])
    elif pre_layout == "transposed":
        n_g, gw, tm = y_ref.shape[1:]
        x = x + _dot_tn(y_ref[0].reshape(n_g * gw, tm), wpre_ref[...])
    h = _rms(x, g_ref[...]).astype(BF16)
    gate = _dot(h, win_ref[:, :d_ff])
    up = _dot(h, win_ref[:, d_ff:])
    act = (gate * _sigmoid(gate) * up).astype(BF16)
    x = x + 0.5 * _dot(act, wout_ref[...])
    if has_ple:
        hg = _rms(x, gp_ref[...]).astype(BF16)
        emb_gate = _sigmoid(_dot(hg, wg_ref[...]))
        emb = _dot(p_ref[...].astype(BF16), wp_ref[...])
        x = x + emb_gate * emb
    if has_final:
        x = _rms(x, gf_ref[...])
    o_ref[...] = x


def _block_call(x, g, w_in, w_out, *, pre=None, ple=None, final=None, tm=512):
    t, d = x.shape
    row = lambda i: (i, 0)
    args, specs = [x], [pl.BlockSpec((tm, d), row)]

    def add(w):
        arr, spec = _pick(w)
        args.append(arr)
        specs.append(spec)
        return spec.block_shape

    pre_layout = None
    if pre is not None:
        y, w_pre, pre_layout = pre
        args.append(y)
        steps = (t // y.shape[0]) // tm
        if pre_layout == "groups":
            specs.append(pl.BlockSpec((1, y.shape[1], tm, y.shape[3]),
                                      lambda i: (i // steps, 0, i % steps, 0)))
        else:
            specs.append(pl.BlockSpec((1, y.shape[1], y.shape[2], tm),
                                      lambda i: (i // steps, 0, 0, i % steps)))
        add(w_pre)
    add(g)
    add(w_in)
    d_ff = add(w_out)[-2]
    if ple is not None:
        gp, wg, p, wp = ple
        add(gp)
        add(wg)
        args.append(p[0])
        specs.append(pl.BlockSpec((None, tm, p[0].shape[2]), lambda i: (p[1], i, 0)))
        add(wp)
    if final is not None:
        add(final)
    kern = functools.partial(_block_kernel, d_ff=d_ff, pre_layout=pre_layout,
                             has_ple=ple is not None, has_final=final is not None)
    return pl.pallas_call(
        kern, grid=(t // tm,), in_specs=specs, out_specs=pl.BlockSpec((tm, d), row),
        out_shape=jax.ShapeDtypeStruct((t, d), F32), compiler_params=_params(1),
        name="token_block")(*args)


def _hgrn_rec_constants(ts, dk):
    incl = jnp.arange(ts)[None, :] <= jnp.arange(ts)[:, None]
    prefix = jnp.concatenate([incl, incl, incl], axis=1).astype(BF16)
    col = jnp.arange(ts)[None, :]
    reduce_j = (jnp.arange(A_SUB * dk)[:, None] // dk) == (col % A_SUB)
    return prefix, reduce_j.astype(BF16)


def _hgrn_rec_kernel(x_ref, g_ref, w_ref, lbl_ref, gain_ref, prefix_ref, reduce_ref,
                     o_ref, st_ref, k_scr, bs_scr, e_scr, a_scr, *, ts, layer, q_scale):
    @pl.when(pl.program_id(2) == 0)
    def _():
        st_ref[...] = jnp.zeros_like(st_ref)

    dk = gain_ref.shape[-1]
    heads = range(st_ref.shape[0])
    lanes = [slice(hd * dk, (hd + 1) * dk) for hd in heads]
    width = len(heads) * dk

    h = _rms(x_ref[0], g_ref[...]).astype(BF16)
    q = _dot(h, w_ref[:, 0 * width:1 * width]) * q_scale
    f = _dot(h, w_ref[:, 1 * width:2 * width])
    v = _dot(h, w_ref[:, 2 * width:3 * width]).astype(BF16)
    out_gate = _sigmoid(_dot(h, w_ref[:, 3 * width:4 * width]))
    lbl = lbl_ref[...]
    e = jnp.exp(lbl - jnp.max(lbl, axis=0, keepdims=True))
    sm = e / jnp.sum(e, axis=0, keepdims=True)
    lb = jnp.sum(sm[:layer + 1], axis=0, keepdims=True)
    k = (1.0 - lb) * _sigmoid(-f)
    g = jnp.log(lb + (1.0 - lb) * _sigmoid(f)) * LOG2_E

    g1 = g.astype(BF16)
    r1 = g - g1.astype(F32)
    g2 = r1.astype(BF16)
    g3 = (r1 - g2.astype(F32)).astype(BF16)
    big_b = _dot(prefix_ref[...], jnp.concatenate([g1, g2, g3], axis=0))
    parts = [_hgrn_decays(q[:, ln], k[:, ln], big_b[:, ln], ts=ts) for ln in lanes]
    steepest = functools.reduce(jnp.maximum, [jnp.max(-pt["b_sub"]) for pt in parts])

    @pl.when(steepest <= A_SAFE_LOG2_DECAY)
    def _():
        for hd, pt in zip(heads, parts):
            k_up = (pt["k"] * jnp.exp2(-pt["b_sub"])).astype(BF16)
            a_scr[hd] = _dot_nt(pt["q_sub"], k_up)

    @pl.when(steepest > A_SAFE_LOG2_DECAY)
    def _():
        for hd, pt in zip(heads, parts):
            a_scr[hd] = _hgrn_exact_sub_scores(pt["q"], pt["k"], pt["b_sub"], reduce_ref[...],
                                                k_scr.at[hd], bs_scr.at[hd], e_scr.at[hd], ts=ts)

    for hd, pt, ln in zip(heads, parts, lanes):
        o, st_ref[hd] = _hgrn_outputs(pt, v[:, ln], st_ref[hd], a_scr[hd], ts=ts)
        o_ref[0, 0, :, ln] = (_rms(o, gain_ref[...]) * out_gate[:, ln]).astype(BF16)


def _hgrn_decays(q, k, big_b, *, ts):
    C, c = A_CHUNK, A_SUB
    nc, dk = ts // C, q.shape[-1]
    b_last = big_b[ts - 1:ts]
    after = b_last - big_b

    def since_start_of(size):
        groups = [big_b[:size]] + [big_b[lo:lo + size] - big_b[lo - 1:lo]
                                   for lo in range(size, ts, size)]
        return jnp.concatenate(groups, axis=0)

    b_chunk = since_start_of(C)
    b_sub = since_start_of(c)
    to_chunk_end = jnp.concatenate(
        [big_b[lo + C - 1:lo + C] - big_b[lo:lo + C] for lo in range(0, ts, C)],
        axis=0).reshape(nc, C, dk)

    return dict(
        q=q, k=k, b_last=b_last, after=after, b_sub=b_sub, to_chunk_end=to_chunk_end,
        q_blk=(q * jnp.exp2(big_b)).astype(BF16),
        k_blk=(k * jnp.exp2(after)).astype(BF16),
        q_chunk=(q * jnp.exp2(b_chunk)).astype(BF16),
        q_sub=(q * jnp.exp2(b_sub)).astype(BF16))


def _hgrn_outputs(pt, v, st, a_rep, *, ts):
    C, c = A_CHUNK, A_SUB
    nc, nsub = ts // C, C // c
    k, after, to_chunk_end = pt["k"], pt["after"], pt["to_chunk_end"]
    q_chunk, q_sub = pt["q_chunk"], pt["q_sub"]
    dk = k.shape[-1]

    zeros2 = lambda n: jnp.zeros((n, dk), BF16)
    qs_parts, ks_parts = [], []
    for n in range(1, nc):
        lo, hi = n * C, (n + 1) * C
        decay = jnp.exp2(after[:lo] - after[lo - 1:lo])
        ks_parts.append(jnp.concatenate([(k[:lo] * decay).astype(BF16), zeros2(ts - lo)], axis=0))
        pieces = [zeros2(lo), q_chunk[lo:hi]] + ([zeros2(ts - hi)] if hi < ts else [])
        qs_parts.append(jnp.concatenate(pieces, axis=0))
    a_chunks = _dot_nt(jnp.concatenate(qs_parts, axis=1), jnp.concatenate(ks_parts, axis=1))

    k3 = k.reshape(nc, C, dk)
    qs3 = q_sub.reshape(nc, C, dk)
    zeros3 = lambda n: jnp.zeros((nc, n, dk), BF16)
    qs_parts, ks_parts = [], []
    for i in range(1, nsub):
        lo, hi = i * c, (i + 1) * c
        decay = jnp.exp2(to_chunk_end[:, :lo] - to_chunk_end[:, lo - 1:lo])
        ks_parts.append(jnp.concatenate([(k3[:, :lo] * decay).astype(BF16), zeros3(C - lo)], axis=1))
        pieces = [zeros3(lo), qs3[:, lo:hi]] + ([zeros3(C - hi)] if hi < C else [])
        qs_parts.append(jnp.concatenate(pieces, axis=1))
    a_subs = _dot_nt(jnp.concatenate(qs_parts, axis=2).reshape(ts, (nsub - 1) * dk),
                     jnp.concatenate(ks_parts, axis=2).reshape(ts, (nsub - 1) * dk))

    rr = lax.broadcasted_iota(jnp.int32, (ts, ts), 0)
    ll = lax.broadcasted_iota(jnp.int32, (ts, ts), 1)
    lc, ls = C.bit_length() - 1, c.bit_length() - 1
    same_chunk = (rr >> lc) == (ll >> lc)
    own_sub = ((rr >> ls) == (ll >> ls)) & ((ll & (c - 1)) <= (rr & (c - 1)))
    a = jnp.where(own_sub, a_rep, 0.0) + jnp.where(same_chunk, a_subs, 0.0) + a_chunks

    o = _dot(a.astype(BF16), v) + _dot_nt(pt["q_blk"], st.astype(BF16))
    return o, st * jnp.exp2(pt["b_last"]) + _dot_tn(v, pt["k_blk"])


def _hgrn_exact_sub_scores(q, k, b_sub, reduce_j, k_scr, bs_scr, e_scr, *, ts):
    c, dk = A_SUB, q.shape[-1]
    k_scr[...] = k
    bs_scr[...] = b_sub
    zero_tile = jnp.zeros((F32_SUBLANES, dk), F32)
    for n in range(ts // c):
        tiles = [slice(n * c + lo, n * c + lo + F32_SUBLANES) for lo in range(0, c, F32_SUBLANES)]
        for j in range(c):
            row = slice(n * c + j, n * c + j + 1)
            k_j, b_j = k_scr[row, :], bs_scr[row, :]
            e = [q[t] * k_j * jnp.exp2(jnp.minimum(b_sub[t] - b_j, 0.0))
                 if t.stop > n * c + j else zero_tile for t in tiles]
            e_scr[n * c:(n + 1) * c, j * dk:(j + 1) * dk] = jnp.concatenate(e, axis=0).astype(BF16)
    return _dot(e_scr[...], reduce_j)


def _hgrn_mixer_call(x, g, w_in, lb_logits, gain, *, layer, ts=256):
    bsz, seq, d = x.shape
    width = w_in.shape[1] // 4
    dk = gain.shape[-1]
    heads = width // dk
    prefix, reduce_j = _hgrn_rec_constants(ts, dk)
    kern = functools.partial(_hgrn_rec_kernel, ts=ts, layer=layer, q_scale=float(dk) ** -0.5)
    consts = (g, w_in, lb_logits, gain, prefix, reduce_j)
    return pl.pallas_call(
        kern, grid=(bsz, 1, seq // ts),
        in_specs=[pl.BlockSpec((1, ts, d), lambda b, h, s: (b, s, 0))]
        + [_const_spec(c.shape) for c in consts],
        out_specs=pl.BlockSpec((1, 1, ts, width), lambda b, h, s: (b, h, s, 0)),
        out_shape=jax.ShapeDtypeStruct((bsz, 1, seq, width), BF16),
        scratch_shapes=[pltpu.VMEM((heads, dk, dk), F32),
                        pltpu.VMEM((heads, ts, dk), F32),
                        pltpu.VMEM((heads, ts, dk), F32),
                        pltpu.VMEM((heads, ts, A_SUB * dk), BF16),
                        pltpu.VMEM((heads, ts, ts), F32)],
        compiler_params=_params(3), name="hgrn_mixer")(x, *consts)


def _rope_tables_lanes(pos_row, inv_freq_col, half):
    ang = inv_freq_col * pos_row.astype(F32)
    cos, sin = jnp.cos(ang), jnp.sin(ang)
    zeros = jnp.zeros((128 - 2 * half, ang.shape[1]), F32)
    c_tab = jnp.transpose(jnp.concatenate([cos, cos, zeros], axis=0))
    s_tab = jnp.transpose(jnp.concatenate([-sin, sin, zeros], axis=0))
    return c_tab, s_tab


def _kv_kernel(x_ref, pos_ref, g_ref, wdc_ref, wdr_ref, gl_ref, wk_ref, wvt_ref, freq_ref,
               kn_out, kr_out, vt_out, *, heads, nope, vdim, half):
    h = _rms(x_ref[0], g_ref[...]).astype(BF16)
    c_kv = _rms(_dot(h, wdc_ref[...]), gl_ref[...]).astype(BF16)
    kr = _dot(h, wdr_ref[...])
    c_tab, s_tab = _rope_tables_lanes(pos_ref[0], freq_ref[...], half)
    kr_out[0] = (kr * c_tab + pltpu.roll(kr, half, 1) * s_tab).astype(BF16)
    k_nope = _dot(c_kv, wk_ref[...])
    v_t = _dot_nt(wvt_ref[...], c_kv)
    ones = jnp.ones((vt_out.shape[2] - vdim, v_t.shape[1]), BF16)
    for hd in range(heads):
        kn_out[0, hd] = k_nope[:, hd * nope:(hd + 1) * nope].astype(BF16)
        vt_out[0, hd, 0:vdim, :] = v_t[hd * vdim:(hd + 1) * vdim].astype(BF16)
        vt_out[0, hd, vdim:, :] = ones


def _kv_call(x, pos_row, g, wd_c, wd_r, gl, wk, wvt, freq_col, *, heads, nope, vdim, half, tm=1024):
    bsz, seq, d = x.shape
    kern = functools.partial(_kv_kernel, heads=heads, nope=nope, vdim=vdim, half=half)
    vrows = vdim + BF16_SUBLANES
    return pl.pallas_call(
        kern, grid=(bsz, seq // tm),
        in_specs=[pl.BlockSpec((1, tm, d), lambda b, s: (b, s, 0)),
                  pl.BlockSpec((1, 1, tm), lambda b, s: (b, 0, s)),
                  _const_spec(g.shape), _const_spec(wd_c.shape), _const_spec(wd_r.shape),
                  _const_spec(gl.shape), _const_spec(wk.shape), _const_spec(wvt.shape),
                  _const_spec(freq_col.shape)],
        out_specs=[pl.BlockSpec((1, heads, tm, nope), lambda b, s: (b, 0, s, 0)),
                   pl.BlockSpec((1, tm, 128), lambda b, s: (b, s, 0)),
                   pl.BlockSpec((1, heads, vrows, tm), lambda b, s: (b, 0, 0, s))],
        out_shape=[jax.ShapeDtypeStruct((bsz, heads, seq, nope), BF16),
                   jax.ShapeDtypeStruct((bsz, seq, 128), BF16),
                   jax.ShapeDtypeStruct((bsz, heads, vrows, seq), BF16)],
        compiler_params=_params(2), name="mla_shared_kv")(
            x, pos_row, g, wd_c, wd_r, gl, wk, wvt, freq_col)


def _q_kernel(x_ref, pos_ref, g_ref, wdq_ref, gq_ref, wuqt_ref, freq_ref, qt_out,
              *, heads, nope, half, qd, scale):
    h = _rms(x_ref[0], g_ref[...]).astype(BF16)
    c_q = (_rms(_dot(h, wdq_ref[...]), gq_ref[...]) * scale).astype(BF16)
    q_t = _dot_nt(wuqt_ref[...], c_q)
    ang = freq_ref[...] * pos_ref[0].astype(F32)
    cos, sin = jnp.cos(ang), jnp.sin(ang)
    zeros = jnp.zeros((qd - nope - 2 * half, q_t.shape[1]), F32)
    for hd in range(heads):
        base = hd * (nope + 2 * half)
        x1 = q_t[base + nope:base + nope + half]
        x2 = q_t[base + nope + half:base + nope + 2 * half]
        full = jnp.concatenate(
            [q_t[base:base + nope], x1 * cos - x2 * sin, x2 * cos + x1 * sin, zeros], axis=0)
        qt_out[0, hd] = full.astype(BF16)


def _q_call(x, pos_row, g, w_dq, gq, w_uqt, freq_col, *, heads, nope, half, qd, scale, tm=1024):
    bsz, seq, d = x.shape
    kern = functools.partial(_q_kernel, heads=heads, nope=nope, half=half, qd=qd, scale=scale)
    return pl.pallas_call(
        kern, grid=(bsz, seq // tm),
        in_specs=[pl.BlockSpec((1, tm, d), lambda b, s: (b, s, 0)),
                  pl.BlockSpec((1, 1, tm), lambda b, s: (b, 0, s)),
                  _const_spec(g.shape), _const_spec(w_dq.shape), _const_spec(gq.shape),
                  _const_spec(w_uqt.shape), _const_spec(freq_col.shape)],
        out_specs=pl.BlockSpec((1, heads, qd, tm), lambda b, s: (b, 0, 0, s)),
        out_shape=jax.ShapeDtypeStruct((bsz, heads, qd, seq), BF16),
        compiler_params=_params(2), name="mla_q")(x, pos_row, g, w_dq, gq, w_uqt, freq_col)


def _attn_kernel(qt_ref, kn_ref, kr_ref, vt_ref, o_ref, s0, s1, p0, p1, acc_ref, *, tq, tk, dv):
    for hd in range(qt_ref.shape[1]):
        for qi in range(qt_ref.shape[3] // tq):
            cols = pl.ds(qi * tq, tq)
            _attn_head(qt_ref.at[0, hd, :, cols], kn_ref.at[0, hd], kr_ref.at[0], vt_ref.at[0, hd],
                       o_ref.at[0, hd, :, cols], s0, s1, p0, p1, acc_ref, qi, tq=tq, tk=tk, dv=dv)


def _attn_head(qt_ref, kn_ref, kr_ref, vt_ref, o_ref, s0, s1, p0, p1, acc_ref, qi, *, tq, tk, dv):

    def key_block(j):
        rows = pl.ds(pl.multiple_of(j * tk, tk), tk)
        return jnp.concatenate([kn_ref[rows, :], kr_ref[rows, :]], axis=1)

    def value_block(j):
        return vt_ref[:, pl.ds(pl.multiple_of(j * tk, tk), tk)]

    def scores(j, s_out):
        s = _dot(key_block(j), qt_ref[...])
        s_out[...] = s
        return jnp.max(s, axis=0, keepdims=True)

    def probabilities(s, m, m_blk):
        m_new = jnp.maximum(m, m_blk)
        return m_new, jnp.exp2(m - m_new), jnp.exp2((s - m_new).astype(BF16))

    def softmax_step(s_in, p_out, m, m_blk):
        m_new, alpha, p = probabilities(s_in[...], m, m_blk)
        p_out[...] = p
        return m_new, alpha

    def accumulate(j, p_in, alpha):
        acc_ref[...] = alpha * acc_ref[...] + _dot(value_block(j), p_in[...])

    def body(j, carry):
        a_prev, m, mb0 = carry
        mb1 = scores(2 * j + 1, s1)
        m, a0 = softmax_step(s0, p0, m, mb0)
        accumulate(jnp.maximum(2 * j - 1, 0), p1, a_prev)
        mb0 = scores(2 * j + 2, s0)
        m, a1 = softmax_step(s1, p1, m, mb1)
        accumulate(2 * j, p0, a0)
        return a1, m, mb0

    n_diag = tq // tk
    n_full = n_diag * qi
    acc_ref[...] = jnp.zeros_like(acc_ref)
    m, mb0 = jnp.full((1, tq), NEG_BIG, F32), scores(0, s0)
    if n_full:
        p1[...] = jnp.zeros_like(p1)
        a_prev, m, _ = lax.fori_loop(0, n_full // 2, body, (jnp.ones((1, tq), F32), m, mb0))
        accumulate(n_full - 1, p1, a_prev)

    causal = (lax.broadcasted_iota(jnp.int32, (tk, tk), 0)
              <= lax.broadcasted_iota(jnp.int32, (tk, tk), 1))
    for d in range(n_diag):
        lo = d * tk
        s = s0[...] if d == 0 else _dot(key_block(n_full + d), qt_ref[:, lo:])
        masked = jnp.where(causal, s[:, :tk], NEG_BIG)
        s = jnp.concatenate([masked, s[:, tk:]], axis=1) if lo + tk < tq else masked
        m_d, a_d, p_d = probabilities(s, m[:, lo:], jnp.max(s, axis=0, keepdims=True))
        m = jnp.concatenate([m[:, :lo], m_d], axis=1) if lo else m_d
        acc_ref[:, lo:] = a_d * acc_ref[:, lo:] + _dot(value_block(n_full + d), p_d)
    acc = acc_ref[...]
    o_ref[...] = (acc[:dv] * (1.0 / acc[dv:dv + 1])).astype(BF16)


def _attn_call(qt, k_nope, k_rope, vt, *, tq=2048, tk=512, hps=1):
    assert (tq // tk) % 2 == 0
    bsz, heads, qd, seq = qt.shape
    vrows = vt.shape[2]
    dv = vrows - BF16_SUBLANES
    assert k_nope.shape[3] + k_rope.shape[2] == qd
    kern = functools.partial(_attn_kernel, tq=tq, tk=tk, dv=dv)
    return pl.pallas_call(
        kern, grid=(bsz, heads // hps, 1),
        in_specs=[pl.BlockSpec((1, hps, qd, seq), lambda b, h, i: (b, h, 0, 0)),
                  pl.BlockSpec((1, hps, seq, k_nope.shape[3]), lambda b, h, i: (b, h, 0, 0)),
                  pl.BlockSpec((1, seq, k_rope.shape[2]), lambda b, h, i: (b, 0, 0)),
                  pl.BlockSpec((1, hps, vrows, seq), lambda b, h, i: (b, h, 0, 0))],
        out_specs=pl.BlockSpec((1, hps, dv, seq), lambda b, h, i: (b, h, 0, 0)),
        out_shape=jax.ShapeDtypeStruct((bsz, heads, dv, seq), BF16),
        scratch_shapes=[pltpu.VMEM((tk, tq), F32), pltpu.VMEM((tk, tq), F32),
                        pltpu.VMEM((tk, tq), BF16), pltpu.VMEM((tk, tq), BF16),
                        pltpu.VMEM((vrows, tq), F32)],
        compiler_params=_params(3), name="mla_attention")(qt, k_nope, k_rope, vt)


def kernel(x, p, positions, norm_gains, ffn_w_in, ffn_w_out, ple_w_gate, ple_w_in, a_w_in,
           a_lb_logits, a_out_gain, a_w_out, kv_norm_in, kv_w_down, kv_latent_norm, kv_w_up,
           b_w_dq, b_q_norm, b_w_uq, b_w_out, final_norm):
    bsz, seq, d = x.shape
    depth = norm_gains.shape[0]
    n_a = a_w_in.shape[0]
    t = bsz * seq
    bf = lambda w: w.astype(BF16)
    row = lambda g: g.reshape(1, -1).astype(F32)

    kv_lora = kv_latent_norm.shape[0]
    rope = kv_w_down.shape[1] - kv_lora
    half = rope // 2
    vdim = 128
    heads = b_w_out.shape[1] // vdim
    nope = kv_w_up.shape[1] // heads - vdim
    qd = 256
    assert nope == 128 and rope == 64 and b_w_uq.shape[2] == heads * (nope + rope)

    inv_freq = 1.0 / (ROPE_THETA ** (jnp.arange(0, rope, 2, dtype=F32) / rope))
    freq_col = inv_freq.reshape(half, 1)
    pos_row = positions.reshape(bsz, 1, seq)

    wd_c = bf(kv_w_down[:, :kv_lora])
    wd_r = bf(jnp.concatenate([kv_w_down[:, kv_lora:]] * (128 // rope), axis=1))
    w_up = kv_w_up.reshape(kv_lora, heads, nope + vdim)
    wk = bf(w_up[:, :, :nope].reshape(kv_lora, heads * nope))
    wvt = bf(w_up[:, :, nope:].reshape(kv_lora, heads * vdim).T)

    def shared_kv(xs):
        return _kv_call(xs.reshape(bsz, seq, d), pos_row, row(kv_norm_in), wd_c, wd_r,
                        row(kv_latent_norm), wk, wvt, freq_col,
                        heads=heads, nope=nope, vdim=vdim, half=half)

    ffn_in, ffn_out = bf(ffn_w_in), bf(ffn_w_out)
    ple_gate, ple_in = bf(ple_w_gate), bf(ple_w_in)
    gains = norm_gains.astype(F32).reshape(depth, norm_gains.shape[1], 1, d)
    p_tokens = p.reshape(depth, t, -1)

    xf = x.reshape(t, d)
    shared = shared_kv(xf) if n_a == 0 else None
    attn_scale = float(nope + rope) ** -0.5 * LOG2_E
    for li in range(depth):
        x1 = _block_call(xf, (gains, (li, 0)), (ffn_in, (li, 0)), (ffn_out, (li, 0)))
        if li < n_a:
            o = _hgrn_mixer_call(x1.reshape(bsz, seq, d), row(norm_gains[li, 1]), bf(a_w_in[li]),
                                 a_lb_logits.astype(F32), row(a_out_gain[li]), layer=li)
            pre = (o, bf(a_w_out[li]), "groups")
        else:
            bi = li - n_a
            w_uqt = bf(b_w_uq[bi].T)
            qt = _q_call(x1.reshape(bsz, seq, d), pos_row, row(norm_gains[li, 1]),
                         bf(b_w_dq[bi]), row(b_q_norm[bi]), w_uqt, freq_col, heads=heads,
                         nope=nope, half=half, qd=qd, scale=attn_scale)
            o = _attn_call(qt, *shared)
            pre = (o, bf(b_w_out[bi]), "transposed")
        ple = ((gains, (li, 3)), (ple_gate, (li,)), (p_tokens, li), (ple_in, (li,)))
        final = row(final_norm) if li == depth - 1 else None
        xf = _block_call(x1, (gains, (li, 2)), (ffn_in, (li, 1)), (ffn_out, (li, 1)),
                         pre=pre, ple=ple, final=final)
        if li == n_a - 1:
            shared = shared_kv(xf)
    return xf.reshape(bsz, seq, d)
```

```python
import functools
import math

import jax
import jax.numpy as jnp
from jax import lax
from jax.experimental import pallas as pl
from jax.experimental.pallas import tpu as pltpu

F32 = jnp.float32
BF16 = jnp.bfloat16

EPS = 1e-6
ROPE_THETA = 10000.0
A_CHUNK = 64
A_SUB = 16
A_SAFE_LOG2_DECAY = 96.0
NEG_BIG = -1e30
LOG2_E = math.log2(math.e)

V7X_VMEM_BYTES = 64 * 1024 * 1024
VMEM_LIMIT = V7X_VMEM_BYTES * 7 // 8
LANES = 128
F32_SUBLANES = 8
BF16_SUBLANES = 16

TOKEN_TILE = 512
PROJ_TILE = 1024
HGRN_BLOCK = 256
ATTN_Q_TILE, ATTN_K_TILE = 2048, 512


def _rms(x, g):
    ms = jnp.mean(x * x, axis=-1, keepdims=True)
    return x * lax.rsqrt(ms + EPS) * g


def _sigmoid(x):
    return 1.0 / (1.0 + jnp.exp(-x))


def _dot(a, b):
    return jnp.dot(a, b, preferred_element_type=F32)


def _dot_nt(a, b):
    return lax.dot_general(a, b, (((1,), (1,)), ((), ())), preferred_element_type=F32)


def _dot_tn(a, b):
    return lax.dot_general(a, b, (((0,), (0,)), ((), ())), preferred_element_type=F32)


def _const_spec(shape):
    nd = len(shape)
    return pl.BlockSpec(shape, lambda *_: (0,) * nd, pipeline_mode=pl.Buffered(1))


def _pick(w):
    arr, lead = w if isinstance(w, tuple) else (w, ())
    rest = arr.shape[len(lead):]
    index = tuple(lead) + (0,) * len(rest)
    spec = pl.BlockSpec((None,) * len(lead) + rest, lambda *_: index,
                        pipeline_mode=pl.Buffered(1))
    return arr, spec


def _params(n_axes):
    return pltpu.CompilerParams(
        dimension_semantics=("arbitrary",) * n_axes, vmem_limit_bytes=VMEM_LIMIT)


def _block_kernel(*refs, d_ff, pre_layout, has_ple, has_final):
    it = iter(refs)
    x_ref = next(it)
    if pre_layout:
        y_ref, wpre_ref = next(it), next(it)
    g_ref, win_ref, wout_ref = next(it), next(it), next(it)
    if has_ple:
        gp_ref, wg_ref, p_ref, wp_ref = next(it), next(it), next(it), next(it)
    if has_final:
        gf_ref = next(it)
    o_ref = next(it)

    x = x_ref[...]
    if pre_layout == "groups":
        y = jnp.concatenate([y_ref[0, gi] for gi in range(y_ref.shape[1])], axis=1)
        x = x + _dot(y, wpre_ref[...])
    elif pre_layout == "transposed":
        n_g, gw, tm = y_ref.shape[1:]
        x = x + _dot_tn(y_ref[0].reshape(n_g * gw, tm), wpre_ref[...])
    h = _rms(x, g_ref[...]).astype(BF16)
    gate = _dot(h, win_ref[:, :d_ff])
    up = _dot(h, win_ref[:, d_ff:])
    act = (gate * _sigmoid(gate) * up).astype(BF16)
    x = x + 0.5 * _dot(act, wout_ref[...])
    if has_ple:
        hg = _rms(x, gp_ref[...]).astype(BF16)
        emb_gate = _sigmoid(_dot(hg, wg_ref[...]))
        emb = _dot(p_ref[...].astype(BF16), wp_ref[...])
        x = x + emb_gate * emb
    if has_final:
        x = _rms(x, gf_ref[...])
    o_ref[...] = x


def _block_call(x, g, w_in, w_out, *, pre=None, ple=None, final=None, tm=TOKEN_TILE):
    t, d = x.shape
    row = lambda i: (i, 0)
    args, specs = [x], [pl.BlockSpec((tm, d), row)]

    def add(w):
        arr, spec = _pick(w)
        args.append(arr)
        specs.append(spec)
        return spec.block_shape

    pre_layout = None
    if pre is not None:
        y, w_pre, pre_layout = pre
        args.append(y)
        steps = (t // y.shape[0]) // tm
        if pre_layout == "groups":
            specs.append(pl.BlockSpec((1, y.shape[1], tm, y.shape[3]),
                                      lambda i: (i // steps, 0, i % steps, 0)))
        else:
            specs.append(pl.BlockSpec((1, y.shape[1], y.shape[2], tm),
                                      lambda i: (i // steps, 0, 0, i % steps)))
        add(w_pre)
    add(g)
    add(w_in)
    d_ff = add(w_out)[-2]
    if ple is not None:
        gp, wg, p, wp = ple
        add(gp)
        add(wg)
        args.append(p[0])
        specs.append(pl.BlockSpec((None, tm, p[0].shape[2]), lambda i: (p[1], i, 0)))
        add(wp)
    if final is not None:
        add(final)
    kern = functools.partial(_block_kernel, d_ff=d_ff, pre_layout=pre_layout,
                             has_ple=ple is not None, has_final=final is not None)
    return pl.pallas_call(
        kern, grid=(t // tm,), in_specs=specs, out_specs=pl.BlockSpec((tm, d), row),
        out_shape=jax.ShapeDtypeStruct((t, d), F32), compiler_params=_params(1),
        name="token_block")(*args)


def _hgrn_rec_constants(ts, dk):
    incl = jnp.arange(ts)[None, :] <= jnp.arange(ts)[:, None]
    prefix = jnp.concatenate([incl, incl, incl], axis=1).astype(BF16)
    col = jnp.arange(ts)[None, :]
    reduce_j = (jnp.arange(A_SUB * dk)[:, None] // dk) == (col % A_SUB)
    return prefix, reduce_j.astype(BF16)


def _hgrn_rec_kernel(x_ref, g_ref, w_ref, lbl_ref, gain_ref, prefix_ref, reduce_ref,
                     o_ref, st_ref, k_scr, bs_scr, e_scr, a_scr, *, ts, layer, q_scale):
    @pl.when(pl.program_id(0) == 0)
    def _():
        st_ref[...] = jnp.zeros_like(st_ref)

    nb, dk = x_ref.shape[0], gain_ref.shape[-1]
    n_heads = st_ref.shape[0] // nb
    width = n_heads * dk
    chains = [(b * n_heads + hd, slice(b * ts, (b + 1) * ts), slice(hd * dk, (hd + 1) * dk))
              for b in range(nb) for hd in range(n_heads)]

    h = _rms(x_ref[...].reshape(nb * ts, -1), g_ref[...]).astype(BF16)
    q = _dot(h, w_ref[:, 0 * width:1 * width]) * q_scale
    f = _dot(h, w_ref[:, 1 * width:2 * width])
    v = _dot(h, w_ref[:, 2 * width:3 * width]).astype(BF16)
    out_gate = _sigmoid(_dot(h, w_ref[:, 3 * width:4 * width]))
    lbl = lbl_ref[...]
    e = jnp.exp(lbl - jnp.max(lbl, axis=0, keepdims=True))
    sm = e / jnp.sum(e, axis=0, keepdims=True)
    lb = jnp.sum(sm[:layer + 1], axis=0, keepdims=True)
    k = (1.0 - lb) * _sigmoid(-f)
    g = jnp.log(lb + (1.0 - lb) * _sigmoid(f)) * LOG2_E

    g1 = g.astype(BF16)
    r1 = g - g1.astype(F32)
    g2 = r1.astype(BF16)
    g3 = (r1 - g2.astype(F32)).astype(BF16)
    big_b = [_dot(prefix_ref[...], jnp.concatenate([g1[rows], g2[rows], g3[rows]], axis=0))
             for rows in (slice(b * ts, (b + 1) * ts) for b in range(nb))]
    parts = [_hgrn_decays(q[rows, ln], k[rows, ln], big_b[i // n_heads][:, ln], ts=ts)
             for i, rows, ln in chains]
    steepest = functools.reduce(jnp.maximum, [jnp.max(-pt["b_sub"]) for pt in parts])

    @pl.when(steepest <= A_SAFE_LOG2_DECAY)
    def _():
        for (i, _, _), pt in zip(chains, parts):
            k_up = (pt["k"] * jnp.exp2(-pt["b_sub"])).astype(BF16)
            a_scr[i] = _dot_nt(pt["q_sub"], k_up)

    @pl.when(steepest > A_SAFE_LOG2_DECAY)
    def _():
        for (i, _, _), pt in zip(chains, parts):
            a_scr[i] = _hgrn_exact_sub_scores(pt["q"], pt["k"], pt["b_sub"], reduce_ref[...],
                                               k_scr, bs_scr, e_scr, ts=ts)

    for (i, rows, ln), pt in zip(chains, parts):
        o, st_ref[i] = _hgrn_outputs(pt, v[rows, ln], st_ref[i], a_scr[i], ts=ts)
        o_ref[i // n_heads, 0, :, ln] = (_rms(o, gain_ref[...]) * out_gate[rows, ln]).astype(BF16)


def _hgrn_decays(q, k, big_b, *, ts):
    C, c = A_CHUNK, A_SUB
    nc, dk = ts // C, q.shape[-1]
    b_last = big_b[ts - 1:ts]
    after = b_last - big_b

    def since_start_of(size):
        groups = [big_b[:size]] + [big_b[lo:lo + size] - big_b[lo - 1:lo]
                                   for lo in range(size, ts, size)]
        return jnp.concatenate(groups, axis=0)

    b_chunk = since_start_of(C)
    b_sub = since_start_of(c)
    to_chunk_end = jnp.concatenate(
        [big_b[lo + C - 1:lo + C] - big_b[lo:lo + C] for lo in range(0, ts, C)],
        axis=0).reshape(nc, C, dk)

    return dict(
        q=q, k=k, b_last=b_last, after=after, b_sub=b_sub, to_chunk_end=to_chunk_end,
        q_blk=(q * jnp.exp2(big_b)).astype(BF16),
        k_blk=(k * jnp.exp2(after)).astype(BF16),
        q_chunk=(q * jnp.exp2(b_chunk)).astype(BF16),
        q_sub=(q * jnp.exp2(b_sub)).astype(BF16))


def _hgrn_outputs(pt, v, st, a_rep, *, ts):
    C, c = A_CHUNK, A_SUB
    nc, nsub = ts // C, C // c
    k, after, to_chunk_end = pt["k"], pt["after"], pt["to_chunk_end"]
    q_chunk, q_sub = pt["q_chunk"], pt["q_sub"]
    dk = k.shape[-1]

    zeros2 = lambda n: jnp.zeros((n, dk), BF16)
    qs_parts, ks_parts = [], []
    for n in range(1, nc):
        lo, hi = n * C, (n + 1) * C
        decay = jnp.exp2(after[:lo] - after[lo - 1:lo])
        ks_parts.append(jnp.concatenate([(k[:lo] * decay).astype(BF16), zeros2(ts - lo)], axis=0))
        pieces = [zeros2(lo), q_chunk[lo:hi]] + ([zeros2(ts - hi)] if hi < ts else [])
        qs_parts.append(jnp.concatenate(pieces, axis=0))
    a_chunks = _dot_nt(jnp.concatenate(qs_parts, axis=1), jnp.concatenate(ks_parts, axis=1))

    k3 = k.reshape(nc, C, dk)
    qs3 = q_sub.reshape(nc, C, dk)
    zeros3 = lambda n: jnp.zeros((nc, n, dk), BF16)
    qs_parts, ks_parts = [], []
    for i in range(1, nsub):
        lo, hi = i * c, (i + 1) * c
        decay = jnp.exp2(to_chunk_end[:, :lo] - to_chunk_end[:, lo - 1:lo])
        ks_parts.append(jnp.concatenate([(k3[:, :lo] * decay).astype(BF16), zeros3(C - lo)], axis=1))
        pieces = [zeros3(lo), qs3[:, lo:hi]] + ([zeros3(C - hi)] if hi < C else [])
        qs_parts.append(jnp.concatenate(pieces, axis=1))
    a_subs = _dot_nt(jnp.concatenate(qs_parts, axis=2).reshape(ts, (nsub - 1) * dk),
                     jnp.concatenate(ks_parts, axis=2).reshape(ts, (nsub - 1) * dk))

    rr = lax.broadcasted_iota(jnp.int32, (ts, ts), 0)
    ll = lax.broadcasted_iota(jnp.int32, (ts, ts), 1)
    lc, ls = C.bit_length() - 1, c.bit_length() - 1
    same_chunk = (rr >> lc) == (ll >> lc)
    own_sub = ((rr >> ls) == (ll >> ls)) & ((ll & (c - 1)) <= (rr & (c - 1)))
    a = jnp.where(own_sub, a_rep, 0.0) + jnp.where(same_chunk, a_subs, 0.0) + a_chunks

    o = _dot(a.astype(BF16), v) + _dot_nt(pt["q_blk"], st.astype(BF16))
    return o, st * jnp.exp2(pt["b_last"]) + _dot_tn(v, pt["k_blk"])


def _hgrn_exact_sub_scores(q, k, b_sub, reduce_j, k_scr, bs_scr, e_scr, *, ts):
    c, dk = A_SUB, q.shape[-1]
    k_scr[...] = k
    bs_scr[...] = b_sub
    zero_tile = jnp.zeros((F32_SUBLANES, dk), F32)
    for n in range(ts // c):
        tiles = [slice(n * c + lo, n * c + lo + F32_SUBLANES) for lo in range(0, c, F32_SUBLANES)]
        for j in range(c):
            row = slice(n * c + j, n * c + j + 1)
            k_j, b_j = k_scr[row, :], bs_scr[row, :]
            e = [q[t] * k_j * jnp.exp2(jnp.minimum(b_sub[t] - b_j, 0.0))
                 if t.stop > n * c + j else zero_tile for t in tiles]
            e_scr[n * c:(n + 1) * c, j * dk:(j + 1) * dk] = jnp.concatenate(e, axis=0).astype(BF16)
    return _dot(e_scr[...], reduce_j)


def _hgrn_mixer_call(x, g, w_in, lb_logits, gain, *, layer, ts=HGRN_BLOCK):
    bsz, seq, d = x.shape
    width = w_in.shape[1] // 4
    dk = gain.shape[-1]
    heads = width // dk
    prefix, reduce_j = _hgrn_rec_constants(ts, dk)
    kern = functools.partial(_hgrn_rec_kernel, ts=ts, layer=layer, q_scale=float(dk) ** -0.5)
    consts = (g, w_in, lb_logits, gain, prefix, reduce_j)
    return pl.pallas_call(
        kern, grid=(seq // ts,),
        in_specs=[pl.BlockSpec((bsz, ts, d), lambda s: (0, s, 0))]
        + [_const_spec(c.shape) for c in consts],
        out_specs=pl.BlockSpec((bsz, 1, ts, width), lambda s: (0, 0, s, 0)),
        out_shape=jax.ShapeDtypeStruct((bsz, 1, seq, width), BF16),
        scratch_shapes=[pltpu.VMEM((bsz * heads, dk, dk), F32),
                        pltpu.VMEM((ts, dk), F32),
                        pltpu.VMEM((ts, dk), F32),
                        pltpu.VMEM((ts, A_SUB * dk), BF16),
                        pltpu.VMEM((bsz * heads, ts, ts), F32)],
        compiler_params=_params(1), name="hgrn_mixer")(x, *consts)


def _rope_tables_lanes(pos_row, inv_freq_col, half):
    ang = inv_freq_col * pos_row.astype(F32)
    cos, sin = jnp.cos(ang), jnp.sin(ang)
    zeros = jnp.zeros((LANES - 2 * half, ang.shape[1]), F32)
    c_tab = jnp.transpose(jnp.concatenate([cos, cos, zeros], axis=0))
    s_tab = jnp.transpose(jnp.concatenate([-sin, sin, zeros], axis=0))
    return c_tab, s_tab


def _kv_kernel(x_ref, pos_ref, g_ref, wdc_ref, wdr_ref, gl_ref, wk_ref, wvt_ref, freq_ref,
               kn_out, kr_out, vt_out, *, heads, nope, vdim, half):
    h = _rms(x_ref[0], g_ref[...]).astype(BF16)
    c_kv = _rms(_dot(h, wdc_ref[...]), gl_ref[...]).astype(BF16)
    kr = _dot(h, wdr_ref[...])
    c_tab, s_tab = _rope_tables_lanes(pos_ref[0], freq_ref[...], half)
    kr_out[0] = (kr * c_tab + pltpu.roll(kr, half, 1) * s_tab).astype(BF16)
    k_nope = _dot(c_kv, wk_ref[...])
    v_t = _dot_nt(wvt_ref[...], c_kv)
    ones = jnp.ones((vt_out.shape[2] - vdim, v_t.shape[1]), BF16)
    for hd in range(heads):
        kn_out[0, hd] = k_nope[:, hd * nope:(hd + 1) * nope].astype(BF16)
        vt_out[0, hd, 0:vdim, :] = v_t[hd * vdim:(hd + 1) * vdim].astype(BF16)
        vt_out[0, hd, vdim:, :] = ones


def _kv_call(x, pos_row, g, wd_c, wd_r, gl, wk, wvt, freq_col, *, heads, nope, vdim, half,
             tm=PROJ_TILE):
    bsz, seq, d = x.shape
    kern = functools.partial(_kv_kernel, heads=heads, nope=nope, vdim=vdim, half=half)
    vrows = vdim + BF16_SUBLANES
    return pl.pallas_call(
        kern, grid=(bsz, seq // tm),
        in_specs=[pl.BlockSpec((1, tm, d), lambda b, s: (b, s, 0)),
                  pl.BlockSpec((1, 1, tm), lambda b, s: (b, 0, s)),
                  _const_spec(g.shape), _const_spec(wd_c.shape), _const_spec(wd_r.shape),
                  _const_spec(gl.shape), _const_spec(wk.shape), _const_spec(wvt.shape),
                  _const_spec(freq_col.shape)],
        out_specs=[pl.BlockSpec((1, heads, tm, nope), lambda b, s: (b, 0, s, 0)),
                   pl.BlockSpec((1, tm, LANES), lambda b, s: (b, s, 0)),
                   pl.BlockSpec((1, heads, vrows, tm), lambda b, s: (b, 0, 0, s))],
        out_shape=[jax.ShapeDtypeStruct((bsz, heads, seq, nope), BF16),
                   jax.ShapeDtypeStruct((bsz, seq, LANES), BF16),
                   jax.ShapeDtypeStruct((bsz, heads, vrows, seq), BF16)],
        compiler_params=_params(2), name="mla_shared_kv")(
            x, pos_row, g, wd_c, wd_r, gl, wk, wvt, freq_col)


def _q_kernel(x_ref, pos_ref, g_ref, wdq_ref, gq_ref, wuqt_ref, freq_ref, qt_out,
              *, heads, nope, half, qd, scale):
    h = _rms(x_ref[0], g_ref[...]).astype(BF16)
    c_q = (_rms(_dot(h, wdq_ref[...]), gq_ref[...]) * scale).astype(BF16)
    q_t = _dot_nt(wuqt_ref[...], c_q)
    ang = freq_ref[...] * pos_ref[0].astype(F32)
    cos, sin = jnp.cos(ang), jnp.sin(ang)
    zeros = jnp.zeros((qd - nope - 2 * half, q_t.shape[1]), F32)
    for hd in range(heads):
        base = hd * (nope + 2 * half)
        x1 = q_t[base + nope:base + nope + half]
        x2 = q_t[base + nope + half:base + nope + 2 * half]
        full = jnp.concatenate(
            [q_t[base:base + nope], x1 * cos - x2 * sin, x2 * cos + x1 * sin, zeros], axis=0)
        qt_out[0, hd] = full.astype(BF16)


def _q_call(x, pos_row, g, w_dq, gq, w_uqt, freq_col, *, heads, nope, half, qd, scale,
            tm=PROJ_TILE):
    bsz, seq, d = x.shape
    kern = functools.partial(_q_kernel, heads=heads, nope=nope, half=half, qd=qd, scale=scale)
    return pl.pallas_call(
        kern, grid=(bsz, seq // tm),
        in_specs=[pl.BlockSpec((1, tm, d), lambda b, s: (b, s, 0)),
                  pl.BlockSpec((1, 1, tm), lambda b, s: (b, 0, s)),
                  _const_spec(g.shape), _const_spec(w_dq.shape), _const_spec(gq.shape),
                  _const_spec(w_uqt.shape), _const_spec(freq_col.shape)],
        out_specs=pl.BlockSpec((1, heads, qd, tm), lambda b, s: (b, 0, 0, s)),
        out_shape=jax.ShapeDtypeStruct((bsz, heads, qd, seq), BF16),
        compiler_params=_params(2), name="mla_q")(x, pos_row, g, w_dq, gq, w_uqt, freq_col)


def _attn_kernel(qt_ref, kn_ref, kr_ref, vt_ref, o_ref, s0, s1, p0, p1, acc_ref, *, tq, tk, dv):
    for hd in range(qt_ref.shape[1]):
        for qi in range(qt_ref.shape[3] // tq):
            cols = pl.ds(qi * tq, tq)
            _attn_head(qt_ref.at[0, hd, :, cols], kn_ref.at[0, hd], kr_ref.at[0], vt_ref.at[0, hd],
                       o_ref.at[0, hd, :, cols], s0, s1, p0, p1, acc_ref, qi, tq=tq, tk=tk, dv=dv)


def _attn_head(qt_ref, kn_ref, kr_ref, vt_ref, o_ref, s0, s1, p0, p1, acc_ref, qi, *, tq, tk, dv):

    def key_block(j):
        rows = pl.ds(pl.multiple_of(j * tk, tk), tk)
        return jnp.concatenate([kn_ref[rows, :], kr_ref[rows, :]], axis=1)

    def value_block(j):
        return vt_ref[:, pl.ds(pl.multiple_of(j * tk, tk), tk)]

    def scores(j, s_out):
        s = _dot(key_block(j), qt_ref[...])
        s_out[...] = s
        return jnp.max(s, axis=0, keepdims=True)

    def probabilities(s, m, m_blk):
        m_new = jnp.maximum(m, m_blk)
        return m_new, jnp.exp2(m - m_new), jnp.exp2((s - m_new).astype(BF16))

    def softmax_step(s_in, p_out, m, m_blk):
        m_new, alpha, p = probabilities(s_in[...], m, m_blk)
        p_out[...] = p
        return m_new, alpha

    def accumulate(j, p_in, alpha):
        acc_ref[...] = alpha * acc_ref[...] + _dot(value_block(j), p_in[...])

    def body(j, carry):
        a_prev, m, mb0 = carry
        mb1 = scores(2 * j + 1, s1)
        m, a0 = softmax_step(s0, p0, m, mb0)
        accumulate(jnp.maximum(2 * j - 1, 0), p1, a_prev)
        mb0 = scores(2 * j + 2, s0)
        m, a1 = softmax_step(s1, p1, m, mb1)
        accumulate(2 * j, p0, a0)
        return a1, m, mb0

    n_diag = tq // tk
    n_full = n_diag * qi
    acc_ref[...] = jnp.zeros_like(acc_ref)
    m, mb0 = jnp.full((1, tq), NEG_BIG, F32), scores(0, s0)
    if n_full:
        p1[...] = jnp.zeros_like(p1)
        a_prev, m, _ = lax.fori_loop(0, n_full // 2, body, (jnp.ones((1, tq), F32), m, mb0))
        accumulate(n_full - 1, p1, a_prev)

    causal = (lax.broadcasted_iota(jnp.int32, (tk, tk), 0)
              <= lax.broadcasted_iota(jnp.int32, (tk, tk), 1))
    for d in range(n_diag):
        lo = d * tk
        s = s0[...] if d == 0 else _dot(key_block(n_full + d), qt_ref[:, lo:])
        masked = jnp.where(causal, s[:, :tk], NEG_BIG)
        s = jnp.concatenate([masked, s[:, tk:]], axis=1) if lo + tk < tq else masked
        m_d, a_d, p_d = probabilities(s, m[:, lo:], jnp.max(s, axis=0, keepdims=True))
        m = jnp.concatenate([m[:, :lo], m_d], axis=1) if lo else m_d
        acc_ref[:, lo:] = a_d * acc_ref[:, lo:] + _dot(value_block(n_full + d), p_d)
    acc = acc_ref[...]
    o_ref[...] = (acc[:dv] * (1.0 / acc[dv:dv + 1])).astype(BF16)


def _attn_call(qt, k_nope, k_rope, vt, *, tq=ATTN_Q_TILE, tk=ATTN_K_TILE, hps=1):
    assert (tq // tk) % 2 == 0
    bsz, heads, qd, seq = qt.shape
    vrows = vt.shape[2]
    dv = vrows - BF16_SUBLANES
    assert k_nope.shape[3] + k_rope.shape[2] == qd
    kern = functools.partial(_attn_kernel, tq=tq, tk=tk, dv=dv)
    return pl.pallas_call(
        kern, grid=(bsz, heads // hps, 1),
        in_specs=[pl.BlockSpec((1, hps, qd, seq), lambda b, h, i: (b, h, 0, 0)),
                  pl.BlockSpec((1, hps, seq, k_nope.shape[3]), lambda b, h, i: (b, h, 0, 0)),
                  pl.BlockSpec((1, seq, k_rope.shape[2]), lambda b, h, i: (b, 0, 0)),
                  pl.BlockSpec((1, hps, vrows, seq), lambda b, h, i: (b, h, 0, 0))],
        out_specs=pl.BlockSpec((1, hps, dv, seq), lambda b, h, i: (b, h, 0, 0)),
        out_shape=jax.ShapeDtypeStruct((bsz, heads, dv, seq), BF16),
        scratch_shapes=[pltpu.VMEM((tk, tq), F32), pltpu.VMEM((tk, tq), F32),
                        pltpu.VMEM((tk, tq), BF16), pltpu.VMEM((tk, tq), BF16),
                        pltpu.VMEM((vrows, tq), F32)],
        compiler_params=_params(3), name="mla_attention")(qt, k_nope, k_rope, vt)


def kernel(x, p, positions, norm_gains, ffn_w_in, ffn_w_out, ple_w_gate, ple_w_in, a_w_in,
           a_lb_logits, a_out_gain, a_w_out, kv_norm_in, kv_w_down, kv_latent_norm, kv_w_up,
           b_w_dq, b_q_norm, b_w_uq, b_w_out, final_norm):
    bsz, seq, d = x.shape
    depth = norm_gains.shape[0]
    n_a = a_w_in.shape[0]
    t = bsz * seq
    bf = lambda w: w.astype(BF16)
    row = lambda g: g.reshape(1, -1).astype(F32)

    kv_lora = kv_latent_norm.shape[0]
    rope = kv_w_down.shape[1] - kv_lora
    half = rope // 2
    heads = (b_w_uq.shape[2] + b_w_out.shape[1] - kv_w_up.shape[1]) // rope
    vdim = b_w_out.shape[1] // heads
    nope = kv_w_up.shape[1] // heads - vdim
    qd = nope + LANES
    assert nope == LANES and vdim == LANES and rope <= LANES // 2

    inv_freq = 1.0 / (ROPE_THETA ** (jnp.arange(0, rope, 2, dtype=F32) / rope))
    freq_col = inv_freq.reshape(half, 1)
    pos_row = positions.reshape(bsz, 1, seq)

    wd_c = bf(kv_w_down[:, :kv_lora])
    wd_r = bf(jnp.concatenate([kv_w_down[:, kv_lora:]] * (LANES // rope), axis=1))
    w_up = kv_w_up.reshape(kv_lora, heads, nope + vdim)
    wk = bf(w_up[:, :, :nope].reshape(kv_lora, heads * nope))
    wvt = bf(w_up[:, :, nope:].reshape(kv_lora, heads * vdim).T)

    def shared_kv(xs):
        return _kv_call(xs.reshape(bsz, seq, d), pos_row, row(kv_norm_in), wd_c, wd_r,
                        row(kv_latent_norm), wk, wvt, freq_col,
                        heads=heads, nope=nope, vdim=vdim, half=half)

    ffn_in, ffn_out = bf(ffn_w_in), bf(ffn_w_out)
    ple_gate, ple_in = bf(ple_w_gate), bf(ple_w_in)
    gains = norm_gains.astype(F32).reshape(depth, norm_gains.shape[1], 1, d)
    p_tokens = p.reshape(depth, t, -1)

    xf = x.reshape(t, d)
    shared = shared_kv(xf) if n_a == 0 else None
    attn_scale = float(nope + rope) ** -0.5 * LOG2_E
    for li in range(depth):
        x1 = _block_call(xf, (gains, (li, 0)), (ffn_in, (li, 0)), (ffn_out, (li, 0)))
        if li < n_a:
            o = _hgrn_mixer_call(x1.reshape(bsz, seq, d), row(norm_gains[li, 1]), bf(a_w_in[li]),
                                 a_lb_logits.astype(F32), row(a_out_gain[li]), layer=li)
            pre = (o, bf(a_w_out[li]), "groups")
        else:
            bi = li - n_a
            w_uqt = bf(b_w_uq[bi].T)
            qt = _q_call(x1.reshape(bsz, seq, d), pos_row, row(norm_gains[li, 1]),
                         bf(b_w_dq[bi]), row(b_q_norm[bi]), w_uqt, freq_col, heads=heads,
                         nope=nope, half=half, qd=qd, scale=attn_scale)
            o = _attn_call(qt, *shared)
            pre = (o, bf(b_w_out[bi]), "transposed")
        ple = ((gains, (li, 3)), (ple_gate, (li,)), (p_tokens, li), (ple_in, (li,)))
        final = row(final_norm) if li == depth - 1 else None
        xf = _block_call(x1, (gains, (li, 2)), (ffn_in, (li, 1)), (ffn_out, (li, 1)),
                         pre=pre, ple=ple, final=final)
        if li == n_a - 1:
            shared = shared_kv(xf)
    return xf.reshape(bsz, seq, d)
```

```python
import functools
import math

import jax
import jax.numpy as jnp
from jax import lax
from jax.experimental import pallas as pl
from jax.experimental.pallas import tpu as pltpu

F32 = jnp.float32
BF16 = jnp.bfloat16

EPS = 1e-6
ROPE_THETA = 10000.0
A_CHUNK = 64
A_SUB = 32
A_SAFE_LOG2_DECAY = 96.0
NEG_BIG = -1e30
LOG2_E = math.log2(math.e)

V7X_VMEM_BYTES = 64 * 1024 * 1024
VMEM_LIMIT = V7X_VMEM_BYTES * 7 // 8
LANES = 128
F32_SUBLANES = 8
BF16_SUBLANES = 16

TOKEN_TILE = 512
PROJ_TILE = 1024
HGRN_BLOCK = 256
ATTN_Q_TILE, ATTN_K_TILE = 2048, 512


def _rms(x, g):
    ms = jnp.mean(x * x, axis=-1, keepdims=True)
    return x * lax.rsqrt(ms + EPS) * g


def _sigmoid(x):
    return 1.0 / (1.0 + jnp.exp(-x))


def _dot(a, b):
    return jnp.dot(a, b, preferred_element_type=F32)


def _dot_nt(a, b):
    return lax.dot_general(a, b, (((1,), (1,)), ((), ())), preferred_element_type=F32)


def _dot_tn(a, b):
    return lax.dot_general(a, b, (((0,), (0,)), ((), ())), preferred_element_type=F32)


def _const_spec(shape):
    nd = len(shape)
    return pl.BlockSpec(shape, lambda *_: (0,) * nd, pipeline_mode=pl.Buffered(1))


def _pick(w):
    arr, lead = w if isinstance(w, tuple) else (w, ())
    rest = arr.shape[len(lead):]
    index = tuple(lead) + (0,) * len(rest)
    spec = pl.BlockSpec((None,) * len(lead) + rest, lambda *_: index,
                        pipeline_mode=pl.Buffered(1))
    return arr, spec


def _params(n_axes):
    return pltpu.CompilerParams(
        dimension_semantics=("arbitrary",) * n_axes, vmem_limit_bytes=VMEM_LIMIT)


def _block_kernel(*refs, d_ff, pre_layout, has_ple, has_final):
    it = iter(refs)
    x_ref = next(it)
    if pre_layout:
        y_ref, wpre_ref = next(it), next(it)
    g_ref, win_ref, wout_ref = next(it), next(it), next(it)
    if has_ple:
        gp_ref, wg_ref, p_ref, wp_ref = next(it), next(it), next(it), next(it)
    if has_final:
        gf_ref = next(it)
    o_ref = next(it)

    x = x_ref[...]
    if pre_layout == "groups":
        y = jnp.concatenate([y_ref[0, gi] for gi in range(y_ref.shape[1])], axis=1)
        x = x + _dot(y, wpre_ref[...])
    elif pre_layout == "transposed":
        n_g, gw, tm = y_ref.shape[1:]
        x = x + _dot_tn(y_ref[0].reshape(n_g * gw, tm), wpre_ref[...])
    h = _rms(x, g_ref[...]).astype(BF16)
    gate = _dot(h, win_ref[:, :d_ff])
    up = _dot(h, win_ref[:, d_ff:])
    act = (gate * _sigmoid(gate) * up).astype(BF16)
    x = x + 0.5 * _dot(act, wout_ref[...])
    if has_ple:
        hg = _rms(x, gp_ref[...]).astype(BF16)
        emb_gate = _sigmoid(_dot(hg, wg_ref[...]))
        emb = _dot(p_ref[...].astype(BF16), wp_ref[...])
        x = x + emb_gate * emb
    if has_final:
        x = _rms(x, gf_ref[...])
    o_ref[...] = x


def _block_call(x, g, w_in, w_out, *, pre=None, ple=None, final=None, tm=TOKEN_TILE):
    t, d = x.shape
    row = lambda i: (i, 0)
    args, specs = [x], [pl.BlockSpec((tm, d), row)]

    def add(w):
        arr, spec = _pick(w)
        args.append(arr)
        specs.append(spec)
        return spec.block_shape

    pre_layout = None
    if pre is not None:
        y, w_pre, pre_layout = pre
        args.append(y)
        steps = (t // y.shape[0]) // tm
        if pre_layout == "groups":
            specs.append(pl.BlockSpec((1, y.shape[1], tm, y.shape[3]),
                                      lambda i: (i // steps, 0, i % steps, 0)))
        else:
            specs.append(pl.BlockSpec((1, y.shape[1], y.shape[2], tm),
                                      lambda i: (i // steps, 0, 0, i % steps)))
        add(w_pre)
    add(g)
    add(w_in)
    d_ff = add(w_out)[-2]
    if ple is not None:
        gp, wg, p, wp = ple
        add(gp)
        add(wg)
        args.append(p[0])
        specs.append(pl.BlockSpec((None, tm, p[0].shape[2]), lambda i: (p[1], i, 0)))
        add(wp)
    if final is not None:
        add(final)
    kern = functools.partial(_block_kernel, d_ff=d_ff, pre_layout=pre_layout,
                             has_ple=ple is not None, has_final=final is not None)
    return pl.pallas_call(
        kern, grid=(t // tm,), in_specs=specs, out_specs=pl.BlockSpec((tm, d), row),
        out_shape=jax.ShapeDtypeStruct((t, d), F32), compiler_params=_params(1),
        name="token_block")(*args)


def _hgrn_rec_constants(ts, dk):
    incl = jnp.arange(ts)[None, :] <= jnp.arange(ts)[:, None]
    prefix = jnp.concatenate([incl, incl, incl], axis=1).astype(BF16)
    col = jnp.arange(ts)[None, :]
    reduce_j = (jnp.arange(A_SUB * dk)[:, None] // dk) == (col % A_SUB)
    return prefix, reduce_j.astype(BF16)


def _hgrn_rec_kernel(x_ref, g_ref, w_ref, lbl_ref, gain_ref, prefix_ref, reduce_ref,
                     o_ref, st_ref, k_scr, bs_scr, e_scr, a_scr, *, ts, layer, q_scale):
    @pl.when(pl.program_id(0) == 0)
    def _():
        st_ref[...] = jnp.zeros_like(st_ref)

    nb, dk = x_ref.shape[0], gain_ref.shape[-1]
    n_heads = st_ref.shape[0] // nb
    width = n_heads * dk
    chains = [(b * n_heads + hd, slice(b * ts, (b + 1) * ts), slice(hd * dk, (hd + 1) * dk))
              for b in range(nb) for hd in range(n_heads)]

    h = _rms(x_ref[...].reshape(nb * ts, -1), g_ref[...]).astype(BF16)
    q = _dot(h, w_ref[:, 0 * width:1 * width]) * q_scale
    f = _dot(h, w_ref[:, 1 * width:2 * width])
    v = _dot(h, w_ref[:, 2 * width:3 * width]).astype(BF16)
    out_gate = _sigmoid(_dot(h, w_ref[:, 3 * width:4 * width]))
    lbl = lbl_ref[...]
    e = jnp.exp(lbl - jnp.max(lbl, axis=0, keepdims=True))
    sm = e / jnp.sum(e, axis=0, keepdims=True)
    lb = jnp.sum(sm[:layer + 1], axis=0, keepdims=True)
    k = (1.0 - lb) * _sigmoid(-f)
    g = jnp.log(lb + (1.0 - lb) * _sigmoid(f)) * LOG2_E

    g1 = g.astype(BF16)
    r1 = g - g1.astype(F32)
    g2 = r1.astype(BF16)
    g3 = (r1 - g2.astype(F32)).astype(BF16)
    big_b = [_dot(prefix_ref[...], jnp.concatenate([g1[rows], g2[rows], g3[rows]], axis=0))
             for rows in (slice(b * ts, (b + 1) * ts) for b in range(nb))]
    parts = [_hgrn_decays(q[rows, ln], k[rows, ln], big_b[i // n_heads][:, ln], ts=ts)
             for i, rows, ln in chains]
    steepest = functools.reduce(jnp.maximum, [jnp.max(-pt["b_sub"]) for pt in parts])

    @pl.when(steepest <= A_SAFE_LOG2_DECAY)
    def _():
        for (i, _, _), pt in zip(chains, parts):
            k_up = (pt["k"] * jnp.exp2(-pt["b_sub"])).astype(BF16)
            a_scr[i] = _dot_nt(pt["q_sub"], k_up)

    @pl.when(steepest > A_SAFE_LOG2_DECAY)
    def _():
        for (i, _, _), pt in zip(chains, parts):
            a_scr[i] = _hgrn_exact_sub_scores(pt["q"], pt["k"], pt["b_sub"], reduce_ref[...],
                                               k_scr, bs_scr, e_scr, ts=ts)

    for (i, rows, ln), pt in zip(chains, parts):
        o, st_ref[i] = _hgrn_outputs(pt, v[rows, ln], st_ref[i], a_scr[i], ts=ts)
        o_ref[i // n_heads, 0, :, ln] = (_rms(o, gain_ref[...]) * out_gate[rows, ln]).astype(BF16)


def _hgrn_decays(q, k, big_b, *, ts):
    C, c = A_CHUNK, A_SUB
    nc, dk = ts // C, q.shape[-1]
    b_last = big_b[ts - 1:ts]
    after = b_last - big_b

    def since_start_of(size):
        groups = [big_b[:size]] + [big_b[lo:lo + size] - big_b[lo - 1:lo]
                                   for lo in range(size, ts, size)]
        return jnp.concatenate(groups, axis=0)

    b_chunk = since_start_of(C)
    b_sub = since_start_of(c)
    to_chunk_end = jnp.concatenate(
        [big_b[lo + C - 1:lo + C] - big_b[lo:lo + C] for lo in range(0, ts, C)],
        axis=0).reshape(nc, C, dk)

    return dict(
        q=q, k=k, b_last=b_last, after=after, b_sub=b_sub, to_chunk_end=to_chunk_end,
        q_blk=(q * jnp.exp2(big_b)).astype(BF16),
        k_blk=(k * jnp.exp2(after)).astype(BF16),
        q_chunk=(q * jnp.exp2(b_chunk)).astype(BF16),
        q_sub=(q * jnp.exp2(b_sub)).astype(BF16))


def _hgrn_outputs(pt, v, st, a_rep, *, ts):
    C, c = A_CHUNK, A_SUB
    nc, nsub = ts // C, C // c
    k, after, to_chunk_end = pt["k"], pt["after"], pt["to_chunk_end"]
    q_chunk, q_sub = pt["q_chunk"], pt["q_sub"]
    dk = k.shape[-1]

    zeros2 = lambda n: jnp.zeros((n, dk), BF16)
    qs_parts, ks_parts = [], []
    for n in range(1, nc):
        lo, hi = n * C, (n + 1) * C
        decay = jnp.exp2(after[:lo] - after[lo - 1:lo])
        ks_parts.append(jnp.concatenate([(k[:lo] * decay).astype(BF16), zeros2(ts - lo)], axis=0))
        pieces = [zeros2(lo), q_chunk[lo:hi]] + ([zeros2(ts - hi)] if hi < ts else [])
        qs_parts.append(jnp.concatenate(pieces, axis=0))
    a_chunks = _dot_nt(jnp.concatenate(qs_parts, axis=1), jnp.concatenate(ks_parts, axis=1))

    k3 = k.reshape(nc, C, dk)
    qs3 = q_sub.reshape(nc, C, dk)
    zeros3 = lambda n: jnp.zeros((nc, n, dk), BF16)
    qs_parts, ks_parts = [], []
    for i in range(1, nsub):
        lo, hi = i * c, (i + 1) * c
        decay = jnp.exp2(to_chunk_end[:, :lo] - to_chunk_end[:, lo - 1:lo])
        ks_parts.append(jnp.concatenate([(k3[:, :lo] * decay).astype(BF16), zeros3(C - lo)], axis=1))
        pieces = [zeros3(lo), qs3[:, lo:hi]] + ([zeros3(C - hi)] if hi < C else [])
        qs_parts.append(jnp.concatenate(pieces, axis=1))
    a_subs = _dot_nt(jnp.concatenate(qs_parts, axis=2).reshape(ts, (nsub - 1) * dk),
                     jnp.concatenate(ks_parts, axis=2).reshape(ts, (nsub - 1) * dk))

    rr = lax.broadcasted_iota(jnp.int32, (ts, ts), 0)
    ll = lax.broadcasted_iota(jnp.int32, (ts, ts), 1)
    lc, ls = C.bit_length() - 1, c.bit_length() - 1
    same_chunk = (rr >> lc) == (ll >> lc)
    own_sub = ((rr >> ls) == (ll >> ls)) & ((ll & (c - 1)) <= (rr & (c - 1)))
    a = jnp.where(own_sub, a_rep, 0.0) + jnp.where(same_chunk, a_subs, 0.0) + a_chunks

    o = _dot(a.astype(BF16), v) + _dot_nt(pt["q_blk"], st.astype(BF16))
    return o, st * jnp.exp2(pt["b_last"]) + _dot_tn(v, pt["k_blk"])


def _hgrn_exact_sub_scores(q, k, b_sub, reduce_j, k_scr, bs_scr, e_scr, *, ts):
    c, dk = A_SUB, q.shape[-1]
    k_scr[...] = k
    bs_scr[...] = b_sub
    zero_tile = jnp.zeros((F32_SUBLANES, dk), F32)
    for n in range(ts // c):
        tiles = [slice(n * c + lo, n * c + lo + F32_SUBLANES) for lo in range(0, c, F32_SUBLANES)]
        for j in range(c):
            row = slice(n * c + j, n * c + j + 1)
            k_j, b_j = k_scr[row, :], bs_scr[row, :]
            e = [q[t] * k_j * jnp.exp2(jnp.minimum(b_sub[t] - b_j, 0.0))
                 if t.stop > n * c + j else zero_tile for t in tiles]
            e_scr[n * c:(n + 1) * c, j * dk:(j + 1) * dk] = jnp.concatenate(e, axis=0).astype(BF16)
    return _dot(e_scr[...], reduce_j)


def _hgrn_mixer_call(x, g, w_in, lb_logits, gain, *, layer, ts=HGRN_BLOCK):
    bsz, seq, d = x.shape
    width = w_in.shape[1] // 4
    dk = gain.shape[-1]
    heads = width // dk
    prefix, reduce_j = _hgrn_rec_constants(ts, dk)
    kern = functools.partial(_hgrn_rec_kernel, ts=ts, layer=layer, q_scale=float(dk) ** -0.5)
    consts = (g, w_in, lb_logits, gain, prefix, reduce_j)
    return pl.pallas_call(
        kern, grid=(seq // ts,),
        in_specs=[pl.BlockSpec((bsz, ts, d), lambda s: (0, s, 0))]
        + [_const_spec(c.shape) for c in consts],
        out_specs=pl.BlockSpec((bsz, 1, ts, width), lambda s: (0, 0, s, 0)),
        out_shape=jax.ShapeDtypeStruct((bsz, 1, seq, width), BF16),
        scratch_shapes=[pltpu.VMEM((bsz * heads, dk, dk), F32),
                        pltpu.VMEM((ts, dk), F32),
                        pltpu.VMEM((ts, dk), F32),
                        pltpu.VMEM((ts, A_SUB * dk), BF16),
                        pltpu.VMEM((bsz * heads, ts, ts), F32)],
        compiler_params=_params(1), name="hgrn_mixer")(x, *consts)


def _rope_tables_lanes(pos_row, inv_freq_col, half):
    ang = inv_freq_col * pos_row.astype(F32)
    cos, sin = jnp.cos(ang), jnp.sin(ang)
    zeros = jnp.zeros((LANES - 2 * half, ang.shape[1]), F32)
    c_tab = jnp.transpose(jnp.concatenate([cos, cos, zeros], axis=0))
    s_tab = jnp.transpose(jnp.concatenate([-sin, sin, zeros], axis=0))
    return c_tab, s_tab


def _kv_kernel(x_ref, pos_ref, g_ref, wdc_ref, wdr_ref, gl_ref, wk_ref, wvt_ref, freq_ref,
               kn_out, kr_out, vt_out, *, heads, nope, vdim, half):
    h = _rms(x_ref[0], g_ref[...]).astype(BF16)
    c_kv = _rms(_dot(h, wdc_ref[...]), gl_ref[...]).astype(BF16)
    kr = _dot(h, wdr_ref[...])
    c_tab, s_tab = _rope_tables_lanes(pos_ref[0], freq_ref[...], half)
    kr_out[0] = (kr * c_tab + pltpu.roll(kr, half, 1) * s_tab).astype(BF16)
    k_nope = _dot(c_kv, wk_ref[...])
    v_t = _dot_nt(wvt_ref[...], c_kv)
    ones = jnp.ones((vt_out.shape[2] - vdim, v_t.shape[1]), BF16)
    for hd in range(heads):
        kn_out[0, hd] = k_nope[:, hd * nope:(hd + 1) * nope].astype(BF16)
        vt_out[0, hd, 0:vdim, :] = v_t[hd * vdim:(hd + 1) * vdim].astype(BF16)
        vt_out[0, hd, vdim:, :] = ones


def _kv_call(x, pos_row, g, wd_c, wd_r, gl, wk, wvt, freq_col, *, heads, nope, vdim, half,
             tm=PROJ_TILE):
    bsz, seq, d = x.shape
    kern = functools.partial(_kv_kernel, heads=heads, nope=nope, vdim=vdim, half=half)
    vrows = vdim + BF16_SUBLANES
    return pl.pallas_call(
        kern, grid=(bsz, seq // tm),
        in_specs=[pl.BlockSpec((1, tm, d), lambda b, s: (b, s, 0)),
                  pl.BlockSpec((1, 1, tm), lambda b, s: (b, 0, s)),
                  _const_spec(g.shape), _const_spec(wd_c.shape), _const_spec(wd_r.shape),
                  _const_spec(gl.shape), _const_spec(wk.shape), _const_spec(wvt.shape),
                  _const_spec(freq_col.shape)],
        out_specs=[pl.BlockSpec((1, heads, tm, nope), lambda b, s: (b, 0, s, 0)),
                   pl.BlockSpec((1, tm, LANES), lambda b, s: (b, s, 0)),
                   pl.BlockSpec((1, heads, vrows, tm), lambda b, s: (b, 0, 0, s))],
        out_shape=[jax.ShapeDtypeStruct((bsz, heads, seq, nope), BF16),
                   jax.ShapeDtypeStruct((bsz, seq, LANES), BF16),
                   jax.ShapeDtypeStruct((bsz, heads, vrows, seq), BF16)],
        compiler_params=_params(2), name="mla_shared_kv")(
            x, pos_row, g, wd_c, wd_r, gl, wk, wvt, freq_col)


def _q_kernel(x_ref, pos_ref, g_ref, wdq_ref, gq_ref, wuqt_ref, freq_ref, qt_out,
              *, heads, nope, half, qd, scale):
    h = _rms(x_ref[0], g_ref[...]).astype(BF16)
    c_q = (_rms(_dot(h, wdq_ref[...]), gq_ref[...]) * scale).astype(BF16)
    q_t = _dot_nt(wuqt_ref[...], c_q)
    ang = freq_ref[...] * pos_ref[0].astype(F32)
    cos, sin = jnp.cos(ang), jnp.sin(ang)
    zeros = jnp.zeros((qd - nope - 2 * half, q_t.shape[1]), F32)
    for hd in range(heads):
        base = hd * (nope + 2 * half)
        x1 = q_t[base + nope:base + nope + half]
        x2 = q_t[base + nope + half:base + nope + 2 * half]
        full = jnp.concatenate(
            [q_t[base:base + nope], x1 * cos - x2 * sin, x2 * cos + x1 * sin, zeros], axis=0)
        qt_out[0, hd] = full.astype(BF16)


def _q_call(x, pos_row, g, w_dq, gq, w_uqt, freq_col, *, heads, nope, half, qd, scale,
            tm=PROJ_TILE):
    bsz, seq, d = x.shape
    kern = functools.partial(_q_kernel, heads=heads, nope=nope, half=half, qd=qd, scale=scale)
    return pl.pallas_call(
        kern, grid=(bsz, seq // tm),
        in_specs=[pl.BlockSpec((1, tm, d), lambda b, s: (b, s, 0)),
                  pl.BlockSpec((1, 1, tm), lambda b, s: (b, 0, s)),
                  _const_spec(g.shape), _const_spec(w_dq.shape), _const_spec(gq.shape),
                  _const_spec(w_uqt.shape), _const_spec(freq_col.shape)],
        out_specs=pl.BlockSpec((1, heads, qd, tm), lambda b, s: (b, 0, 0, s)),
        out_shape=jax.ShapeDtypeStruct((bsz, heads, qd, seq), BF16),
        compiler_params=_params(2), name="mla_q")(x, pos_row, g, w_dq, gq, w_uqt, freq_col)


def _attn_kernel(qt_ref, kn_ref, kr_ref, vt_ref, o_ref, s0, s1, p0, p1, acc_ref, *, tq, tk, dv):
    for hd in range(qt_ref.shape[1]):
        for qi in range(qt_ref.shape[3] // tq):
            cols = pl.ds(qi * tq, tq)
            _attn_head(qt_ref.at[0, hd, :, cols], kn_ref.at[0, hd], kr_ref.at[0], vt_ref.at[0, hd],
                       o_ref.at[0, hd, :, cols], s0, s1, p0, p1, acc_ref, qi, tq=tq, tk=tk, dv=dv)


def _attn_head(qt_ref, kn_ref, kr_ref, vt_ref, o_ref, s0, s1, p0, p1, acc_ref, qi, *, tq, tk, dv):

    def key_block(j):
        rows = pl.ds(pl.multiple_of(j * tk, tk), tk)
        return jnp.concatenate([kn_ref[rows, :], kr_ref[rows, :]], axis=1)

    def value_block(j):
        return vt_ref[:, pl.ds(pl.multiple_of(j * tk, tk), tk)]

    def scores(j, s_out):
        s = _dot(key_block(j), qt_ref[...])
        s_out[...] = s
        return jnp.max(s, axis=0, keepdims=True)

    def probabilities(s, m, m_blk):
        m_new = jnp.maximum(m, m_blk)
        return m_new, jnp.exp2(m - m_new), jnp.exp2((s - m_new).astype(BF16))

    def softmax_step(s_in, p_out, m, m_blk):
        m_new, alpha, p = probabilities(s_in[...], m, m_blk)
        p_out[...] = p
        return m_new, alpha

    def accumulate(j, p_in, alpha):
        acc_ref[...] = alpha * acc_ref[...] + _dot(value_block(j), p_in[...])

    def body(j, carry):
        a_prev, m, mb0 = carry
        mb1 = scores(2 * j + 1, s1)
        m, a0 = softmax_step(s0, p0, m, mb0)
        accumulate(jnp.maximum(2 * j - 1, 0), p1, a_prev)
        mb0 = scores(2 * j + 2, s0)
        m, a1 = softmax_step(s1, p1, m, mb1)
        accumulate(2 * j, p0, a0)
        return a1, m, mb0

    n_diag = tq // tk
    n_full = n_diag * qi
    acc_ref[...] = jnp.zeros_like(acc_ref)
    m, mb0 = jnp.full((1, tq), NEG_BIG, F32), scores(0, s0)
    if n_full:
        p1[...] = jnp.zeros_like(p1)
        a_prev, m, _ = lax.fori_loop(0, n_full // 2, body, (jnp.ones((1, tq), F32), m, mb0))
        accumulate(n_full - 1, p1, a_prev)

    causal = (lax.broadcasted_iota(jnp.int32, (tk, tk), 0)
              <= lax.broadcasted_iota(jnp.int32, (tk, tk), 1))
    for d in range(n_diag):
        lo = d * tk
        s = s0[...] if d == 0 else _dot(key_block(n_full + d), qt_ref[:, lo:])
        masked = jnp.where(causal, s[:, :tk], NEG_BIG)
        s = jnp.concatenate([masked, s[:, tk:]], axis=1) if lo + tk < tq else masked
        m_d, a_d, p_d = probabilities(s, m[:, lo:], jnp.max(s, axis=0, keepdims=True))
        m = jnp.concatenate([m[:, :lo], m_d], axis=1) if lo else m_d
        acc_ref[:, lo:] = a_d * acc_ref[:, lo:] + _dot(value_block(n_full + d), p_d)
    acc = acc_ref[...]
    o_ref[...] = (acc[:dv] * (1.0 / acc[dv:dv + 1])).astype(BF16)


def _attn_call(qt, k_nope, k_rope, vt, *, tq=ATTN_Q_TILE, tk=ATTN_K_TILE, hps=1):
    assert (tq // tk) % 2 == 0
    bsz, heads, qd, seq = qt.shape
    vrows = vt.shape[2]
    dv = vrows - BF16_SUBLANES
    assert k_nope.shape[3] + k_rope.shape[2] == qd
    kern = functools.partial(_attn_kernel, tq=tq, tk=tk, dv=dv)
    return pl.pallas_call(
        kern, grid=(bsz, heads // hps, 1),
        in_specs=[pl.BlockSpec((1, hps, qd, seq), lambda b, h, i: (b, h, 0, 0)),
                  pl.BlockSpec((1, hps, seq, k_nope.shape[3]), lambda b, h, i: (b, h, 0, 0)),
                  pl.BlockSpec((1, seq, k_rope.shape[2]), lambda b, h, i: (b, 0, 0)),
                  pl.BlockSpec((1, hps, vrows, seq), lambda b, h, i: (b, h, 0, 0))],
        out_specs=pl.BlockSpec((1, hps, dv, seq), lambda b, h, i: (b, h, 0, 0)),
        out_shape=jax.ShapeDtypeStruct((bsz, heads, dv, seq), BF16),
        scratch_shapes=[pltpu.VMEM((tk, tq), F32), pltpu.VMEM((tk, tq), F32),
                        pltpu.VMEM((tk, tq), BF16), pltpu.VMEM((tk, tq), BF16),
                        pltpu.VMEM((vrows, tq), F32)],
        compiler_params=_params(3), name="mla_attention")(qt, k_nope, k_rope, vt)


def kernel(x, p, positions, norm_gains, ffn_w_in, ffn_w_out, ple_w_gate, ple_w_in, a_w_in,
           a_lb_logits, a_out_gain, a_w_out, kv_norm_in, kv_w_down, kv_latent_norm, kv_w_up,
           b_w_dq, b_q_norm, b_w_uq, b_w_out, final_norm):
    bsz, seq, d = x.shape
    depth = norm_gains.shape[0]
    n_a = a_w_in.shape[0]
    t = bsz * seq
    bf = lambda w: w.astype(BF16)
    row = lambda g: g.reshape(1, -1).astype(F32)

    kv_lora = kv_latent_norm.shape[0]
    rope = kv_w_down.shape[1] - kv_lora
    half = rope // 2
    heads = (b_w_uq.shape[2] + b_w_out.shape[1] - kv_w_up.shape[1]) // rope
    vdim = b_w_out.shape[1] // heads
    nope = kv_w_up.shape[1] // heads - vdim
    qd = nope + LANES
    assert nope == LANES and vdim == LANES and rope <= LANES // 2

    inv_freq = 1.0 / (ROPE_THETA ** (jnp.arange(0, rope, 2, dtype=F32) / rope))
    freq_col = inv_freq.reshape(half, 1)
    pos_row = positions.reshape(bsz, 1, seq)

    wd_c = bf(kv_w_down[:, :kv_lora])
    wd_r = bf(jnp.concatenate([kv_w_down[:, kv_lora:]] * (LANES // rope), axis=1))
    w_up = kv_w_up.reshape(kv_lora, heads, nope + vdim)
    wk = bf(w_up[:, :, :nope].reshape(kv_lora, heads * nope))
    wvt = bf(w_up[:, :, nope:].reshape(kv_lora, heads * vdim).T)

    def shared_kv(xs):
        return _kv_call(xs.reshape(bsz, seq, d), pos_row, row(kv_norm_in), wd_c, wd_r,
                        row(kv_latent_norm), wk, wvt, freq_col,
                        heads=heads, nope=nope, vdim=vdim, half=half)

    ffn_in, ffn_out = bf(ffn_w_in), bf(ffn_w_out)
    ple_gate, ple_in = bf(ple_w_gate), bf(ple_w_in)
    gains = norm_gains.astype(F32).reshape(depth, norm_gains.shape[1], 1, d)
    p_tokens = p.reshape(depth, t, -1)

    xf = x.reshape(t, d)
    shared = shared_kv(xf) if n_a == 0 else None
    attn_scale = float(nope + rope) ** -0.5 * LOG2_E
    for li in range(depth):
        x1 = _block_call(xf, (gains, (li, 0)), (ffn_in, (li, 0)), (ffn_out, (li, 0)))
        if li < n_a:
            o = _hgrn_mixer_call(x1.reshape(bsz, seq, d), row(norm_gains[li, 1]), bf(a_w_in[li]),
                                 a_lb_logits.astype(F32), row(a_out_gain[li]), layer=li)
            pre = (o, bf(a_w_out[li]), "groups")
        else:
            bi = li - n_a
            w_uqt = bf(b_w_uq[bi].T)
            qt = _q_call(x1.reshape(bsz, seq, d), pos_row, row(norm_gains[li, 1]),
                         bf(b_w_dq[bi]), row(b_q_norm[bi]), w_uqt, freq_col, heads=heads,
                         nope=nope, half=half, qd=qd, scale=attn_scale)
            o = _attn_call(qt, *shared)
            pre = (o, bf(b_w_out[bi]), "transposed")
        ple = ((gains, (li, 3)), (ple_gate, (li,)), (p_tokens, li), (ple_in, (li,)))
        final = row(final_norm) if li == depth - 1 else None
        xf = _block_call(x1, (gains, (li, 2)), (ffn_in, (li, 1)), (ffn_out, (li, 1)),
                         pre=pre, ple=ple, final=final)
        if li == n_a - 1:
            shared = shared_kv(xf)
    return xf.reshape(bsz, seq, d)
```

```python
import functools
import math

import jax
import jax.numpy as jnp
from jax import lax
from jax.experimental import pallas as pl
from jax.experimental.pallas import tpu as pltpu

F32 = jnp.float32
BF16 = jnp.bfloat16

EPS = 1e-6
ROPE_THETA = 10000.0
A_CHUNK = 64
A_SUB = 32
A_SAFE_LOG2_DECAY = 96.0
NEG_BIG = -1e30
LOG2_E = math.log2(math.e)

V7X_VMEM_BYTES = 64 * 1024 * 1024
VMEM_LIMIT = V7X_VMEM_BYTES * 7 // 8
LANES = 128
F32_SUBLANES = 8
BF16_SUBLANES = 16

TOKEN_TILE = 512
WEIGHT_CHUNKS = 8
PROJ_TILE = 1024
HGRN_BLOCK = 256
ATTN_Q_TILE, ATTN_K_TILE = 2048, 512


def _rms(x, g):
    ms = jnp.mean(x * x, axis=-1, keepdims=True)
    return x * lax.rsqrt(ms + EPS) * g


def _sigmoid(x):
    return 1.0 / (1.0 + jnp.exp(-x))


def _dot(a, b):
    return jnp.dot(a, b, preferred_element_type=F32)


def _dot_nt(a, b):
    return lax.dot_general(a, b, (((1,), (1,)), ((), ())), preferred_element_type=F32)


def _dot_tn(a, b):
    return lax.dot_general(a, b, (((0,), (0,)), ((), ())), preferred_element_type=F32)


def _const_spec(shape):
    nd = len(shape)
    return pl.BlockSpec(shape, lambda *_: (0,) * nd, pipeline_mode=pl.Buffered(1))


def _pick(w):
    arr, lead = w if isinstance(w, tuple) else (w, ())
    rest = arr.shape[len(lead):]
    index = tuple(lead) + (0,) * len(rest)
    spec = pl.BlockSpec((None,) * len(lead) + rest, lambda *_: index,
                        pipeline_mode=pl.Buffered(1))
    return arr, spec


def _params(n_axes):
    return pltpu.CompilerParams(
        dimension_semantics=("arbitrary",) * n_axes, vmem_limit_bytes=VMEM_LIMIT)


def _fetch_as_bf16(hbm, lead, dst, stage, sem):
    rows = stage.shape[1]
    n = dst.shape[0] // rows

    def copy(c):
        return pltpu.make_async_copy(hbm.at[(*lead, pl.ds(c * rows, rows))],
                                     stage.at[c % 2], sem.at[c % 2])

    copy(0).start()
    for c in range(n):
        if c + 1 < n:
            copy(c + 1).start()
        copy(c).wait()
        dst[c * rows:(c + 1) * rows] = stage[c % 2].astype(BF16)


def _block_kernel(*refs, d_ff, pre_layout, has_ple, has_final, w_lead):
    it = iter(refs)
    x_ref = next(it)
    if pre_layout:
        y_ref, wpre_ref = next(it), next(it)
    g_ref, win_hbm, wout_hbm = next(it), next(it), next(it)
    if has_ple:
        gp_ref, wg_ref, p_ref, wp_ref = next(it), next(it), next(it), next(it)
    if has_final:
        gf_ref = next(it)
    o_ref = next(it)
    win_ref, wout_ref, stage_in, stage_out, sem_in, sem_out = (next(it) for _ in range(6))

    @pl.when(pl.program_id(0) == 0)
    def _():
        _fetch_as_bf16(win_hbm, w_lead[0], win_ref, stage_in, sem_in)
        _fetch_as_bf16(wout_hbm, w_lead[1], wout_ref, stage_out, sem_out)

    x = x_ref[...]
    if pre_layout == "groups":
        y = jnp.concatenate([y_ref[0, gi] for gi in range(y_ref.shape[1])], axis=1)
        x = x + _dot(y, wpre_ref[...])
    elif pre_layout == "transposed":
        n_g, gw, tm = y_ref.shape[1:]
        x = x + _dot_tn(y_ref[0].reshape(n_g * gw, tm), wpre_ref[...])
    h = _rms(x, g_ref[...]).astype(BF16)
    gate = _dot(h, win_ref[:, :d_ff])
    up = _dot(h, win_ref[:, d_ff:])
    act = (gate * _sigmoid(gate) * up).astype(BF16)
    x = x + 0.5 * _dot(act, wout_ref[...])
    if has_ple:
        hg = _rms(x, gp_ref[...]).astype(BF16)
        emb_gate = _sigmoid(_dot(hg, wg_ref[...]))
        emb = _dot(p_ref[...].astype(BF16), wp_ref[...])
        x = x + emb_gate * emb
    if has_final:
        x = _rms(x, gf_ref[...])
    o_ref[...] = x


def _block_call(x, g, w_in, w_out, *, pre=None, ple=None, final=None, tm=TOKEN_TILE):
    t, d = x.shape
    row = lambda i: (i, 0)
    args, specs = [x], [pl.BlockSpec((tm, d), row)]

    def add(w):
        arr, spec = _pick(w)
        args.append(arr)
        specs.append(spec)
        return spec.block_shape

    pre_layout = None
    if pre is not None:
        y, w_pre, pre_layout = pre
        args.append(y)
        steps = (t // y.shape[0]) // tm
        if pre_layout == "groups":
            specs.append(pl.BlockSpec((1, y.shape[1], tm, y.shape[3]),
                                      lambda i: (i // steps, 0, i % steps, 0)))
        else:
            specs.append(pl.BlockSpec((1, y.shape[1], y.shape[2], tm),
                                      lambda i: (i // steps, 0, 0, i % steps)))
        add(w_pre)
    add(g)
    (w_in_arr, in_lead), (w_out_arr, out_lead) = w_in, w_out
    args += [w_in_arr, w_out_arr]
    specs += [pl.BlockSpec(memory_space=pl.ANY), pl.BlockSpec(memory_space=pl.ANY)]
    d_ff = w_out_arr.shape[-2]
    if ple is not None:
        gp, wg, p, wp = ple
        add(gp)
        add(wg)
        args.append(p[0])
        specs.append(pl.BlockSpec((None, tm, p[0].shape[2]), lambda i: (p[1], i, 0)))
        add(wp)
    if final is not None:
        add(final)
    kern = functools.partial(_block_kernel, d_ff=d_ff, pre_layout=pre_layout,
                             has_ple=ple is not None, has_final=final is not None,
                             w_lead=(tuple(in_lead), tuple(out_lead)))
    in_rows, in_cols = w_in_arr.shape[-2:]
    out_rows, out_cols = w_out_arr.shape[-2:]
    return pl.pallas_call(
        kern, grid=(t // tm,), in_specs=specs, out_specs=pl.BlockSpec((tm, d), row),
        out_shape=jax.ShapeDtypeStruct((t, d), F32),
        scratch_shapes=[pltpu.VMEM((in_rows, in_cols), BF16), pltpu.VMEM((out_rows, out_cols), BF16),
                        pltpu.VMEM((2, in_rows // WEIGHT_CHUNKS, in_cols), F32),
                        pltpu.VMEM((2, out_rows // WEIGHT_CHUNKS, out_cols), F32),
                        pltpu.SemaphoreType.DMA((2,)), pltpu.SemaphoreType.DMA((2,))],
        compiler_params=_params(1), name="token_block")(*args)


def _hgrn_rec_constants(ts, dk):
    incl = jnp.arange(ts)[None, :] <= jnp.arange(ts)[:, None]
    prefix = jnp.concatenate([incl, incl, incl], axis=1).astype(BF16)
    col = jnp.arange(ts)[None, :]
    reduce_j = (jnp.arange(A_SUB * dk)[:, None] // dk) == (col % A_SUB)
    return prefix, reduce_j.astype(BF16)


def _hgrn_rec_kernel(x_ref, g_ref, w_ref, lbl_ref, gain_ref, prefix_ref, reduce_ref,
                     o_ref, st_ref, k_scr, bs_scr, e_scr, a_scr, *, ts, layer, q_scale):
    @pl.when(pl.program_id(0) == 0)
    def _():
        st_ref[...] = jnp.zeros_like(st_ref)

    nb, dk = x_ref.shape[0], gain_ref.shape[-1]
    n_heads = st_ref.shape[0] // nb
    width = n_heads * dk
    chains = [(b * n_heads + hd, slice(b * ts, (b + 1) * ts), slice(hd * dk, (hd + 1) * dk))
              for b in range(nb) for hd in range(n_heads)]

    h = _rms(x_ref[...].reshape(nb * ts, -1), g_ref[...]).astype(BF16)
    q = _dot(h, w_ref[:, 0 * width:1 * width]) * q_scale
    f = _dot(h, w_ref[:, 1 * width:2 * width])
    v = _dot(h, w_ref[:, 2 * width:3 * width]).astype(BF16)
    out_gate = _sigmoid(_dot(h, w_ref[:, 3 * width:4 * width]))
    lbl = lbl_ref[...]
    e = jnp.exp(lbl - jnp.max(lbl, axis=0, keepdims=True))
    sm = e / jnp.sum(e, axis=0, keepdims=True)
    lb = jnp.sum(sm[:layer + 1], axis=0, keepdims=True)
    k = (1.0 - lb) * _sigmoid(-f)
    g = jnp.log(lb + (1.0 - lb) * _sigmoid(f)) * LOG2_E

    g1 = g.astype(BF16)
    r1 = g - g1.astype(F32)
    g2 = r1.astype(BF16)
    g3 = (r1 - g2.astype(F32)).astype(BF16)
    big_b = [_dot(prefix_ref[...], jnp.concatenate([g1[rows], g2[rows], g3[rows]], axis=0))
             for rows in (slice(b * ts, (b + 1) * ts) for b in range(nb))]
    parts = [_hgrn_decays(q[rows, ln], k[rows, ln], big_b[i // n_heads][:, ln], ts=ts)
             for i, rows, ln in chains]
    steepest = functools.reduce(jnp.maximum, [jnp.max(-pt["b_sub"]) for pt in parts])

    @pl.when(steepest <= A_SAFE_LOG2_DECAY)
    def _():
        for (i, _, _), pt in zip(chains, parts):
            k_up = (pt["k"] * jnp.exp2(-pt["b_sub"])).astype(BF16)
            a_scr[i] = _dot_nt(pt["q_sub"], k_up)

    @pl.when(steepest > A_SAFE_LOG2_DECAY)
    def _():
        for (i, _, _), pt in zip(chains, parts):
            a_scr[i] = _hgrn_exact_sub_scores(pt["q"], pt["k"], pt["b_sub"], reduce_ref[...],
                                               k_scr, bs_scr, e_scr, ts=ts)

    for (i, rows, ln), pt in zip(chains, parts):
        o, st_ref[i] = _hgrn_outputs(pt, v[rows, ln], st_ref[i], a_scr[i], ts=ts)
        o_ref[i // n_heads, 0, :, ln] = (_rms(o, gain_ref[...]) * out_gate[rows, ln]).astype(BF16)


def _hgrn_decays(q, k, big_b, *, ts):
    C, c = A_CHUNK, A_SUB
    nc, dk = ts // C, q.shape[-1]
    b_last = big_b[ts - 1:ts]
    after = b_last - big_b

    def since_start_of(size):
        groups = [big_b[:size]] + [big_b[lo:lo + size] - big_b[lo - 1:lo]
                                   for lo in range(size, ts, size)]
        return jnp.concatenate(groups, axis=0)

    b_chunk = since_start_of(C)
    b_sub = since_start_of(c)
    to_chunk_end = jnp.concatenate(
        [big_b[lo + C - 1:lo + C] - big_b[lo:lo + C] for lo in range(0, ts, C)],
        axis=0).reshape(nc, C, dk)

    return dict(
        q=q, k=k, b_last=b_last, after=after, b_sub=b_sub, to_chunk_end=to_chunk_end,
        q_blk=(q * jnp.exp2(big_b)).astype(BF16),
        k_blk=(k * jnp.exp2(after)).astype(BF16),
        q_chunk=(q * jnp.exp2(b_chunk)).astype(BF16),
        q_sub=(q * jnp.exp2(b_sub)).astype(BF16))


def _hgrn_outputs(pt, v, st, a_rep, *, ts):
    C, c = A_CHUNK, A_SUB
    nc, nsub = ts // C, C // c
    k, after, to_chunk_end = pt["k"], pt["after"], pt["to_chunk_end"]
    q_chunk, q_sub = pt["q_chunk"], pt["q_sub"]
    dk = k.shape[-1]

    zeros2 = lambda n: jnp.zeros((n, dk), BF16)
    qs_parts, ks_parts = [], []
    for n in range(1, nc):
        lo, hi = n * C, (n + 1) * C
        decay = jnp.exp2(after[:lo] - after[lo - 1:lo])
        ks_parts.append(jnp.concatenate([(k[:lo] * decay).astype(BF16), zeros2(ts - lo)], axis=0))
        pieces = [zeros2(lo), q_chunk[lo:hi]] + ([zeros2(ts - hi)] if hi < ts else [])
        qs_parts.append(jnp.concatenate(pieces, axis=0))
    a_chunks = _dot_nt(jnp.concatenate(qs_parts, axis=1), jnp.concatenate(ks_parts, axis=1))

    k3 = k.reshape(nc, C, dk)
    qs3 = q_sub.reshape(nc, C, dk)
    zeros3 = lambda n: jnp.zeros((nc, n, dk), BF16)
    qs_parts, ks_parts = [], []
    for i in range(1, nsub):
        lo, hi = i * c, (i + 1) * c
        decay = jnp.exp2(to_chunk_end[:, :lo] - to_chunk_end[:, lo - 1:lo])
        ks_parts.append(jnp.concatenate([(k3[:, :lo] * decay).astype(BF16), zeros3(C - lo)], axis=1))
        pieces = [zeros3(lo), qs3[:, lo:hi]] + ([zeros3(C - hi)] if hi < C else [])
        qs_parts.append(jnp.concatenate(pieces, axis=1))
    a_subs = _dot_nt(jnp.concatenate(qs_parts, axis=2).reshape(ts, (nsub - 1) * dk),
                     jnp.concatenate(ks_parts, axis=2).reshape(ts, (nsub - 1) * dk))

    rr = lax.broadcasted_iota(jnp.int32, (ts, ts), 0)
    ll = lax.broadcasted_iota(jnp.int32, (ts, ts), 1)
    lc, ls = C.bit_length() - 1, c.bit_length() - 1
    same_chunk = (rr >> lc) == (ll >> lc)
    own_sub = ((rr >> ls) == (ll >> ls)) & ((ll & (c - 1)) <= (rr & (c - 1)))
    a = jnp.where(own_sub, a_rep, 0.0) + jnp.where(same_chunk, a_subs, 0.0) + a_chunks

    o = _dot(a.astype(BF16), v) + _dot_nt(pt["q_blk"], st.astype(BF16))
    return o, st * jnp.exp2(pt["b_last"]) + _dot_tn(v, pt["k_blk"])


def _hgrn_exact_sub_scores(q, k, b_sub, reduce_j, k_scr, bs_scr, e_scr, *, ts):
    c, dk = A_SUB, q.shape[-1]
    k_scr[...] = k
    bs_scr[...] = b_sub
    zero_tile = jnp.zeros((F32_SUBLANES, dk), F32)
    for n in range(ts // c):
        tiles = [slice(n * c + lo, n * c + lo + F32_SUBLANES) for lo in range(0, c, F32_SUBLANES)]
        for j in range(c):
            row = slice(n * c + j, n * c + j + 1)
            k_j, b_j = k_scr[row, :], bs_scr[row, :]
            e = [q[t] * k_j * jnp.exp2(jnp.minimum(b_sub[t] - b_j, 0.0))
                 if t.stop > n * c + j else zero_tile for t in tiles]
            e_scr[n * c:(n + 1) * c, j * dk:(j + 1) * dk] = jnp.concatenate(e, axis=0).astype(BF16)
    return _dot(e_scr[...], reduce_j)


def _hgrn_mixer_call(x, g, w_in, lb_logits, gain, *, layer, ts=HGRN_BLOCK):
    bsz, seq, d = x.shape
    width = w_in.shape[1] // 4
    dk = gain.shape[-1]
    heads = width // dk
    prefix, reduce_j = _hgrn_rec_constants(ts, dk)
    kern = functools.partial(_hgrn_rec_kernel, ts=ts, layer=layer, q_scale=float(dk) ** -0.5)
    consts = (g, w_in, lb_logits, gain, prefix, reduce_j)
    return pl.pallas_call(
        kern, grid=(seq // ts,),
        in_specs=[pl.BlockSpec((bsz, ts, d), lambda s: (0, s, 0))]
        + [_const_spec(c.shape) for c in consts],
        out_specs=pl.BlockSpec((bsz, 1, ts, width), lambda s: (0, 0, s, 0)),
        out_shape=jax.ShapeDtypeStruct((bsz, 1, seq, width), BF16),
        scratch_shapes=[pltpu.VMEM((bsz * heads, dk, dk), F32),
                        pltpu.VMEM((ts, dk), F32),
                        pltpu.VMEM((ts, dk), F32),
                        pltpu.VMEM((ts, A_SUB * dk), BF16),
                        pltpu.VMEM((bsz * heads, ts, ts), F32)],
        compiler_params=_params(1), name="hgrn_mixer")(x, *consts)


def _rope_tables_lanes(pos_row, inv_freq_col, half):
    ang = inv_freq_col * pos_row.astype(F32)
    cos, sin = jnp.cos(ang), jnp.sin(ang)
    zeros = jnp.zeros((LANES - 2 * half, ang.shape[1]), F32)
    c_tab = jnp.transpose(jnp.concatenate([cos, cos, zeros], axis=0))
    s_tab = jnp.transpose(jnp.concatenate([-sin, sin, zeros], axis=0))
    return c_tab, s_tab


def _kv_kernel(x_ref, pos_ref, g_ref, wdc_ref, wdr_ref, gl_ref, wk_ref, wvt_ref, freq_ref,
               kn_out, kr_out, vt_out, *, heads, nope, vdim, half):
    h = _rms(x_ref[0], g_ref[...]).astype(BF16)
    c_kv = _rms(_dot(h, wdc_ref[...]), gl_ref[...]).astype(BF16)
    kr = _dot(h, wdr_ref[...])
    c_tab, s_tab = _rope_tables_lanes(pos_ref[0], freq_ref[...], half)
    kr_out[0] = (kr * c_tab + pltpu.roll(kr, half, 1) * s_tab).astype(BF16)
    k_nope = _dot(c_kv, wk_ref[...])
    v_t = _dot_nt(wvt_ref[...], c_kv)
    ones = jnp.ones((vt_out.shape[2] - vdim, v_t.shape[1]), BF16)
    for hd in range(heads):
        kn_out[0, hd] = k_nope[:, hd * nope:(hd + 1) * nope].astype(BF16)
        vt_out[0, hd, 0:vdim, :] = v_t[hd * vdim:(hd + 1) * vdim].astype(BF16)
        vt_out[0, hd, vdim:, :] = ones


def _kv_call(x, pos_row, g, wd_c, wd_r, gl, wk, wvt, freq_col, *, heads, nope, vdim, half,
             tm=PROJ_TILE):
    bsz, seq, d = x.shape
    kern = functools.partial(_kv_kernel, heads=heads, nope=nope, vdim=vdim, half=half)
    vrows = vdim + BF16_SUBLANES
    return pl.pallas_call(
        kern, grid=(bsz, seq // tm),
        in_specs=[pl.BlockSpec((1, tm, d), lambda b, s: (b, s, 0)),
                  pl.BlockSpec((1, 1, tm), lambda b, s: (b, 0, s)),
                  _const_spec(g.shape), _const_spec(wd_c.shape), _const_spec(wd_r.shape),
                  _const_spec(gl.shape), _const_spec(wk.shape), _const_spec(wvt.shape),
                  _const_spec(freq_col.shape)],
        out_specs=[pl.BlockSpec((1, heads, tm, nope), lambda b, s: (b, 0, s, 0)),
                   pl.BlockSpec((1, tm, LANES), lambda b, s: (b, s, 0)),
                   pl.BlockSpec((1, heads, vrows, tm), lambda b, s: (b, 0, 0, s))],
        out_shape=[jax.ShapeDtypeStruct((bsz, heads, seq, nope), BF16),
                   jax.ShapeDtypeStruct((bsz, seq, LANES), BF16),
                   jax.ShapeDtypeStruct((bsz, heads, vrows, seq), BF16)],
        compiler_params=_params(2), name="mla_shared_kv")(
            x, pos_row, g, wd_c, wd_r, gl, wk, wvt, freq_col)


def _q_kernel(x_ref, pos_ref, g_ref, wdq_ref, gq_ref, wuqt_ref, freq_ref, qt_out,
              *, heads, nope, half, qd, scale):
    h = _rms(x_ref[0], g_ref[...]).astype(BF16)
    c_q = (_rms(_dot(h, wdq_ref[...]), gq_ref[...]) * scale).astype(BF16)
    q_t = _dot_nt(wuqt_ref[...], c_q)
    ang = freq_ref[...] * pos_ref[0].astype(F32)
    cos, sin = jnp.cos(ang), jnp.sin(ang)
    zeros = jnp.zeros((qd - nope - 2 * half, q_t.shape[1]), F32)
    for hd in range(heads):
        base = hd * (nope + 2 * half)
        x1 = q_t[base + nope:base + nope + half]
        x2 = q_t[base + nope + half:base + nope + 2 * half]
        full = jnp.concatenate(
            [q_t[base:base + nope], x1 * cos - x2 * sin, x2 * cos + x1 * sin, zeros], axis=0)
        qt_out[0, hd] = full.astype(BF16)


def _q_call(x, pos_row, g, w_dq, gq, w_uqt, freq_col, *, heads, nope, half, qd, scale,
            tm=PROJ_TILE):
    bsz, seq, d = x.shape
    kern = functools.partial(_q_kernel, heads=heads, nope=nope, half=half, qd=qd, scale=scale)
    return pl.pallas_call(
        kern, grid=(bsz, seq // tm),
        in_specs=[pl.BlockSpec((1, tm, d), lambda b, s: (b, s, 0)),
                  pl.BlockSpec((1, 1, tm), lambda b, s: (b, 0, s)),
                  _const_spec(g.shape), _const_spec(w_dq.shape), _const_spec(gq.shape),
                  _const_spec(w_uqt.shape), _const_spec(freq_col.shape)],
        out_specs=pl.BlockSpec((1, heads, qd, tm), lambda b, s: (b, 0, 0, s)),
        out_shape=jax.ShapeDtypeStruct((bsz, heads, qd, seq), BF16),
        compiler_params=_params(2), name="mla_q")(x, pos_row, g, w_dq, gq, w_uqt, freq_col)


def _attn_kernel(qt_ref, kn_ref, kr_ref, vt_ref, o_ref, s0, s1, p0, p1, acc_ref, *, tq, tk, dv):
    for hd in range(qt_ref.shape[1]):
        for qi in range(qt_ref.shape[3] // tq):
            cols = pl.ds(qi * tq, tq)
            _attn_head(qt_ref.at[0, hd, :, cols], kn_ref.at[0, hd], kr_ref.at[0], vt_ref.at[0, hd],
                       o_ref.at[0, hd, :, cols], s0, s1, p0, p1, acc_ref, qi, tq=tq, tk=tk, dv=dv)


def _attn_head(qt_ref, kn_ref, kr_ref, vt_ref, o_ref, s0, s1, p0, p1, acc_ref, qi, *, tq, tk, dv):

    def key_block(j):
        rows = pl.ds(pl.multiple_of(j * tk, tk), tk)
        return jnp.concatenate([kn_ref[rows, :], kr_ref[rows, :]], axis=1)

    def value_block(j):
        return vt_ref[:, pl.ds(pl.multiple_of(j * tk, tk), tk)]

    def scores(j, s_out):
        s = _dot(key_block(j), qt_ref[...])
        s_out[...] = s
        return jnp.max(s, axis=0, keepdims=True)

    def probabilities(s, m, m_blk):
        m_new = jnp.maximum(m, m_blk)
        return m_new, jnp.exp2(m - m_new), jnp.exp2((s - m_new).astype(BF16))

    def softmax_step(s_in, p_out, m, m_blk):
        m_new, alpha, p = probabilities(s_in[...], m, m_blk)
        p_out[...] = p
        return m_new, alpha

    def accumulate(j, p_in, alpha):
        acc_ref[...] = alpha * acc_ref[...] + _dot(value_block(j), p_in[...])

    def body(j, carry):
        a_prev, m, mb0 = carry
        mb1 = scores(2 * j + 1, s1)
        m, a0 = softmax_step(s0, p0, m, mb0)
        accumulate(jnp.maximum(2 * j - 1, 0), p1, a_prev)
        mb0 = scores(2 * j + 2, s0)
        m, a1 = softmax_step(s1, p1, m, mb1)
        accumulate(2 * j, p0, a0)
        return a1, m, mb0

    n_diag = tq // tk
    n_full = n_diag * qi
    acc_ref[...] = jnp.zeros_like(acc_ref)
    m, mb0 = jnp.full((1, tq), NEG_BIG, F32), scores(0, s0)
    if n_full:
        p1[...] = jnp.zeros_like(p1)
        a_prev, m, _ = lax.fori_loop(0, n_full // 2, body, (jnp.ones((1, tq), F32), m, mb0))
        accumulate(n_full - 1, p1, a_prev)

    causal = (lax.broadcasted_iota(jnp.int32, (tk, tk), 0)
              <= lax.broadcasted_iota(jnp.int32, (tk, tk), 1))
    for d in range(n_diag):
        lo = d * tk
        s = s0[...] if d == 0 else _dot(key_block(n_full + d), qt_ref[:, lo:])
        masked = jnp.where(causal, s[:, :tk], NEG_BIG)
        s = jnp.concatenate([masked, s[:, tk:]], axis=1) if lo + tk < tq else masked
        m_d, a_d, p_d = probabilities(s, m[:, lo:], jnp.max(s, axis=0, keepdims=True))
        m = jnp.concatenate([m[:, :lo], m_d], axis=1) if lo else m_d
        acc_ref[:, lo:] = a_d * acc_ref[:, lo:] + _dot(value_block(n_full + d), p_d)
    acc = acc_ref[...]
    o_ref[...] = (acc[:dv] * (1.0 / acc[dv:dv + 1])).astype(BF16)


def _attn_call(qt, k_nope, k_rope, vt, *, tq=ATTN_Q_TILE, tk=ATTN_K_TILE, hps=1):
    assert (tq // tk) % 2 == 0
    bsz, heads, qd, seq = qt.shape
    vrows = vt.shape[2]
    dv = vrows - BF16_SUBLANES
    assert k_nope.shape[3] + k_rope.shape[2] == qd
    kern = functools.partial(_attn_kernel, tq=tq, tk=tk, dv=dv)
    return pl.pallas_call(
        kern, grid=(bsz, heads // hps, 1),
        in_specs=[pl.BlockSpec((1, hps, qd, seq), lambda b, h, i: (b, h, 0, 0)),
                  pl.BlockSpec((1, hps, seq, k_nope.shape[3]), lambda b, h, i: (b, h, 0, 0)),
                  pl.BlockSpec((1, seq, k_rope.shape[2]), lambda b, h, i: (b, 0, 0)),
                  pl.BlockSpec((1, hps, vrows, seq), lambda b, h, i: (b, h, 0, 0))],
        out_specs=pl.BlockSpec((1, hps, dv, seq), lambda b, h, i: (b, h, 0, 0)),
        out_shape=jax.ShapeDtypeStruct((bsz, heads, dv, seq), BF16),
        scratch_shapes=[pltpu.VMEM((tk, tq), F32), pltpu.VMEM((tk, tq), F32),
                        pltpu.VMEM((tk, tq), BF16), pltpu.VMEM((tk, tq), BF16),
                        pltpu.VMEM((vrows, tq), F32)],
        compiler_params=_params(3), name="mla_attention")(qt, k_nope, k_rope, vt)


def kernel(x, p, positions, norm_gains, ffn_w_in, ffn_w_out, ple_w_gate, ple_w_in, a_w_in,
           a_lb_logits, a_out_gain, a_w_out, kv_norm_in, kv_w_down, kv_latent_norm, kv_w_up,
           b_w_dq, b_q_norm, b_w_uq, b_w_out, final_norm):
    bsz, seq, d = x.shape
    depth = norm_gains.shape[0]
    n_a = a_w_in.shape[0]
    t = bsz * seq
    bf = lambda w: w.astype(BF16)
    row = lambda g: g.reshape(1, -1).astype(F32)

    kv_lora = kv_latent_norm.shape[0]
    rope = kv_w_down.shape[1] - kv_lora
    half = rope // 2
    heads = (b_w_uq.shape[2] + b_w_out.shape[1] - kv_w_up.shape[1]) // rope
    vdim = b_w_out.shape[1] // heads
    nope = kv_w_up.shape[1] // heads - vdim
    qd = nope + LANES
    assert nope == LANES and vdim == LANES and rope <= LANES // 2

    inv_freq = 1.0 / (ROPE_THETA ** (jnp.arange(0, rope, 2, dtype=F32) / rope))
    freq_col = inv_freq.reshape(half, 1)
    pos_row = positions.reshape(bsz, 1, seq)

    wd_c = bf(kv_w_down[:, :kv_lora])
    wd_r = bf(jnp.concatenate([kv_w_down[:, kv_lora:]] * (LANES // rope), axis=1))
    w_up = kv_w_up.reshape(kv_lora, heads, nope + vdim)
    wk = bf(w_up[:, :, :nope].reshape(kv_lora, heads * nope))
    wvt = bf(w_up[:, :, nope:].reshape(kv_lora, heads * vdim).T)

    def shared_kv(xs):
        return _kv_call(xs.reshape(bsz, seq, d), pos_row, row(kv_norm_in), wd_c, wd_r,
                        row(kv_latent_norm), wk, wvt, freq_col,
                        heads=heads, nope=nope, vdim=vdim, half=half)

    ffn_in, ffn_out = ffn_w_in.astype(F32), ffn_w_out.astype(F32)
    ple_gate, ple_in = bf(ple_w_gate), bf(ple_w_in)
    gains = norm_gains.astype(F32).reshape(depth, norm_gains.shape[1], 1, d)
    p_tokens = p.reshape(depth, t, -1)

    xf = x.reshape(t, d)
    shared = shared_kv(xf) if n_a == 0 else None
    attn_scale = float(nope + rope) ** -0.5 * LOG2_E
    for li in range(depth):
        x1 = _block_call(xf, (gains, (li, 0)), (ffn_in, (li, 0)), (ffn_out, (li, 0)))
        if li < n_a:
            o = _hgrn_mixer_call(x1.reshape(bsz, seq, d), row(norm_gains[li, 1]), bf(a_w_in[li]),
                                 a_lb_logits.astype(F32), row(a_out_gain[li]), layer=li)
            pre = (o, bf(a_w_out[li]), "groups")
        else:
            bi = li - n_a
            w_uqt = bf(b_w_uq[bi].T)
            qt = _q_call(x1.reshape(bsz, seq, d), pos_row, row(norm_gains[li, 1]),
                         bf(b_w_dq[bi]), row(b_q_norm[bi]), w_uqt, freq_col, heads=heads,
                         nope=nope, half=half, qd=qd, scale=attn_scale)
            o = _attn_call(qt, *shared)
            pre = (o, bf(b_w_out[bi]), "transposed")
        ple = ((gains, (li, 3)), (ple_gate, (li,)), (p_tokens, li), (ple_in, (li,)))
        final = row(final_norm) if li == depth - 1 else None
        xf = _block_call(x1, (gains, (li, 2)), (ffn_in, (li, 1)), (ffn_out, (li, 1)),
                         pre=pre, ple=ple, final=final)
        if li == n_a - 1:
            shared = shared_kv(xf)
    return xf.reshape(bsz, seq, d)
```
